```python
import math
import jax, jax.numpy as jnp
from jax import lax
import numpy as np


D_MODEL = 2048
BATCH = 8
SEQ = 2048
DEPTH = 2

GRID_W = 64
CTX_LEN = 256
HEAD_DIM = 128
D_CONV = 1536
D_SSM = 512
CONV_HEADS = D_CONV // HEAD_DIM
SSM_HEADS = D_SSM // HEAD_DIM
CONV_W = 3
SSM_GROUP = 16
N_SSM_GROUPS = D_SSM // SSM_GROUP
SSM_STATE = 64
D_IN_PROJ = 3 * D_CONV + D_SSM
N_EXPERTS = 16
D_EXPERT = 2048
EC_FACTOR = 2
N_MOD = 6
EPS = 1e-6
DT_MIN = 0.001
DT_MAX = 0.1

kernel_name = 'hybrid_conv_s5_ec_moe_diffusion_trunk'


def rmsnorm(x, g):
    xf = x.astype(jnp.float32)
    y = xf * lax.rsqrt(jnp.mean(xf * xf, axis=-1, keepdims=True) + EPS)
    return (y * g.astype(jnp.float32)).astype(x.dtype)


def head_rmsnorm(y, g, n_heads):
    shp = y.shape
    yf = y.astype(jnp.float32).reshape(shp[:-1] + (n_heads, shp[-1] // n_heads))
    yf = yf * lax.rsqrt(jnp.mean(yf * yf, axis=-1, keepdims=True) + EPS)
    return (yf.reshape(shp) * g.astype(jnp.float32)).astype(y.dtype)


def modulate(h, shift, scale):
    return h * (1.0 + scale) + shift


def centred_conv3(v, w, axis):
    n = v.shape[axis]
    pad = [(0, 0)] * v.ndim
    pad[axis] = (1, 1)
    vp = jnp.pad(v, pad)
    out = w[0] * lax.slice_in_dim(vp, 0, n, axis=axis)
    for k in range(1, CONV_W):
        out = out + w[k] * lax.slice_in_dim(vp, k, k + n, axis=axis)
    return out


def conv_mixer(p, w_conv, on_grid):
    b_g, c_g, v = jnp.split(p, 3, axis=-1)
    z = c_g * v
    if on_grid:
        bsz, n, ch = z.shape
        rows = n // GRID_W
        z = centred_conv3(z.reshape(bsz, rows, GRID_W, ch), w_conv, axis=2).reshape(bsz, n, ch)
    else:
        z = centred_conv3(z, w_conv, axis=1)
    return b_g * z


def _complex_affine_combine(e1, e2):
    ar1, ai1, br1, bi1 = e1
    ar2, ai2, br2, bi2 = e2
    return (ar2 * ar1 - ai2 * ai1,
            ar2 * ai1 + ai2 * ar1,
            ar2 * br1 - ai2 * bi1 + br2,
            ar2 * bi1 + ai2 * br1 + bi2)


def s5_discretise(lam_re, lam_im, log_dt, b_re, b_im):
    lr = lam_re.astype(jnp.float32)
    li = lam_im.astype(jnp.float32)
    dt = jnp.exp(log_dt.astype(jnp.float32))[:, None]
    mag = jnp.exp(lr * dt)
    ar = mag * jnp.cos(li * dt)
    ai = mag * jnp.sin(li * dt)
    den = lr * lr + li * li
    nr = ar - 1.0
    kr = (nr * lr + ai * li) / den
    ki = (ai * lr - nr * li) / den
    br = b_re.astype(jnp.float32)
    bi = b_im.astype(jnp.float32)
    bbr = kr[..., None] * br - ki[..., None] * bi
    bbi = kr[..., None] * bi + ki[..., None] * br
    return dt, lr, li, ar, ai, bbr, bbi


def s5_scan(u, disc, h0, reverse):
    dt, lr, li, ar, ai, bbr, bbi = disc
    if reverse:
        u = jnp.flip(u, axis=1)
    n = u.shape[1]
    bu_re = jnp.einsum('blgh,gph->lbgp', u, bbr)
    bu_im = jnp.einsum('blgh,gph->lbgp', u, bbi)
    a_shape = (n, 1) + ar.shape
    a_re = jnp.broadcast_to(ar, a_shape)
    a_im = jnp.broadcast_to(ai, a_shape)
    _, _, h_re, h_im = lax.associative_scan(_complex_affine_combine, (a_re, a_im, bu_re, bu_im), axis=0)
    if h0 is not None:
        h0r, h0i = h0
        t = jnp.arange(1, n + 1, dtype=jnp.float32)[:, None, None, None]
        pm = jnp.exp(t * dt * lr)
        pr = pm * jnp.cos(t * dt * li)
        pi = pm * jnp.sin(t * dt * li)
        h_re = h_re + pr * h0r[None] - pi * h0i[None]
        h_im = h_im + pr * h0i[None] + pi * h0r[None]
    return (h_re, h_im), (h_re[-1], h_im[-1])


def s5_readout(h, c_re, c_im, reverse):
    h_re, h_im = h
    y = (jnp.einsum('lbgp,ghp->blgh', h_re, c_re.astype(jnp.float32))
         - jnp.einsum('lbgp,ghp->blgh', h_im, c_im.astype(jnp.float32)))
    if reverse:
        y = jnp.flip(y, axis=1)
    return y


def s5_mixer(u_ctx, u_lat, lam_re, lam_im, log_dt, b_re, b_im, c_re, c_im, d_skip, w_glu, ctx_out):
    dtype = u_lat.dtype

    def groups(u):
        return u.astype(jnp.float32).reshape(u.shape[0], u.shape[1], N_SSM_GROUPS, SSM_GROUP)

    uc = groups(u_ctx)
    ul = groups(u_lat)
    d = d_skip.astype(jnp.float32).reshape(N_SSM_GROUPS, SSM_GROUP)
    y_lat = d * ul
    y_ctx = d * uc if ctx_out else None
    for direction in range(2):
        rev = direction == 1
        disc = s5_discretise(lam_re[direction], lam_im[direction], log_dt[direction],
                             b_re[direction], b_im[direction])
        h_c, fin_c = s5_scan(uc, disc, None, rev)
        h_l, _ = s5_scan(ul, disc, fin_c, rev)
        y_lat = y_lat + s5_readout(h_l, c_re[direction], c_im[direction], rev)
        if ctx_out:
            y_ctx = y_ctx + s5_readout(h_c, c_re[direction], c_im[direction], rev)

    def glu(y):
        y = jax.nn.gelu(y.reshape(y.shape[0], y.shape[1], D_SSM))
        return (y * jax.nn.sigmoid(y @ w_glu.astype(jnp.float32))).astype(dtype)

    return glu(y_lat), (glu(y_ctx) if ctx_out else None)


def ec_moe(h, router_w, w_gate, w_up, w_down):
    bsz, n, dm = h.shape
    cap = EC_FACTOR * n // N_EXPERTS
    logits = jnp.einsum('bnd,de->bne', h, router_w).astype(jnp.float32)
    aff = jnp.swapaxes(jax.nn.softmax(logits, axis=-1), 1, 2)
    vals, idx = lax.top_k(aff, cap)
    xs = jax.vmap(lambda hb, ib: hb[ib])(h, idx)
    g = jnp.einsum('becd,edf->becf', xs, w_gate)
    u = jnp.einsum('becd,edf->becf', xs, w_up)
    y = jnp.einsum('becf,efd->becd', jax.nn.silu(g) * u, w_down)
    y = y * vals[..., None].astype(y.dtype)
    return jax.vmap(lambda yb, ib: jnp.zeros((n, dm), yb.dtype).at[ib.reshape(-1)].add(yb.reshape(-1, dm)))(y, idx)


def setup_inputs(seed: int = 0) -> dict:
    key = jax.random.key(seed)
    ks = jax.random.split(key, 32)
    f32 = jnp.float32

    def nrm(k, shape, scale):
        return jax.random.normal(k, shape, f32) * scale

    G, P, H = N_SSM_GROUPS, SSM_STATE, SSM_GROUP
    lam_im_base = jnp.pi * jnp.arange(P, dtype=f32)
    return {
        'x': nrm(ks[0], (BATCH, SEQ, D_MODEL), 1.0),
        'c': nrm(ks[1], (BATCH, D_MODEL), 1.0),
        'ctx': nrm(ks[2], (BATCH, CTX_LEN, D_MODEL), 1.0),
        'c_ctx': nrm(ks[3], (D_MODEL,), 1.0),
        'ada_w': nrm(ks[4], (DEPTH, D_MODEL, N_MOD * D_MODEL), 0.5 * D_MODEL ** -0.5),
        'ada_b': nrm(ks[5], (DEPTH, N_MOD * D_MODEL), 0.02),
        'norm1_g': 1.0 + nrm(ks[6], (DEPTH, D_MODEL), 0.02),
        'w_in': nrm(ks[7], (DEPTH, D_MODEL, D_IN_PROJ), D_MODEL ** -0.5),
        'conv_w': nrm(ks[8], (DEPTH, CONV_W, D_CONV), CONV_W ** -0.5),
        'ssm_lam_re': -0.5 + nrm(ks[9], (DEPTH, 2, G, P), 0.01),
        'ssm_lam_im': lam_im_base + nrm(ks[10], (DEPTH, 2, G, P), 0.01),
        'ssm_log_dt': jax.random.uniform(ks[11], (DEPTH, 2, G), f32, math.log(DT_MIN), math.log(DT_MAX)),
        'ssm_b_re': nrm(ks[12], (DEPTH, 2, G, P, H), (2.0 * H) ** -0.5),
        'ssm_b_im': nrm(ks[13], (DEPTH, 2, G, P, H), (2.0 * H) ** -0.5),
        'ssm_c_re': nrm(ks[14], (DEPTH, 2, G, H, P), (2.0 * P) ** -0.5),
        'ssm_c_im': nrm(ks[15], (DEPTH, 2, G, H, P), (2.0 * P) ** -0.5),
        'ssm_d': nrm(ks[16], (DEPTH, D_SSM), 1.0),
        'ssm_w_glu': nrm(ks[17], (DEPTH, D_SSM, D_SSM), D_SSM ** -0.5),
        'out_norm_conv_g': 1.0 + nrm(ks[18], (DEPTH, D_CONV), 0.02),
        'out_norm_ssm_g': 1.0 + nrm(ks[19], (DEPTH, D_SSM), 0.02),
        'w_out': nrm(ks[20], (DEPTH, D_CONV + D_SSM, D_MODEL), (D_CONV + D_SSM) ** -0.5),
        'norm2_g': 1.0 + nrm(ks[21], (DEPTH, D_MODEL), 0.02),
        'router_w': nrm(ks[22], (DEPTH, D_MODEL, N_EXPERTS), D_MODEL ** -0.5),
        'exp_w_gate': nrm(ks[23], (DEPTH, N_EXPERTS, D_MODEL, D_EXPERT), D_MODEL ** -0.5),
        'exp_w_up': nrm(ks[24], (DEPTH, N_EXPERTS, D_MODEL, D_EXPERT), D_MODEL ** -0.5),
        'exp_w_down': nrm(ks[25], (DEPTH, N_EXPERTS, D_EXPERT, D_MODEL), D_EXPERT ** -0.5),
        'final_norm_g': 1.0 + nrm(ks[26], (D_MODEL,), 0.02),
    }


def reference(x, c, ctx, c_ctx, ada_w, ada_b, norm1_g, w_in, conv_w, ssm_lam_re, ssm_lam_im,
              ssm_log_dt, ssm_b_re, ssm_b_im, ssm_c_re, ssm_c_im, ssm_d, ssm_w_glu,
              out_norm_conv_g, out_norm_ssm_g, w_out, norm2_g, router_w, exp_w_gate, exp_w_up,
              exp_w_down, final_norm_g):
    xl = x
    xc = ctx
    sc = jax.nn.silu(c)
    scc = jax.nn.silu(c_ctx)
    for l in range(DEPTH):
        last = l == DEPTH - 1
        mod_l = (sc @ ada_w[l] + ada_b[l])[:, None, :]
        mod_c = scc @ ada_w[l] + ada_b[l]
        sh1, sc1, g1, sh2, sc2, g2 = jnp.split(mod_l, N_MOD, axis=-1)
        csh1, csc1, cg1, csh2, csc2, cg2 = jnp.split(mod_c, N_MOD, axis=-1)

        h = modulate(rmsnorm(xl, norm1_g[l]), sh1, sc1)
        hc = modulate(rmsnorm(xc, norm1_g[l]), csh1, csc1)
        proj = h @ w_in[l]
        if last:
            u_ctx = hc @ w_in[l][:, 3 * D_CONV:]
        else:
            proj_c = hc @ w_in[l]
            u_ctx = proj_c[..., 3 * D_CONV:]
        conv_l = conv_mixer(proj[..., :3 * D_CONV], conv_w[l], on_grid=True)
        ssm_l, ssm_c = s5_mixer(u_ctx, proj[..., 3 * D_CONV:], ssm_lam_re[l], ssm_lam_im[l], ssm_log_dt[l],
                                ssm_b_re[l], ssm_b_im[l], ssm_c_re[l], ssm_c_im[l], ssm_d[l], ssm_w_glu[l],
                                ctx_out=not last)
        mix_l = jnp.concatenate([head_rmsnorm(conv_l, out_norm_conv_g[l], CONV_HEADS),
                                 head_rmsnorm(ssm_l, out_norm_ssm_g[l], SSM_HEADS)], axis=-1) @ w_out[l]
        xl = xl + g1 * mix_l
        h2 = modulate(rmsnorm(xl, norm2_g[l]), sh2, sc2)
        xl = xl + g2 * ec_moe(h2, router_w[l], exp_w_gate[l], exp_w_up[l], exp_w_down[l])

        if not last:
            conv_c = conv_mixer(proj_c[..., :3 * D_CONV], conv_w[l], on_grid=False)
            mix_c = jnp.concatenate([head_rmsnorm(conv_c, out_norm_conv_g[l], CONV_HEADS),
                                     head_rmsnorm(ssm_c, out_norm_ssm_g[l], SSM_HEADS)], axis=-1) @ w_out[l]
            xc = xc + cg1 * mix_c
            hc2 = modulate(rmsnorm(xc, norm2_g[l]), csh2, csc2)
            xc = xc + cg2 * ec_moe(hc2, router_w[l], exp_w_gate[l], exp_w_up[l], exp_w_down[l])
    return rmsnorm(xl, final_norm_g)
```

```python
import functools

import jax
import jax.numpy as jnp
from jax import lax
from jax.experimental import pallas as pl
from jax.experimental.pallas import tpu as pltpu

F32 = jnp.float32
BF16 = jnp.bfloat16
I32 = jnp.int32

EPS = 1e-6
GRID_W = 64
HEAD_DIM = 128
SSM_GROUP = 16
EC_FACTOR = 2
N_MOD = 6
CHUNK = 16
LANES = 128
CHUNK_W = CHUNK * SSM_GROUP
SCAN_BLOCK = 16

_MIB = 1 << 20


def _params(n_axes, vmem_mib):
    return pltpu.CompilerParams(dimension_semantics=("arbitrary",) * n_axes,
                                vmem_limit_bytes=vmem_mib * _MIB)


def _split_bf16(a):
    hi = a.astype(BF16)
    lo = (a - hi.astype(F32)).astype(BF16)
    return hi, lo


_NN = (((1,), (0,)), ((), ()))
_NT = (((1,), (1,)), ((), ()))


def _dot(a, b, dims=_NN):
    return lax.dot_general(a, b, dims, preferred_element_type=F32)


def _dot3(a, b, dims=_NN):
    ah, al = _split_bf16(a)
    bh, bl = _split_bf16(b)
    return _dot(ah, bh, dims) + _dot(ah, bl, dims) + _dot(al, bh, dims)


def _rms(x):
    return x * lax.rsqrt(jnp.mean(x * x, axis=-1, keepdims=True) + EPS)


def _ada_body(c_ref, w_ref, b_ref, o_ref):
    s = jax.nn.silu(c_ref[...])
    o_ref[...] = _dot3(s, w_ref[...]) + b_ref[...]


def _ada(cvec, ada_w, ada_b):
    depth, d, n6 = ada_w.shape
    tn = next(t for t in (1024, 512, 256, 128) if n6 % t == 0)
    rows = cvec.shape[0]
    return pl.pallas_call(
        _ada_body,
        out_shape=jax.ShapeDtypeStruct((depth, rows, n6), F32),
        grid=(depth, n6 // tn),
        in_specs=[
            pl.BlockSpec((rows, d), lambda l, j: (0, 0)),
            pl.BlockSpec((None, d, tn), lambda l, j: (l, 0, j)),
            pl.BlockSpec((None, 1, tn), lambda l, j: (l, 0, j)),
        ],
        out_specs=pl.BlockSpec((None, rows, tn), lambda l, j: (l, 0, j)),
        compiler_params=_params(2, 40),
        name="ada",
    )(cvec, ada_w, ada_b.reshape(depth, 1, n6))


def _inproj_body(x_ref, mod_ref, g_ref, w_ref, cw_ref, gc_ref, conv_ref, u_ref, *, rowlen, d_conv, cn):
    x = x_ref[...]
    tm = x.shape[0]
    h = _rms(x) * g_ref[...]
    h = h * (1.0 + mod_ref[1:2, :]) + mod_ref[0:1, :]
    hb = h.astype(BF16)
    t = jnp.bitwise_and(lax.broadcasted_iota(I32, (tm, 1), 0), rowlen - 1)
    first = t == 0
    last = t == rowlen - 1
    for j in range(d_conv // cn):
        c0 = j * cn
        bg = _dot(hb, w_ref[:, c0:c0 + cn])
        cg = _dot(hb, w_ref[:, d_conv + c0:d_conv + c0 + cn])
        v = _dot(hb, w_ref[:, 2 * d_conv + c0:2 * d_conv + c0 + cn])
        z = cg * v
        zp = jnp.where(first, 0.0, pltpu.roll(z, 1, 0))
        zn = jnp.where(last, 0.0, pltpu.roll(z, tm - 1, 0))
        cw = cw_ref[:, c0:c0 + cn]
        y = bg * (cw[0:1, :] * zp + cw[1:2, :] * z + cw[2:3, :] * zn)
        for hd in range(cn // HEAD_DIM):
            lo = hd * HEAD_DIM
            yh = _rms(y[:, lo:lo + HEAD_DIM]) * gc_ref[:, c0 + lo:c0 + lo + HEAD_DIM]
            conv_ref[:, c0 + lo:c0 + lo + HEAD_DIM] = yh.astype(conv_ref.dtype)
    u_ref[...] = _dot(hb, w_ref[:, 3 * d_conv:])


def _inproj(x, mod4, l, mod_row, norm_g, w_bf, conv_w, gc, *, rowlen, tm):
    nb, n, d = x.shape
    depth, _, d_in = w_bf.shape
    d_conv = conv_w.shape[-1]
    d_ssm = d_in - 3 * d_conv
    cn = min(512, d_conv)
    body = functools.partial(_inproj_body, rowlen=rowlen, d_conv=d_conv, cn=cn)
    return pl.pallas_call(
        body,
        out_shape=(jax.ShapeDtypeStruct((nb, n, d_conv), BF16),
                   jax.ShapeDtypeStruct((nb, n, d_ssm), F32)),
        grid=(nb, n // tm),
        in_specs=[
            pl.BlockSpec((None, tm, d), lambda b, i: (b, i, 0)),
            pl.BlockSpec((None, None, N_MOD, d), lambda b, i: (l, mod_row(b), 0, 0)),
            pl.BlockSpec((None, 1, d), lambda b, i: (l, 0, 0)),
            pl.BlockSpec((None, d, d_in), lambda b, i: (l, 0, 0), pipeline_mode=pl.Buffered(1)),
            pl.BlockSpec((None, 3, d_conv), lambda b, i: (l, 0, 0)),
            pl.BlockSpec((None, 1, d_conv), lambda b, i: (l, 0, 0)),
        ],
        out_specs=(pl.BlockSpec((None, tm, d_conv), lambda b, i: (b, i, 0)),
                   pl.BlockSpec((None, tm, d_ssm), lambda b, i: (b, i, 0))),
        compiler_params=_params(2, 56),
        name="inproj",
    )(x, mod4, norm_g, w_bf, conv_w, gc)


def _uproj_body(x_ref, mod_ref, g_ref, w_ref, u_ref):
    h = _rms(x_ref[...]) * g_ref[...]
    h = h * (1.0 + mod_ref[1:2, :]) + mod_ref[0:1, :]
    u_ref[...] = _dot(h.astype(BF16), w_ref[...])


def _uproj(x, mod4, l, mod_row, norm_g, w_bf, d_ssm, *, tm):
    nb, n, d = x.shape
    d_in = w_bf.shape[-1]
    col_blk = (d_in - d_ssm) // d_ssm
    return pl.pallas_call(
        _uproj_body,
        out_shape=jax.ShapeDtypeStruct((nb, n, d_ssm), F32),
        grid=(nb, n // tm),
        in_specs=[
            pl.BlockSpec((None, tm, d), lambda b, i: (b, i, 0)),
            pl.BlockSpec((None, None, N_MOD, d), lambda b, i: (l, mod_row(b), 0, 0)),
            pl.BlockSpec((None, 1, d), lambda b, i: (l, 0, 0)),
            pl.BlockSpec((None, d, d_ssm), lambda b, i: (l, 0, col_blk)),
        ],
        out_specs=pl.BlockSpec((None, tm, d_ssm), lambda b, i: (b, i, 0)),
        compiler_params=_params(2, 32),
        name="uproj",
    )(x, mod4, norm_g, w_bf)


def _s5prep_body(lr_ref, li_ref, ldt_ref, bc1_ref, bc2_ref, ca_ref, cb_ref, dv_ref,
                 wcat_ref, mtot_ref, v_ref, a1_ref, a2_ref):
    t_ = CHUNK
    kk = lax.broadcasted_iota(I32, (24, 1), 0).astype(F32)
    lane_blk = jnp.right_shift(lax.broadcasted_iota(I32, (SSM_GROUP, CHUNK_W), 1), 4)
    lane = lax.broadcasted_iota(I32, (1, LANES), 1)
    sgn = jnp.where(lane < LANES // 2, -1.0, 1.0)
    eye = jnp.where(lax.broadcasted_iota(I32, (LANES, LANES), 0)
                    == lax.broadcasted_iota(I32, (LANES, LANES), 1), 1.0, 0.0)
    mtot = jnp.zeros((CHUNK_W, CHUNK_W), F32)
    for d in range(2):
        lr = lr_ref[d]
        li = li_ref[d]
        dt = jnp.exp(ldt_ref[d])
        pm = jnp.exp(kk * (dt * lr))
        ang = kk * (dt * li)
        pr = pm * jnp.cos(ang)
        pi = pm * jnp.sin(ang)
        ar = pr[1:2, :]
        ai = pi[1:2, :]
        den = lr * lr + li * li
        nr = ar - 1.0
        kr = (nr * lr + ai * li) / den
        ki = (ai * lr - nr * li) / den
        bc1 = bc1_ref[d]
        bc2 = bc2_ref[d]
        bb1 = kr * bc1 + ki * bc2
        bb2 = kr * bc2 - ki * bc1
        ca = ca_ref[d]
        cb = cb_ref[d]
        cak = [ca * pr[k:k + 1, :] + cb * pi[k:k + 1, :] for k in range(t_ + 1)]
        lag_order = range(t_) if d == 0 else range(t_ - 1, -1, -1)
        cak_all = jnp.concatenate([cak[k] for k in lag_order], axis=0)
        kall_t = _dot3(bb1, cak_all, _NT)
        rows = []
        for j in range(t_):
            if d == 0:
                shift, keep = (SSM_GROUP * j) % CHUNK_W, lane_blk >= j
            else:
                shift, keep = (SSM_GROUP * (j + 1)) % CHUNK_W, lane_blk <= j
            r = pltpu.roll(kall_t, shift, 1) if shift else kall_t
            rows.append(jnp.where(keep, r, 0.0))
        mtot = mtot + jnp.concatenate(rows, axis=0)
        e_v = [i + 1 for i in range(t_)] if d == 0 else [t_ - i for i in range(t_)]
        v_t = jnp.concatenate([cak[e] for e in e_v], axis=0)
        v_ref[d] = _dot3(eye, v_t, _NT).astype(v_ref.dtype)
        e_w = [t_ - 1 - j for j in range(t_)] if d == 0 else list(range(t_))
        w = jnp.concatenate([bb1 * pr[e:e + 1, :] + bb2 * pi[e:e + 1, :] for e in e_w], axis=0)
        wcat_ref[:, d * LANES:(d + 1) * LANES] = w.astype(wcat_ref.dtype)
        a1_ref[d] = jnp.broadcast_to(pr[t_:t_ + 1, :], (8, LANES))
        a2_ref[d] = jnp.broadcast_to(sgn * pi[t_:t_ + 1, :], (8, LANES))
    diag = (lax.broadcasted_iota(I32, (CHUNK_W, CHUNK_W), 0)
            == lax.broadcasted_iota(I32, (CHUNK_W, CHUNK_W), 1))
    mtot = mtot + jnp.where(diag, dv_ref[...], 0.0)
    mtot_ref[...] = mtot.astype(mtot_ref.dtype)


def _s5prep(l, lr_t, li_t, ldt, bc1, bc2, ca, cb, dv):
    g = lr_t.shape[2]
    spec5 = lambda r, c: pl.BlockSpec((None, 2, None, r, c), lambda i: (l, 0, i, 0, 0))
    return pl.pallas_call(
        _s5prep_body,
        out_shape=(jax.ShapeDtypeStruct((g, CHUNK_W, 2 * LANES), BF16),
                   jax.ShapeDtypeStruct((g, CHUNK_W, CHUNK_W), BF16),
                   jax.ShapeDtypeStruct((2, g, LANES, CHUNK_W), BF16),
                   jax.ShapeDtypeStruct((2, g, 8, LANES), F32),
                   jax.ShapeDtypeStruct((2, g, 8, LANES), F32)),
        grid=(g,),
        in_specs=[spec5(1, LANES), spec5(1, LANES), spec5(1, 1),
                  spec5(SSM_GROUP, LANES), spec5(SSM_GROUP, LANES),
                  spec5(SSM_GROUP, LANES), spec5(SSM_GROUP, LANES),
                  pl.BlockSpec((None, None, 1, CHUNK_W), lambda i: (l, i, 0, 0))],
        out_specs=(pl.BlockSpec((None, CHUNK_W, 2 * LANES), lambda i: (i, 0, 0)),
                   pl.BlockSpec((None, CHUNK_W, CHUNK_W), lambda i: (i, 0, 0)),
                   pl.BlockSpec((2, None, LANES, CHUNK_W), lambda i: (0, i, 0, 0)),
                   pl.BlockSpec((2, None, 8, LANES), lambda i: (0, i, 0, 0)),
                   pl.BlockSpec((2, None, 8, LANES), lambda i: (0, i, 0, 0))),
        compiler_params=_params(1, 32),
        name="s5prep",
    )(lr_t, li_t, ldt, bc1, bc2, ca, cb, dv)


def _s5local_body(u_ref, w_ref, s_ref):
    s_ref[...] = _dot(u_ref[...], w_ref[...])


def _s5local(u, wcat):
    g, r, _ = u.shape
    return pl.pallas_call(
        _s5local_body,
        out_shape=jax.ShapeDtypeStruct((g, r, 2 * LANES), F32),
        grid=(g,),
        in_specs=[pl.BlockSpec((None, r, CHUNK_W), lambda i: (i, 0, 0)),
                  pl.BlockSpec((None, CHUNK_W, 2 * LANES), lambda i: (i, 0, 0))],
        out_specs=pl.BlockSpec((None, r, 2 * LANES), lambda i: (i, 0, 0)),
        compiler_params=_params(1, 32),
        name="s5local",
    )(u, wcat)


def _s5scan_body(sf_ref, sr_ref, a1_ref, a2_ref, hf_ref, hr_ref, cf_ref, cr_ref):
    @pl.when(pl.program_id(0) == 0)
    def _():
        cf_ref[...] = jnp.zeros_like(cf_ref)
        cr_ref[...] = jnp.zeros_like(cr_ref)

    a1f, a2f = a1_ref[0], a2_ref[0]
    a1r, a2r = a1_ref[1], a2_ref[1]
    hf = cf_ref[...]
    hr = cr_ref[...]
    for k in range(SCAN_BLOCK):
        rf = slice(8 * k, 8 * k + 8)
        rr = slice(8 * (SCAN_BLOCK - 1 - k), 8 * (SCAN_BLOCK - k))
        hf_ref[:, rf, :] = hf
        hr_ref[:, rr, :] = hr
        hf = a1f * hf + a2f * pltpu.roll(hf, LANES // 2, 2) + sf_ref[:, rf, :]
        hr = a1r * hr + a2r * pltpu.roll(hr, LANES // 2, 2) + sr_ref[:, rr, :]
    cf_ref[...] = hf
    cr_ref[...] = hr


def _s5scan(s, a1, a2, nblk_ctx, nblk_lat):
    g, r, _ = s.shape
    nblk = nblk_ctx + nblk_lat
    rb = SCAN_BLOCK * 8

    def rev_blk(i):
        return jnp.where(i < nblk_ctx, nblk_ctx - 1 - i, nblk + nblk_ctx - 1 - i)

    full = pl.BlockSpec((2, g, 8, LANES), lambda i: (0, 0, 0, 0))
    return pl.pallas_call(
        _s5scan_body,
        out_shape=(jax.ShapeDtypeStruct((g, r, LANES), F32),
                   jax.ShapeDtypeStruct((g, r, LANES), F32)),
        grid=(nblk,),
        in_specs=[pl.BlockSpec((g, rb, LANES), lambda i: (0, i, 0)),
                  pl.BlockSpec((g, rb, LANES), lambda i: (0, rev_blk(i), 1)),
                  full, full],
        out_specs=(pl.BlockSpec((g, rb, LANES), lambda i: (0, i, 0)),
                   pl.BlockSpec((g, rb, LANES), lambda i: (0, rev_blk(i), 0))),
        scratch_shapes=[pltpu.VMEM((g, 8, LANES), F32), pltpu.VMEM((g, 8, LANES), F32)],
        compiler_params=_params(1, 32),
        name="s5scan",
    )(s, s, a1, a2)


def _s5out_body(u_ref, hf_ref, hr_ref, m_ref, v_ref, y_ref):
    y = _dot(u_ref[...], m_ref[...])
    y = y + _dot(hf_ref[...].astype(BF16), v_ref[0])
    y = y + _dot(hr_ref[...].astype(BF16), v_ref[1])
    y_ref[...] = y


def _s5out(u, hf, hr, mtot, v):
    g, r, _ = u.shape
    return pl.pallas_call(
        _s5out_body,
        out_shape=jax.ShapeDtypeStruct((g, r, CHUNK_W), F32),
        grid=(g,),
        in_specs=[pl.BlockSpec((None, r, CHUNK_W), lambda i: (i, 0, 0)),
                  pl.BlockSpec((None, r, LANES), lambda i: (i, 0, 0)),
                  pl.BlockSpec((None, r, LANES), lambda i: (i, 0, 0)),
                  pl.BlockSpec((None, CHUNK_W, CHUNK_W), lambda i: (i, 0, 0)),
                  pl.BlockSpec((2, None, LANES, CHUNK_W), lambda i: (0, i, 0, 0))],
        out_specs=pl.BlockSpec((None, r, CHUNK_W), lambda i: (i, 0, 0)),
        compiler_params=_params(1, 32),
        name="s5out",
    )(u, hf, hr, mtot, v)


def _to_chunks(u, g):
    b, n, _ = u.shape
    c = n // CHUNK
    return u.reshape(b, c, CHUNK, g, SSM_GROUP).transpose(3, 1, 0, 2, 4).reshape(g, c * b, CHUNK_W)


def _from_chunks(y, b, g):
    c = y.shape[1] // b
    return (y.reshape(g, c, b, CHUNK, SSM_GROUP).transpose(2, 1, 3, 0, 4)
            .reshape(b, c * CHUNK, g * SSM_GROUP))


def _postmix_body(cn_ref, y_ref, x_ref, mod_ref, wglu_ref, gs_ref, wout_ref, g2_ref, r2_ref, r1_ref,
                  xo_ref, h2_ref, lg_ref, *, d_conv):
    yg = jax.nn.gelu(y_ref[...])
    z = _dot(yg.astype(BF16), wglu_ref[...])
    s = yg * jax.nn.sigmoid(z)
    d_ssm = s.shape[-1]
    mix = _dot(cn_ref[...], wout_ref[0:d_conv, :])
    for hd in range(d_ssm // HEAD_DIM):
        lo = hd * HEAD_DIM
        sn = _rms(s[:, lo:lo + HEAD_DIM]) * gs_ref[:, lo:lo + HEAD_DIM]
        mix = mix + _dot(sn.astype(BF16), wout_ref[d_conv + lo:d_conv + lo + HEAD_DIM, :])
    xn = x_ref[...] + mod_ref[2:3, :] * mix
    xo_ref[...] = xn
    h2 = _rms(xn) * g2_ref[...]
    h2 = h2 * (1.0 + mod_ref[4:5, :]) + mod_ref[3:4, :]
    hi, lo_ = _split_bf16(h2)
    h2_ref[...] = hi
    d1 = _dot(hi, r2_ref[...])
    d2 = _dot(lo_, r1_ref[...])
    lg_ref[...] = d1[:, 0:LANES] + d1[:, LANES:2 * LANES] + d2


def _postmix(conv_n, y_ssm, x, mod4, l, mod_row, wglu_bf, gs, wout_bf, norm2_g, r2, r1, *, tm):
    nb, n, d = x.shape
    d_conv = conv_n.shape[-1]
    d_ssm = y_ssm.shape[-1]
    body = functools.partial(_postmix_body, d_conv=d_conv)
    tok = lambda w: pl.BlockSpec((None, tm, w), lambda b, i: (b, i, 0))
    lay = lambda r, c, **kw: pl.BlockSpec((None, r, c), lambda b, i: (l, 0, 0), **kw)
    return pl.pallas_call(
        body,
        out_shape=(jax.ShapeDtypeStruct((nb, n, d), F32),
                   jax.ShapeDtypeStruct((nb, n, d), BF16),
                   jax.ShapeDtypeStruct((nb, n, LANES), F32)),
        grid=(nb, n // tm),
        in_specs=[tok(d_conv), tok(d_ssm), tok(d),
                  pl.BlockSpec((None, None, N_MOD, d), lambda b, i: (l, mod_row(b), 0, 0)),
                  lay(d_ssm, d_ssm, pipeline_mode=pl.Buffered(1)),
                  lay(1, d_ssm),
                  lay(d_conv + d_ssm, d, pipeline_mode=pl.Buffered(1)),
                  lay(1, d),
                  lay(d, 2 * LANES, pipeline_mode=pl.Buffered(1)),
                  lay(d, LANES, pipeline_mode=pl.Buffered(1))],
        out_specs=(tok(d), tok(d), tok(LANES)),
        compiler_params=_params(2, 56),
        name="postmix",
    )(conv_n, y_ssm, x, mod4, wglu_bf, gs, wout_bf, norm2_g, r2, r1)


def _route_body(lg_ref, pos_ref, wv_ref, tri_ref, *, cap):
    nb, ne, n = lg_ref.shape
    rc = min(256, n)
    for r0 in range(0, n, rc):
        ri = lax.broadcasted_iota(I32, (rc, n), 0) + r0
        ci = lax.broadcasted_iota(I32, (rc, n), 1)
        tri_ref[r0:r0 + rc, :] = jnp.where(ri < ci, 1.0, 0.0).astype(BF16)
    lg = lg_ref[...]
    e = jnp.exp(lg - jnp.max(lg, axis=1, keepdims=True))
    aff = e / jnp.sum(e, axis=1, keepdims=True)
    keys = pltpu.bitcast(aff, I32)
    capf = float(cap)
    thr = jnp.zeros((nb, ne, 1), I32)
    for bit in range(30, -1, -1):
        cand = jnp.bitwise_or(thr, jnp.int32(1 << bit))
        cnt = jnp.sum(jnp.where(keys >= cand, 1.0, 0.0), axis=2, keepdims=True)
        thr = jnp.where(cnt >= capf, cand, thr)
    gt = keys > thr
    eq = keys == thr
    need = capf - jnp.sum(jnp.where(gt, 1.0, 0.0), axis=2, keepdims=True)
    tri = tri_ref[...]
    eq_rank = _dot(jnp.where(eq, 1.0, 0.0).astype(BF16).reshape(nb * ne, n), tri).reshape(nb, ne, n)
    sel = jnp.logical_or(gt, jnp.logical_and(eq, eq_rank < need))
    pos = _dot(jnp.where(sel, 1.0, 0.0).astype(BF16).reshape(nb * ne, n), tri).reshape(nb, ne, n)
    pos_ref[...] = jnp.where(sel, pos.astype(I32), -1)
    wv_ref[...] = jnp.where(sel, aff, 0.0)


def _route(logits_t, cap):
    nb, ne, n = logits_t.shape
    body = functools.partial(_route_body, cap=cap)
    full = pl.BlockSpec((nb, ne, n), lambda i: (0, 0, 0))
    return pl.pallas_call(
        body,
        out_shape=(jax.ShapeDtypeStruct((nb, ne, n), I32),
                   jax.ShapeDtypeStruct((nb, ne, n), F32)),
        grid=(1,),
        in_specs=[full],
        out_specs=(full, full),
        scratch_shapes=[pltpu.VMEM((n, n), BF16)],
        compiler_params=_params(1, 48),
        name="route",
    )(logits_t)


def _onehot_rows(pos_row, cap):
    n = pos_row.shape[-1]
    slot = lax.broadcasted_iota(I32, (cap, n), 0)
    return jnp.where(slot == pos_row, 1.0, 0.0).astype(BF16)


def _gather_body(*refs, nb_lat, cap, with_ctx):
    if with_ctx:
        pos_ref, h_ref, posc_ref, hc_ref, xs_ref = refs
    else:
        pos_ref, h_ref, xs_ref = refs
    b = pl.program_id(0)

    @pl.when(b < nb_lat)
    def _():
        xs_ref[...] = _dot(_onehot_rows(pos_ref[...], cap), h_ref[...]).astype(xs_ref.dtype)

    if with_ctx:
        @pl.when(b == nb_lat)
        def _():
            xs_ref[...] = _dot(_onehot_rows(posc_ref[...], cap), hc_ref[...]).astype(xs_ref.dtype)


def _gather(pos_rows, h2, cap, posc_rows=None, h2c=None):
    nb, ne, _, n = pos_rows.shape
    d = h2.shape[-1]
    with_ctx = posc_rows is not None
    nbt = nb + (1 if with_ctx else 0)
    clampb = lambda b: jnp.minimum(b, nb - 1)
    in_specs = [pl.BlockSpec((None, None, 1, n), lambda b, e: (clampb(b), e, 0, 0)),
                pl.BlockSpec((None, n, d), lambda b, e: (clampb(b), 0, 0))]
    args = [pos_rows, h2]
    if with_ctx:
        in_specs += [pl.BlockSpec((None, 1, n), lambda b, e: (e, 0, 0)),
                     pl.BlockSpec((n, d), lambda b, e: (0, 0))]
        args += [posc_rows, h2c]
    body = functools.partial(_gather_body, nb_lat=nb, cap=cap, with_ctx=with_ctx)
    return pl.pallas_call(
        body,
        out_shape=jax.ShapeDtypeStruct((ne, nbt * cap, d), BF16),
        grid=(nbt, ne),
        in_specs=in_specs,
        out_specs=pl.BlockSpec((None, cap, d), lambda b, e: (e, b, 0)),
        compiler_params=_params(2, 48),
        name="gather",
    )(*args)


def _ffn_body(xs_ref, wg_ref, wu_ref, wd_ref, y_ref, acc_ref):
    f = pl.program_id(2)

    @pl.when(f == 0)
    def _():
        acc_ref[...] = jnp.zeros_like(acc_ref)

    xs = xs_ref[...]
    g = _dot(xs, wg_ref[...].astype(BF16))
    u = _dot(xs, wu_ref[...].astype(BF16))
    act = (jax.nn.silu(g) * u).astype(BF16)
    acc_ref[...] += _dot(act, wd_ref[...].astype(BF16))

    @pl.when(f == pl.num_programs(2) - 1)
    def _():
        y_ref[...] = acc_ref[...].astype(y_ref.dtype)


def _ffn(xs, l, w_gate, w_up, w_down, *, tf, m_split):
    ne, m, d = xs.shape
    dff = w_gate.shape[-1]
    mh = m // m_split
    return pl.pallas_call(
        _ffn_body,
        out_shape=jax.ShapeDtypeStruct((ne, m, d), BF16),
        grid=(m_split, ne, dff // tf),
        in_specs=[pl.BlockSpec((None, mh, d), lambda h, e, f: (e, h, 0)),
                  pl.BlockSpec((None, None, d, tf), lambda h, e, f: (l, e, 0, f)),
                  pl.BlockSpec((None, None, d, tf), lambda h, e, f: (l, e, 0, f)),
                  pl.BlockSpec((None, None, tf, d), lambda h, e, f: (l, e, f, 0))],
        out_specs=pl.BlockSpec((None, mh, d), lambda h, e, f: (e, h, 0)),
        scratch_shapes=[pltpu.VMEM((mh, d), F32)],
        compiler_params=_params(3, 56),
        name="ffn",
    )(xs, w_gate, w_up, w_down)


def _combine_body(*refs, cap, final):
    if final:
        y_ref, pos_ref, wv_ref, x_ref, mod_ref, gf_ref, o_ref = refs
    else:
        y_ref, pos_ref, wv_ref, x_ref, mod_ref, o_ref = refs
    ne = y_ref.shape[0]
    tt = x_ref.shape[0]
    slot = lax.broadcasted_iota(I32, (tt, cap), 1)
    pos = pos_ref[...]
    wv = wv_ref[...]
    pieces = [jnp.where(slot == pos[:, e:e + 1], wv[:, e:e + 1], 0.0).astype(BF16) for e in range(ne)]
    pt = jnp.concatenate(pieces, axis=1)
    y = y_ref[...].reshape(ne * cap, y_ref.shape[-1])
    xn = x_ref[...] + mod_ref[5:6, :] * _dot(pt, y)
    if final:
        xn = _rms(xn) * gf_ref[...]
    o_ref[...] = xn


def _combine(y, blk0, pos_cols, wv_cols, x, mod4, l, mod_row, cap, final_g=None, *, tt):
    nb, n, d = x.shape
    ne = y.shape[0]
    final = final_g is not None
    body = functools.partial(_combine_body, cap=cap, final=final)
    in_specs = [pl.BlockSpec((ne, cap, d), lambda b, i: (0, blk0 + b, 0)),
                pl.BlockSpec((None, tt, ne), lambda b, i: (b, i, 0)),
                pl.BlockSpec((None, tt, ne), lambda b, i: (b, i, 0)),
                pl.BlockSpec((None, tt, d), lambda b, i: (b, i, 0)),
                pl.BlockSpec((None, None, N_MOD, d), lambda b, i: (l, mod_row(b), 0, 0))]
    args = [y, pos_cols, wv_cols, x, mod4]
    if final:
        in_specs.append(pl.BlockSpec((1, d), lambda b, i: (0, 0)))
        args.append(final_g)
    return pl.pallas_call(
        body,
        out_shape=jax.ShapeDtypeStruct((nb, n, d), F32),
        grid=(nb, n // tt),
        in_specs=in_specs,
        out_specs=pl.BlockSpec((None, tt, d), lambda b, i: (b, i, 0)),
        compiler_params=_params(2, 56),
        name="combine",
    )(*args)


def kernel(x, c, ctx, c_ctx, ada_w, ada_b, norm1_g, w_in, conv_w, ssm_lam_re, ssm_lam_im, ssm_log_dt,
           ssm_b_re, ssm_b_im, ssm_c_re, ssm_c_im, ssm_d, ssm_w_glu, out_norm_conv_g, out_norm_ssm_g,
           w_out, norm2_g, router_w, exp_w_gate, exp_w_up, exp_w_down, final_norm_g):
    bsz, n, d = x.shape
    nc = ctx.shape[1]
    depth = ada_w.shape[0]
    d_conv = conv_w.shape[-1]
    d_ssm = ssm_d.shape[-1]
    g = d_ssm // SSM_GROUP
    p = ssm_lam_re.shape[-1]
    ne = router_w.shape[-1]
    dff = exp_w_gate.shape[-1]
    cap = EC_FACTOR * n // ne
    cap_c = EC_FACTOR * nc // ne
    nbc = bsz * nc
    assert bsz == 8 and 2 * p == LANES and ssm_b_re.shape[-1] == SSM_GROUP
    assert nbc == n and bsz * cap_c == cap
    assert n % (CHUNK * SCAN_BLOCK) == 0 and nc % (CHUNK * SCAN_BLOCK) == 0
    assert d_conv % HEAD_DIM == 0 and d_ssm % HEAD_DIM == 0 and (3 * d_conv) % d_ssm == 0
    assert N_MOD * d == ada_w.shape[-1] and ne <= LANES

    tm = min(512, n)
    tt = min(256, n)
    tf = min(256, dff)

    rows = 16
    cvec = jnp.zeros((rows, d), F32).at[:bsz].set(c).at[bsz].set(c_ctx)
    mod4 = _ada(cvec, ada_w, ada_b).reshape(depth, rows, N_MOD, d)
    lat_row = lambda b: b
    ctx_row = lambda b: bsz

    w_in_bf = w_in.astype(BF16)
    w_out_bf = w_out.astype(BF16)
    w_glu_bf = ssm_w_glu.astype(BF16)
    rw_hi, rw_lo = _split_bf16(router_w)
    zpad = jnp.zeros((depth, d, LANES - ne), BF16)
    r1 = jnp.concatenate([rw_hi, zpad], axis=-1)
    r2 = jnp.concatenate([rw_hi, zpad, rw_lo, zpad], axis=-1)

    tile2 = lambda a: jnp.concatenate([a, a], axis=-1)
    lr_t = tile2(ssm_lam_re)[:, :, :, None, :]
    li_t = tile2(ssm_lam_im)[:, :, :, None, :]
    ldt = ssm_log_dt[:, :, :, None, None]
    brt = jnp.swapaxes(ssm_b_re, -1, -2)
    bit = jnp.swapaxes(ssm_b_im, -1, -2)
    bc1 = jnp.concatenate([brt, bit], axis=-1)
    bc2 = jnp.concatenate([-bit, brt], axis=-1)
    ca = jnp.concatenate([ssm_c_re, -ssm_c_im], axis=-1)
    cb = jnp.concatenate([-ssm_c_im, -ssm_c_re], axis=-1)
    dv = jnp.tile(ssm_d.reshape(depth, g, 1, SSM_GROUP), (1, 1, 1, CHUNK))

    g1n = norm1_g.reshape(depth, 1, d)
    g2n = norm2_g.reshape(depth, 1, d)
    gcn = out_norm_conv_g.reshape(depth, 1, d_conv)
    gsn = out_norm_ssm_g.reshape(depth, 1, d_ssm)
    gfin = final_norm_g.reshape(1, d)

    nblk_ctx = nc // (CHUNK * SCAN_BLOCK)
    nblk_lat = n // (CHUNK * SCAN_BLOCK)
    rows_ctx = nc // CHUNK * bsz

    xl = x
    xc = ctx
    for l in range(depth):
        last = l == depth - 1
        conv_l, u_lat = _inproj(xl, mod4, l, lat_row, g1n, w_in_bf, conv_w, gcn, rowlen=GRID_W, tm=tm)
        if last:
            u_ctx = _uproj(xc, mod4, l, ctx_row, g1n, w_in_bf, d_ssm, tm=nc)
        else:
            conv_c, u_ctx = _inproj(xc, mod4, l, ctx_row, g1n, w_in_bf, conv_w, gcn, rowlen=nc, tm=nc)

        u_all = jnp.concatenate([_to_chunks(u_ctx, g), _to_chunks(u_lat, g)], axis=1).astype(BF16)
        wcat, mtot, vmat, a1, a2 = _s5prep(l, lr_t, li_t, ldt, bc1, bc2, ca, cb, dv)
        s_loc = _s5local(u_all, wcat)
        h_f, h_r = _s5scan(s_loc, a1, a2, nblk_ctx, nblk_lat)
        y_all = _s5out(u_all, h_f, h_r, mtot, vmat)
        y_lat = _from_chunks(y_all[:, rows_ctx:], bsz, g)

        xl, h2, lg = _postmix(conv_l, y_lat, xl, mod4, l, lat_row, w_glu_bf, gsn, w_out_bf, g2n, r2, r1, tm=tm)
        lg_t = jnp.swapaxes(lg[:, :, :ne], 1, 2)
        pos, wv = _route(lg_t, cap)
        pos_cols = jnp.swapaxes(pos, 1, 2)
        wv_cols = jnp.swapaxes(wv, 1, 2)

        if not last:
            y_ctx = _from_chunks(y_all[:, :rows_ctx], bsz, g)
            xc, h2c, lgc = _postmix(conv_c, y_ctx, xc, mod4, l, ctx_row, w_glu_bf, gsn, w_out_bf, g2n,
                                    r2, r1, tm=nc)
            lgc_t = jnp.swapaxes(lgc[:, :, :ne], 1, 2)
            posc, wvc = _route(lgc_t, cap_c)
            offs = (jnp.arange(bsz, dtype=I32) * cap_c)[:, None, None]
            posc = jnp.where(posc >= 0, posc + offs, -1)
            posc_rows = jnp.swapaxes(posc, 0, 1).reshape(ne, 1, nbc)
            posc_cols = jnp.swapaxes(posc, 1, 2).reshape(1, nbc, ne)
            wvc_cols = jnp.swapaxes(wvc, 1, 2).reshape(1, nbc, ne)
            xs = _gather(pos[:, :, None, :], h2, cap, posc_rows, h2c.reshape(nbc, d))
        else:
            xs = _gather(pos[:, :, None, :], h2, cap)

        y_exp = _ffn(xs, l, exp_w_gate, exp_w_up, exp_w_down, tf=tf, m_split=2)
        xl = _combine(y_exp, 0, pos_cols, wv_cols, xl, mod4, l, lat_row, cap,
                      gfin if last else None, tt=tt)
        if not last:
            xc = _combine(y_exp, bsz, posc_cols, wvc_cols, xc.reshape(1, nbc, d), mod4, l, ctx_row, cap,
                          tt=tt).reshape(bsz, nc, d)
    return xl
```

```python
import functools

import jax
import jax.numpy as jnp
from jax import lax
from jax.experimental import pallas as pl
from jax.experimental.pallas import tpu as pltpu

F32 = jnp.float32
BF16 = jnp.bfloat16
I32 = jnp.int32

EPS = 1e-6
GRID_W = 64
HEAD_DIM = 128
SSM_GROUP = 16
EC_FACTOR = 2
N_MOD = 6
CHUNK = 16
LANES = 128
CHUNK_W = CHUNK * SSM_GROUP

_MIB = 1 << 20


def _params(n_axes, vmem_mib):
    return pltpu.CompilerParams(dimension_semantics=("arbitrary",) * n_axes,
                                vmem_limit_bytes=vmem_mib * _MIB)


def _split_bf16(a):
    hi = a.astype(BF16)
    lo = (a - hi.astype(F32)).astype(BF16)
    return hi, lo


_NN = (((1,), (0,)), ((), ()))
_NT = (((1,), (1,)), ((), ()))


def _dot(a, b, dims=_NN):
    return lax.dot_general(a, b, dims, preferred_element_type=F32)


def _dot3(a, b, dims=_NN):
    ah, al = _split_bf16(a)
    bh, bl = _split_bf16(b)
    return _dot(ah, bh, dims) + _dot(ah, bl, dims) + _dot(al, bh, dims)


def _rms(x):
    return x * lax.rsqrt(jnp.mean(x * x, axis=-1, keepdims=True) + EPS)


def _ada_body(c_ref, w_ref, b_ref, o_ref):
    s = jax.nn.silu(c_ref[...])
    o_ref[...] = _dot3(s, w_ref[...]) + b_ref[...]


def _ada(cvec, ada_w, ada_b):
    depth, d, n6 = ada_w.shape
    tn = next(t for t in (1024, 512, 256, 128) if n6 % t == 0)
    rows = cvec.shape[0]
    return pl.pallas_call(
        _ada_body,
        out_shape=jax.ShapeDtypeStruct((depth, rows, n6), F32),
        grid=(depth, n6 // tn),
        in_specs=[
            pl.BlockSpec((rows, d), lambda l, j: (0, 0)),
            pl.BlockSpec((None, d, tn), lambda l, j: (l, 0, j)),
            pl.BlockSpec((None, 1, tn), lambda l, j: (l, 0, j)),
        ],
        out_specs=pl.BlockSpec((None, rows, tn), lambda l, j: (l, 0, j)),
        compiler_params=_params(2, 40),
        name="ada",
    )(cvec, ada_w, ada_b.reshape(depth, 1, n6))


def _store_lane_tiles(u_ref, u):
    for k in range(u_ref.shape[0]):
        u_ref[k] = u[:, k * LANES:(k + 1) * LANES]


def _u_tiles(nb, n, d_ssm, tm, pseudo):
    k4 = d_ssm // LANES
    if pseudo:
        return (jax.ShapeDtypeStruct((1, k4, nb * n, LANES), F32),
                pl.BlockSpec((None, k4, tm, LANES), lambda b, i: (0, 0, b * (n // tm) + i, 0)))
    return (jax.ShapeDtypeStruct((nb, k4, n, LANES), F32),
            pl.BlockSpec((None, k4, tm, LANES), lambda b, i: (b, 0, i, 0)))


def _inproj_body(x_ref, mod_ref, g_ref, w_ref, cw_ref, gc_ref, conv_ref, u_ref, *, rowlen, d_conv, cn):
    x = x_ref[...]
    tm = x.shape[0]
    h = _rms(x) * g_ref[...]
    h = h * (1.0 + mod_ref[1:2, :]) + mod_ref[0:1, :]
    hb = h.astype(BF16)
    t = jnp.bitwise_and(lax.broadcasted_iota(I32, (tm, 1), 0), rowlen - 1)
    first = t == 0
    last = t == rowlen - 1
    for j in range(d_conv // cn):
        c0 = j * cn
        bg = _dot(hb, w_ref[:, c0:c0 + cn])
        cg = _dot(hb, w_ref[:, d_conv + c0:d_conv + c0 + cn])
        v = _dot(hb, w_ref[:, 2 * d_conv + c0:2 * d_conv + c0 + cn])
        z = cg * v
        zp = jnp.where(first, 0.0, pltpu.roll(z, 1, 0))
        zn = jnp.where(last, 0.0, pltpu.roll(z, tm - 1, 0))
        cw = cw_ref[:, c0:c0 + cn]
        y = bg * (cw[0:1, :] * zp + cw[1:2, :] * z + cw[2:3, :] * zn)
        for hd in range(cn // HEAD_DIM):
            lo = hd * HEAD_DIM
            yh = _rms(y[:, lo:lo + HEAD_DIM]) * gc_ref[:, c0 + lo:c0 + lo + HEAD_DIM]
            conv_ref[:, c0 + lo:c0 + lo + HEAD_DIM] = yh.astype(conv_ref.dtype)
    _store_lane_tiles(u_ref, _dot(hb, w_ref[:, 3 * d_conv:]))


def _inproj(x, mod4, l, mod_row, norm_g, w_bf, conv_w, gc, *, rowlen, tm, pseudo):
    nb, n, d = x.shape
    depth, _, d_in = w_bf.shape
    d_conv = conv_w.shape[-1]
    d_ssm = d_in - 3 * d_conv
    cn = min(512, d_conv)
    body = functools.partial(_inproj_body, rowlen=rowlen, d_conv=d_conv, cn=cn)
    u_shape, u_spec = _u_tiles(nb, n, d_ssm, tm, pseudo)
    return pl.pallas_call(
        body,
        out_shape=(jax.ShapeDtypeStruct((nb, n, d_conv), BF16), u_shape),
        grid=(nb, n // tm),
        in_specs=[
            pl.BlockSpec((None, tm, d), lambda b, i: (b, i, 0)),
            pl.BlockSpec((None, None, N_MOD, d), lambda b, i: (l, mod_row(b), 0, 0)),
            pl.BlockSpec((None, 1, d), lambda b, i: (l, 0, 0)),
            pl.BlockSpec((None, d, d_in), lambda b, i: (l, 0, 0), pipeline_mode=pl.Buffered(1)),
            pl.BlockSpec((None, 3, d_conv), lambda b, i: (l, 0, 0)),
            pl.BlockSpec((None, 1, d_conv), lambda b, i: (l, 0, 0)),
        ],
        out_specs=(pl.BlockSpec((None, tm, d_conv), lambda b, i: (b, i, 0)), u_spec),
        compiler_params=_params(2, 56),
        name="inproj",
    )(x, mod4, norm_g, w_bf, conv_w, gc)


def _uproj_body(x_ref, mod_ref, g_ref, w_ref, u_ref):
    h = _rms(x_ref[...]) * g_ref[...]
    h = h * (1.0 + mod_ref[1:2, :]) + mod_ref[0:1, :]
    _store_lane_tiles(u_ref, _dot(h.astype(BF16), w_ref[...]))


def _uproj(x, mod4, l, mod_row, norm_g, w_bf, d_ssm, *, tm):
    nb, n, d = x.shape
    d_in = w_bf.shape[-1]
    col_blk = (d_in - d_ssm) // d_ssm
    u_shape, u_spec = _u_tiles(nb, n, d_ssm, tm, True)
    return pl.pallas_call(
        _uproj_body,
        out_shape=u_shape,
        grid=(nb, n // tm),
        in_specs=[
            pl.BlockSpec((None, tm, d), lambda b, i: (b, i, 0)),
            pl.BlockSpec((None, None, N_MOD, d), lambda b, i: (l, mod_row(b), 0, 0)),
            pl.BlockSpec((None, 1, d), lambda b, i: (l, 0, 0)),
            pl.BlockSpec((None, d, d_ssm), lambda b, i: (l, 0, col_blk)),
        ],
        out_specs=u_spec,
        compiler_params=_params(2, 32),
        name="uproj",
    )(x, mod4, norm_g, w_bf)


N_LEVELS = 7
POW_ROWS = 24


def _s5prep_body(kk_ref, lr_ref, li_ref, ldt_ref, bc1_ref, bc2_ref, ca_ref, cb_ref, dv_ref,
                 wcat_ref, mtot_ref, v_ref, a1_ref, a2_ref):
    t_ = CHUNK
    kk = kk_ref[...]
    lane_blk = jnp.right_shift(lax.broadcasted_iota(I32, (SSM_GROUP, CHUNK_W), 1), 4)
    lane = lax.broadcasted_iota(I32, (1, LANES), 1)
    sgn = jnp.where(lane < LANES // 2, -1.0, 1.0)
    eye = jnp.where(lax.broadcasted_iota(I32, (LANES, LANES), 0)
                    == lax.broadcasted_iota(I32, (LANES, LANES), 1), 1.0, 0.0)
    mtot = jnp.zeros((CHUNK_W, CHUNK_W), F32)
    for d in range(2):
        lr = lr_ref[d]
        li = li_ref[d]
        dt = jnp.exp(ldt_ref[d])
        pm = jnp.exp(kk * (dt * lr))
        ang = kk * (dt * li)
        pr = pm * jnp.cos(ang)
        pi = pm * jnp.sin(ang)
        ar = pr[1:2, :]
        ai = pi[1:2, :]
        den = lr * lr + li * li
        nr = ar - 1.0
        kr = (nr * lr + ai * li) / den
        ki = (ai * lr - nr * li) / den
        bc1 = bc1_ref[d]
        bc2 = bc2_ref[d]
        bb1 = kr * bc1 + ki * bc2
        bb2 = kr * bc2 - ki * bc1
        ca = ca_ref[d]
        cb = cb_ref[d]
        cak = [ca * pr[k:k + 1, :] + cb * pi[k:k + 1, :] for k in range(t_ + 1)]
        lag_order = range(t_) if d == 0 else range(t_ - 1, -1, -1)
        cak_all = jnp.concatenate([cak[k] for k in lag_order], axis=0)
        kall_t = _dot3(bb1, cak_all, _NT)
        rows = []
        for j in range(t_):
            if d == 0:
                shift, keep = (SSM_GROUP * j) % CHUNK_W, lane_blk >= j
            else:
                shift, keep = (SSM_GROUP * (j + 1)) % CHUNK_W, lane_blk <= j
            r = pltpu.roll(kall_t, shift, 1) if shift else kall_t
            rows.append(jnp.where(keep, r, 0.0))
        mtot = mtot + jnp.concatenate(rows, axis=0)
        e_v = [i + 1 for i in range(t_)] if d == 0 else [t_ - i for i in range(t_)]
        v_t = jnp.concatenate([cak[e] for e in e_v], axis=0)
        v_ref[d] = _dot3(eye, v_t, _NT).astype(v_ref.dtype)
        e_w = [t_ - 1 - j for j in range(t_)] if d == 0 else list(range(t_))
        w = jnp.concatenate([bb1 * pr[e:e + 1, :] + bb2 * pi[e:e + 1, :] for e in e_w], axis=0)
        wcat_ref[:, d * LANES:(d + 1) * LANES] = w.astype(wcat_ref.dtype)
        a1_ref[d] = pr[t_:t_ + 8, :]
        a2_ref[d] = sgn * pi[t_:t_ + 8, :]
    diag = (lax.broadcasted_iota(I32, (CHUNK_W, CHUNK_W), 0)
            == lax.broadcasted_iota(I32, (CHUNK_W, CHUNK_W), 1))
    mtot = mtot + jnp.where(diag, dv_ref[...], 0.0)
    mtot_ref[...] = mtot.astype(mtot_ref.dtype)


def _s5prep(l, kk, lr_t, li_t, ldt, bc1, bc2, ca, cb, dv):
    g = lr_t.shape[2]
    spec5 = lambda r, c: pl.BlockSpec((None, 2, None, r, c), lambda i: (l, 0, i, 0, 0))
    return pl.pallas_call(
        _s5prep_body,
        out_shape=(jax.ShapeDtypeStruct((g, CHUNK_W, 2 * LANES), BF16),
                   jax.ShapeDtypeStruct((g, CHUNK_W, CHUNK_W), BF16),
                   jax.ShapeDtypeStruct((2, g, LANES, CHUNK_W), BF16),
                   jax.ShapeDtypeStruct((2, g, 8, LANES), F32),
                   jax.ShapeDtypeStruct((2, g, 8, LANES), F32)),
        grid=(g,),
        in_specs=[pl.BlockSpec((POW_ROWS, 1), lambda i: (0, 0)),
                  spec5(1, LANES), spec5(1, LANES), spec5(1, 1),
                  spec5(SSM_GROUP, LANES), spec5(SSM_GROUP, LANES),
                  spec5(SSM_GROUP, LANES), spec5(SSM_GROUP, LANES),
                  pl.BlockSpec((None, None, 1, CHUNK_W), lambda i: (l, i, 0, 0))],
        out_specs=(pl.BlockSpec((None, CHUNK_W, 2 * LANES), lambda i: (i, 0, 0)),
                   pl.BlockSpec((None, CHUNK_W, CHUNK_W), lambda i: (i, 0, 0)),
                   pl.BlockSpec((2, None, LANES, CHUNK_W), lambda i: (0, i, 0, 0)),
                   pl.BlockSpec((2, None, 8, LANES), lambda i: (0, i, 0, 0)),
                   pl.BlockSpec((2, None, 8, LANES), lambda i: (0, i, 0, 0))),
        compiler_params=_params(1, 32),
        name="s5prep",
    )(kk, lr_t, li_t, ldt, bc1, bc2, ca, cb, dv)


def _s5mix_body(*refs, seg, with_h0):
    if with_h0:
        u_ref, wcat_ref, m_ref, v_ref, a1_ref, a2_ref, h0_ref, y_ref = refs
    else:
        u_ref, wcat_ref, m_ref, v_ref, a1_ref, a2_ref, y_ref, fin_ref, fs_ref = refs
    k4, n, _ = u_ref.shape
    c = n // CHUNK
    gpt = LANES // SSM_GROUP
    b = pl.program_id(0)
    rowm = jnp.bitwise_and(lax.broadcasted_iota(I32, (c, 1), 0), seg - 1)
    shifts = [1 << j for j in range(seg.bit_length() - 1)]

    def swap(x):
        return pltpu.roll(x, LANES // 2, 1)

    def scan(s, d, g):
        a1 = a1_ref[d, g]
        a2 = a2_ref[d, g]
        h0 = h0_ref[d, g, pl.ds(b, 1), :] if with_h0 else 0.0
        if d == 0:
            x = jnp.where(rowm == 0, h0, pltpu.roll(s, 1, 0))
        else:
            x = jnp.where(rowm == seg - 1, h0, pltpu.roll(s, c - 1, 0))
        for j, sh in enumerate(shifts):
            if d == 0:
                xs = jnp.where(rowm >= sh, pltpu.roll(x, sh, 0), 0.0)
            else:
                xs = jnp.where(rowm < seg - sh, pltpu.roll(x, c - sh, 0), 0.0)
            x = x + a1[j:j + 1, :] * xs + a2[j:j + 1, :] * swap(xs)
        return x

    for k in range(k4):
        xk = jnp.concatenate([u_ref[k, pl.ds(t, c, stride=CHUNK), :] for t in range(CHUNK)], axis=0)
        xt = xk.T
        yts = []
        for gg in range(gpt):
            g = k * gpt + gg
            rg = jnp.concatenate([xt[gg * SSM_GROUP:(gg + 1) * SSM_GROUP, t * c:(t + 1) * c]
                                  for t in range(CHUNK)], axis=0)
            z = rg.T.astype(BF16)
            s = _dot(z, wcat_ref[g])
            y = _dot(z, m_ref[g])
            for d in range(2):
                sd = s[:, d * LANES:(d + 1) * LANES]
                hin = scan(sd, d, g)
                y = y + _dot(hin.astype(BF16), v_ref[d, g])
                if not with_h0:
                    a1 = a1_ref[d, g]
                    a2 = a2_ref[d, g]
                    fs_ref[...] = a1[0:1, :] * hin + a2[0:1, :] * swap(hin) + sd
                    first = seg - 1 if d == 0 else 0
                    fin_ref[d, g] = fs_ref[pl.ds(first, c // seg, stride=seg), :]
            yts.append(y.T)
        xto = jnp.concatenate(
            [jnp.concatenate([yt[t * SSM_GROUP:(t + 1) * SSM_GROUP, :] for yt in yts], axis=0)
             for t in range(CHUNK)], axis=1)
        xo = xto.T
        for t in range(CHUNK):
            y_ref[k, pl.ds(t, c, stride=CHUNK), :] = xo[t * c:(t + 1) * c, :]


def _s5mix(u4, wcat, mtot, v, a1, a2, *, seg, h0=None):
    nb, k4, n, _ = u4.shape
    g = wcat.shape[0]
    c = n // CHUNK
    with_h0 = h0 is not None
    body = functools.partial(_s5mix_body, seg=seg, with_h0=with_h0)
    tile = pl.BlockSpec((None, k4, n, LANES), lambda b: (b, 0, 0, 0))
    res = lambda shape: pl.BlockSpec(shape, lambda b: (0,) * len(shape), pipeline_mode=pl.Buffered(1))
    in_specs = [tile, res((g, CHUNK_W, 2 * LANES)), res((g, CHUNK_W, CHUNK_W)), res((2, g, LANES, CHUNK_W)),
                res((2, g, 8, LANES)), res((2, g, 8, LANES))]
    args = [u4, wcat, mtot, v, a1, a2]
    y_shape = jax.ShapeDtypeStruct((nb, k4, n, LANES), F32)
    if with_h0:
        in_specs.append(res((2, g, h0.shape[2], LANES)))
        args.append(h0)
        out_shape, out_specs, scratch = y_shape, tile, []
    else:
        out_shape = (y_shape, jax.ShapeDtypeStruct((2, g, c // seg, LANES), F32))
        out_specs = (tile, pl.BlockSpec((2, g, c // seg, LANES), lambda b: (0, 0, 0, 0)))
        scratch = [pltpu.VMEM((c, LANES), F32)]
    return pl.pallas_call(
        body,
        out_shape=out_shape,
        grid=(nb,),
        in_specs=in_specs,
        out_specs=out_specs,
        scratch_shapes=scratch,
        compiler_params=_params(1, 48),
        name="s5mix",
    )(*args)


def _postmix_body(*refs, d_conv, aliased):
    if aliased:
        refs = refs[1:]
    (cn_ref, y_ref, x_ref, mod_ref, wglu_ref, gs_ref, wout_ref, g2_ref, r2_ref, r1_ref,
     xo_ref, h2_ref, lg_ref) = refs
    yg = jax.nn.gelu(jnp.concatenate([y_ref[k] for k in range(y_ref.shape[0])], axis=1))
    z = _dot(yg.astype(BF16), wglu_ref[...])
    s = yg * jax.nn.sigmoid(z)
    d_ssm = s.shape[-1]
    mix = _dot(cn_ref[...], wout_ref[0:d_conv, :])
    for hd in range(d_ssm // HEAD_DIM):
        lo = hd * HEAD_DIM
        sn = _rms(s[:, lo:lo + HEAD_DIM]) * gs_ref[:, lo:lo + HEAD_DIM]
        mix = mix + _dot(sn.astype(BF16), wout_ref[d_conv + lo:d_conv + lo + HEAD_DIM, :])
    xn = x_ref[...] + mod_ref[2:3, :] * mix
    xo_ref[...] = xn
    h2 = _rms(xn) * g2_ref[...]
    h2 = h2 * (1.0 + mod_ref[4:5, :]) + mod_ref[3:4, :]
    hi, lo_ = _split_bf16(h2)
    tm, d = h2.shape
    for k in range(d // LANES):
        h2_ref[pl.ds(k, tm, stride=d // LANES), :] = h2[:, k * LANES:(k + 1) * LANES]
    d1 = _dot(hi, r2_ref[...])
    d2 = _dot(lo_, r1_ref[...])
    lg_ref[...] = d1[:, 0:LANES] + d1[:, LANES:2 * LANES] + d2


def _postmix(conv_n, y4, x, mod4, l, mod_row, wglu_bf, gs, wout_bf, norm2_g, r2, r1, *, tm, pseudo,
             table_tokens, table=None, tok0=0):
    nb, n, d = x.shape
    d_conv = conv_n.shape[-1]
    d_ssm = y4.shape[1] * LANES
    tok_rows = d // LANES
    aliased = table is not None
    body = functools.partial(_postmix_body, d_conv=d_conv, aliased=aliased)
    _, y_spec = _u_tiles(nb, n, d_ssm, tm, pseudo)
    tok = lambda w: pl.BlockSpec((None, tm, w), lambda b, i: (b, i, 0))
    lay = lambda r, c, **kw: pl.BlockSpec((None, r, c), lambda b, i: (l, 0, 0), **kw)
    in_specs = [tok(d_conv), y_spec, tok(d),
                pl.BlockSpec((None, None, N_MOD, d), lambda b, i: (l, mod_row(b), 0, 0)),
                lay(d_ssm, d_ssm, pipeline_mode=pl.Buffered(1)),
                lay(1, d_ssm),
                lay(d_conv + d_ssm, d, pipeline_mode=pl.Buffered(1)),
                lay(1, d),
                lay(d, 2 * LANES, pipeline_mode=pl.Buffered(1)),
                lay(d, LANES, pipeline_mode=pl.Buffered(1))]
    args = [conv_n, y4, x, mod4, wglu_bf, gs, wout_bf, norm2_g, r2, r1]
    if aliased:
        in_specs.insert(0, pl.BlockSpec(memory_space=pl.ANY))
        args.insert(0, table)
    blk0 = tok0 // tm
    return pl.pallas_call(
        body,
        out_shape=(jax.ShapeDtypeStruct((nb, n, d), F32),
                   jax.ShapeDtypeStruct((table_tokens * tok_rows, LANES), F32),
                   jax.ShapeDtypeStruct((nb, n, LANES), F32)),
        grid=(nb, n // tm),
        in_specs=in_specs,
        out_specs=(tok(d),
                   pl.BlockSpec((tm * tok_rows, LANES), lambda b, i: (blk0 + b * (n // tm) + i, 0)),
                   tok(LANES)),
        input_output_aliases={0: 1} if aliased else {},
        compiler_params=_params(2, 56),
        name="postmix",
    )(*args)


BISECT_STEPS = 32

def _route_body(lg_ref, pos_ref, wv_ref, tri_ref, *, cap):
    nb, ne, n = lg_ref.shape
    rc = min(256, n)
    for r0 in range(0, n, rc):
        ri = lax.broadcasted_iota(I32, (rc, n), 0) + r0
        ci = lax.broadcasted_iota(I32, (rc, n), 1)
        tri_ref[r0:r0 + rc, :] = jnp.where(ri < ci, 1.0, 0.0).astype(BF16)
    lg = lg_ref[...]
    e = jnp.exp(lg - jnp.max(lg, axis=1, keepdims=True))
    aff = e / jnp.sum(e, axis=1, keepdims=True)
    capf = float(cap)

    def enough(t):
        return jnp.sum(jnp.where(aff >= t, 1.0, 0.0), axis=2, keepdims=True) >= capf

    hi = jnp.full((nb, ne, 1), 2.0, F32)
    for j in range(6, -1, -1):
        cand = hi * (2.0 ** -(1 << j))
        hi = jnp.where(enough(cand), hi, cand)
    half = hi * 0.5
    lo = jnp.where(enough(half), half, 0.0)
    for _ in range(BISECT_STEPS):
        mid = (lo + hi) * 0.5
        ok = enough(mid)
        lo = jnp.where(ok, mid, lo)
        hi = jnp.where(ok, hi, mid)
    gt = aff >= hi
    eq = jnp.logical_and(aff >= lo, aff < hi)
    need = capf - jnp.sum(jnp.where(gt, 1.0, 0.0), axis=2, keepdims=True)
    tri = tri_ref[...]
    eq_rank = _dot(jnp.where(eq, 1.0, 0.0).astype(BF16).reshape(nb * ne, n), tri).reshape(nb, ne, n)
    sel = jnp.logical_or(gt, jnp.logical_and(eq, eq_rank < need))
    pos = _dot(jnp.where(sel, 1.0, 0.0).astype(BF16).reshape(nb * ne, n), tri).reshape(nb, ne, n)
    pos_ref[...] = jnp.where(sel, pos.astype(I32), -1)
    wv_ref[...] = jnp.where(sel, aff, 0.0)


def _route(logits_t, cap):
    nb, ne, n = logits_t.shape
    body = functools.partial(_route_body, cap=cap)
    full = pl.BlockSpec((nb, ne, n), lambda i: (0, 0, 0))
    return pl.pallas_call(
        body,
        out_shape=(jax.ShapeDtypeStruct((nb, ne, n), I32),
                   jax.ShapeDtypeStruct((nb, ne, n), F32)),
        grid=(1,),
        in_specs=[full],
        out_specs=(full, full),
        scratch_shapes=[pltpu.VMEM((n, n), BF16)],
        compiler_params=_params(1, 48),
        name="route",
    )(logits_t)


def _slotidx_body(pos_ref, o_ref, *, cap):
    n, ne = pos_ref.shape
    pos = pos_ref[...]
    tok = lax.broadcasted_iota(I32, (n, 1), 0).astype(F32)
    slot = lax.broadcasted_iota(I32, (n, cap), 1)
    for e in range(ne):
        hit = jnp.where(slot == pos[:, e:e + 1], tok, 0.0)
        o_ref[e:e + 1, :] = jnp.sum(hit, axis=0, keepdims=True).astype(I32)


def _slotidx(pos_cols, cap):
    nb, n, ne = pos_cols.shape
    return pl.pallas_call(
        functools.partial(_slotidx_body, cap=cap),
        out_shape=jax.ShapeDtypeStruct((nb, ne, cap), I32),
        grid=(nb,),
        in_specs=[pl.BlockSpec((None, n, ne), lambda b: (b, 0, 0))],
        out_specs=pl.BlockSpec((None, ne, cap), lambda b: (b, 0, 0)),
        compiler_params=_params(1, 32),
        name="slotidx",
    )(pos_cols)


def _ffn_body(idx_ref, tab_ref, wg_ref, wu_ref, wd_ref, y_ref, xbuf_ref, xs_ref, acc_ref, sem, *, tok_rows, nf):
    f = pl.program_id(2)
    blk = pl.program_id(0) * pl.num_programs(1) + pl.program_id(1)
    nblk = pl.num_programs(0) * pl.num_programs(1)
    step = blk * nf + f
    mh = xs_ref.shape[1]
    ch = mh // nf
    cur = jnp.bitwise_and(blk, 1)
    ring = jnp.bitwise_and(step, 1)

    def token_copy(blk_i, c, j, ring_i):
        tok = idx_ref[blk_i * mh + c * ch + j]
        src = tab_ref.at[pl.ds(pl.multiple_of(tok * tok_rows, tok_rows), tok_rows), :]
        dst = xbuf_ref.at[ring_i, pl.ds(pl.multiple_of(j * tok_rows, tok_rows), tok_rows), :]
        return pltpu.make_async_copy(src, dst, sem.at[ring_i])

    def wait_chunk(ring_i):
        pltpu.make_async_copy(tab_ref.at[pl.ds(0, ch * tok_rows), :], xbuf_ref.at[ring_i], sem.at[ring_i]).wait()

    def convert(ring_i, xs_slot, c):
        rows = pl.ds(pl.multiple_of(c * ch, ch), ch)
        for k in range(tok_rows):
            w = xbuf_ref[ring_i, pl.ds(k, ch, stride=tok_rows), :]
            xs_ref[xs_slot, rows, k * LANES:(k + 1) * LANES] = w.astype(BF16)

    def issue_loop(blk_i, c, ring_i):
        def issue(j, carry):
            token_copy(blk_i, c, j, ring_i).start()
            return carry
        lax.fori_loop(0, ch, issue, 0)

    @pl.when(step == 0)
    def _():
        for c in range(nf):
            issue_loop(0, c, 0)
            wait_chunk(0)
            convert(0, 0, c)
        issue_loop(1 % nblk, 0, 1)

    q = step + nf + 1
    qb = q // nf
    qb = jnp.where(qb < nblk, qb, 0)
    for j in range(ch):
        token_copy(qb, q % nf, j, ring).start()

    xs = xs_ref[cur]
    g = _dot(xs, wg_ref[...].astype(BF16))
    u = _dot(xs, wu_ref[...].astype(BF16))
    act = (jax.nn.silu(g) * u).astype(BF16)
    acc = jnp.where(f == 0, 0.0, acc_ref[...]) + _dot(act, wd_ref[...].astype(BF16))
    acc_ref[...] = acc
    y_ref[...] = acc.astype(y_ref.dtype)

    wait_chunk(1 - ring)
    convert(1 - ring, 1 - cur, f)

    @pl.when(step == nblk * nf - 1)
    def _():
        wait_chunk(ring)


def _ffn(slot_tok, table, l, w_gate, w_up, w_down, *, tf, m_split):
    _, ne, d, dff = w_gate.shape
    tok_rows = d // LANES
    m = slot_tok.shape[0] // ne
    mh = m // m_split
    nf = dff // tf
    assert mh % nf == 0 and table.shape[1] == LANES
    body = functools.partial(_ffn_body, tok_rows=tok_rows, nf=nf)
    grid_spec = pltpu.PrefetchScalarGridSpec(
        num_scalar_prefetch=1,
        grid=(m_split, ne, nf),
        in_specs=[pl.BlockSpec(memory_space=pl.ANY),
                  pl.BlockSpec((None, None, d, tf), lambda h, e, f, idx: (l, e, 0, f)),
                  pl.BlockSpec((None, None, d, tf), lambda h, e, f, idx: (l, e, 0, f)),
                  pl.BlockSpec((None, None, tf, d), lambda h, e, f, idx: (l, e, f, 0))],
        out_specs=pl.BlockSpec((None, mh, d), lambda h, e, f, idx: (e, h, 0)),
        scratch_shapes=[pltpu.VMEM((2, mh // nf * tok_rows, LANES), F32),
                        pltpu.VMEM((2, mh, d), BF16),
                        pltpu.VMEM((mh, d), F32),
                        pltpu.SemaphoreType.DMA((2,))])
    return pl.pallas_call(
        body,
        out_shape=jax.ShapeDtypeStruct((ne, m, d), BF16),
        grid_spec=grid_spec,
        compiler_params=_params(3, 56),
        name="ffn",
    )(slot_tok, table, w_gate, w_up, w_down)


def _combine_body(*refs, cap, final):
    if final:
        y_ref, pos_ref, wv_ref, x_ref, mod_ref, gf_ref, o_ref = refs
    else:
        y_ref, pos_ref, wv_ref, x_ref, mod_ref, o_ref = refs
    ne = y_ref.shape[0]
    tt = x_ref.shape[0]
    slot = lax.broadcasted_iota(I32, (tt, cap), 1)
    pos = pos_ref[...]
    wv = wv_ref[...]
    pieces = [jnp.where(slot == pos[:, e:e + 1], wv[:, e:e + 1], 0.0).astype(BF16) for e in range(ne)]
    pt = jnp.concatenate(pieces, axis=1)
    y = y_ref[...].reshape(ne * cap, y_ref.shape[-1])
    xn = x_ref[...] + mod_ref[5:6, :] * _dot(pt, y)
    if final:
        xn = _rms(xn) * gf_ref[...]
    o_ref[...] = xn


def _combine(y, blk0, pos_cols, wv_cols, x, mod4, l, mod_row, cap, final_g=None, *, tt):
    nb, n, d = x.shape
    ne = y.shape[0]
    final = final_g is not None
    body = functools.partial(_combine_body, cap=cap, final=final)
    in_specs = [pl.BlockSpec((ne, cap, d), lambda b, i: (0, blk0 + b, 0)),
                pl.BlockSpec((None, tt, ne), lambda b, i: (b, i, 0)),
                pl.BlockSpec((None, tt, ne), lambda b, i: (b, i, 0)),
                pl.BlockSpec((None, tt, d), lambda b, i: (b, i, 0)),
                pl.BlockSpec((None, None, N_MOD, d), lambda b, i: (l, mod_row(b), 0, 0))]
    args = [y, pos_cols, wv_cols, x, mod4]
    if final:
        in_specs.append(pl.BlockSpec((1, d), lambda b, i: (0, 0)))
        args.append(final_g)
    return pl.pallas_call(
        body,
        out_shape=jax.ShapeDtypeStruct((nb, n, d), F32),
        grid=(nb, n // tt),
        in_specs=in_specs,
        out_specs=pl.BlockSpec((None, tt, d), lambda b, i: (b, i, 0)),
        compiler_params=_params(2, 56),
        name="combine",
    )(*args)


def kernel(x, c, ctx, c_ctx, ada_w, ada_b, norm1_g, w_in, conv_w, ssm_lam_re, ssm_lam_im, ssm_log_dt,
           ssm_b_re, ssm_b_im, ssm_c_re, ssm_c_im, ssm_d, ssm_w_glu, out_norm_conv_g, out_norm_ssm_g,
           w_out, norm2_g, router_w, exp_w_gate, exp_w_up, exp_w_down, final_norm_g):
    bsz, n, d = x.shape
    nc = ctx.shape[1]
    depth = ada_w.shape[0]
    d_conv = conv_w.shape[-1]
    d_ssm = ssm_d.shape[-1]
    g = d_ssm // SSM_GROUP
    p = ssm_lam_re.shape[-1]
    ne = router_w.shape[-1]
    dff = exp_w_gate.shape[-1]
    cap = EC_FACTOR * n // ne
    cap_c = EC_FACTOR * nc // ne
    nbc = bsz * nc
    assert bsz == 8 and 2 * p == LANES and ssm_b_re.shape[-1] == SSM_GROUP
    assert nbc == n and bsz * cap_c == cap
    assert n // CHUNK == 1 << N_LEVELS and nc % CHUNK == 0 and (nc // CHUNK) & (nc // CHUNK - 1) == 0
    assert d_conv % HEAD_DIM == 0 and d_ssm % HEAD_DIM == 0 and (3 * d_conv) % d_ssm == 0
    assert N_MOD * d == ada_w.shape[-1] and ne <= LANES

    tm = min(512, n)
    tt = min(256, n)
    tf = min(256, dff)

    rows = 16
    cvec = jnp.zeros((rows, d), F32).at[:bsz].set(c).at[bsz].set(c_ctx)
    mod4 = _ada(cvec, ada_w, ada_b).reshape(depth, rows, N_MOD, d)
    lat_row = lambda b: b
    ctx_row = lambda b: bsz

    w_in_bf = w_in.astype(BF16)
    w_out_bf = w_out.astype(BF16)
    w_glu_bf = ssm_w_glu.astype(BF16)
    rw_hi, rw_lo = _split_bf16(router_w)
    zpad = jnp.zeros((depth, d, LANES - ne), BF16)
    r1 = jnp.concatenate([rw_hi, zpad], axis=-1)
    r2 = jnp.concatenate([rw_hi, zpad, rw_lo, zpad], axis=-1)

    tile2 = lambda a: jnp.concatenate([a, a], axis=-1)
    lr_t = tile2(ssm_lam_re)[:, :, :, None, :]
    li_t = tile2(ssm_lam_im)[:, :, :, None, :]
    ldt = ssm_log_dt[:, :, :, None, None]
    brt = jnp.swapaxes(ssm_b_re, -1, -2)
    bit = jnp.swapaxes(ssm_b_im, -1, -2)
    bc1 = jnp.concatenate([brt, bit], axis=-1)
    bc2 = jnp.concatenate([-bit, brt], axis=-1)
    ca = jnp.concatenate([ssm_c_re, -ssm_c_im], axis=-1)
    cb = jnp.concatenate([-ssm_c_im, -ssm_c_re], axis=-1)
    dv = jnp.tile(ssm_d.reshape(depth, g, 1, SSM_GROUP), (1, 1, 1, CHUNK))

    g1n = norm1_g.reshape(depth, 1, d)
    g2n = norm2_g.reshape(depth, 1, d)
    gcn = out_norm_conv_g.reshape(depth, 1, d_conv)
    gsn = out_norm_ssm_g.reshape(depth, 1, d_ssm)
    gfin = final_norm_g.reshape(1, d)

    kk = jnp.asarray([float(k) for k in range(CHUNK + 1)]
                     + [float(CHUNK << j) for j in range(1, N_LEVELS)] + [0.0], F32).reshape(POW_ROWS, 1)

    xl = x
    xc = ctx
    for l in range(depth):
        last = l == depth - 1
        conv_l, u_lat = _inproj(xl, mod4, l, lat_row, g1n, w_in_bf, conv_w, gcn, rowlen=GRID_W, tm=tm,
                                pseudo=False)
        if last:
            u_ctx = _uproj(xc, mod4, l, ctx_row, g1n, w_in_bf, d_ssm, tm=nc)
        else:
            conv_c, u_ctx = _inproj(xc, mod4, l, ctx_row, g1n, w_in_bf, conv_w, gcn, rowlen=nc, tm=nc,
                                    pseudo=True)

        wcat, mtot, vmat, a1, a2 = _s5prep(l, kk, lr_t, li_t, ldt, bc1, bc2, ca, cb, dv)
        y_ctx, h0 = _s5mix(u_ctx, wcat, mtot, vmat, a1, a2, seg=nc // CHUNK)
        y_lat = _s5mix(u_lat, wcat, mtot, vmat, a1, a2, seg=n // CHUNK, h0=h0)

        n_tab = bsz * n + (0 if last else nbc)
        table = None if last else jnp.zeros((n_tab * (d // LANES), LANES), F32)
        xl, table, lg = _postmix(conv_l, y_lat, xl, mod4, l, lat_row, w_glu_bf, gsn, w_out_bf, g2n, r2, r1,
                                 tm=tm, pseudo=False, table_tokens=n_tab, table=table)
        lg_t = jnp.swapaxes(lg[:, :, :ne], 1, 2)
        pos, wv = _route(lg_t, cap)
        pos_cols = jnp.swapaxes(pos, 1, 2)
        wv_cols = jnp.swapaxes(wv, 1, 2)
        offs_n = (jnp.arange(bsz, dtype=I32) * n)[:, None, None]
        slot_tok = jnp.swapaxes(_slotidx(pos_cols, cap) + offs_n, 0, 1).reshape(ne, bsz * cap)

        if not last:
            xc, table, lgc = _postmix(conv_c, y_ctx, xc, mod4, l, ctx_row, w_glu_bf, gsn, w_out_bf, g2n,
                                      r2, r1, tm=nc, pseudo=True, table_tokens=n_tab, table=table, tok0=bsz * n)
            lgc_t = jnp.swapaxes(lgc[:, :, :ne], 1, 2)
            posc, wvc = _route(lgc_t, cap_c)
            offs = (jnp.arange(bsz, dtype=I32) * cap_c)[:, None, None]
            posc = jnp.where(posc >= 0, posc + offs, -1)
            posc_cols = jnp.swapaxes(posc, 1, 2).reshape(1, nbc, ne)
            wvc_cols = jnp.swapaxes(wvc, 1, 2).reshape(1, nbc, ne)
            slot_tok = jnp.concatenate([slot_tok, _slotidx(posc_cols, cap)[0] + bsz * n], axis=1)

        m_split = 2
        mh = slot_tok.shape[1] // m_split
        slot_tok = jnp.swapaxes(slot_tok.reshape(ne, m_split, mh), 0, 1).reshape(-1)
        y_exp = _ffn(slot_tok, table, l, exp_w_gate, exp_w_up, exp_w_down, tf=tf, m_split=m_split)
        xl = _combine(y_exp, 0, pos_cols, wv_cols, xl, mod4, l, lat_row, cap,
                      gfin if last else None, tt=tt)
        if not last:
            xc = _combine(y_exp, bsz, posc_cols, wvc_cols, xc.reshape(1, nbc, d), mod4, l, ctx_row, cap,
                          tt=tt).reshape(bsz, nc, d)
    return xl
```

```python
import functools

import jax
import jax.numpy as jnp
from jax import lax
from jax.experimental import pallas as pl
from jax.experimental.pallas import tpu as pltpu

F32 = jnp.float32
BF16 = jnp.bfloat16
I32 = jnp.int32

EPS = 1e-6
GRID_W = 64
HEAD_DIM = 128
SSM_GROUP = 16
EC_FACTOR = 2
N_MOD = 6
CHUNK = 16
LANES = 128
CHUNK_W = CHUNK * SSM_GROUP

_MIB = 1 << 20


def _params(n_axes, vmem_mib):
    return pltpu.CompilerParams(dimension_semantics=("arbitrary",) * n_axes,
                                vmem_limit_bytes=vmem_mib * _MIB)


def _split_bf16(a):
    hi = a.astype(BF16)
    lo = (a - hi.astype(F32)).astype(BF16)
    return hi, lo


_NN = (((1,), (0,)), ((), ()))
_NT = (((1,), (1,)), ((), ()))


def _dot(a, b, dims=_NN):
    return lax.dot_general(a, b, dims, preferred_element_type=F32)


def _dot3(a, b, dims=_NN):
    ah, al = _split_bf16(a)
    bh, bl = _split_bf16(b)
    return _dot(ah, bh, dims) + _dot(ah, bl, dims) + _dot(al, bh, dims)


def _rms(x):
    return x * lax.rsqrt(jnp.mean(x * x, axis=-1, keepdims=True) + EPS)


def _ada_body(c_ref, w_ref, b_ref, o_ref):
    s = jax.nn.silu(c_ref[...])
    o_ref[...] = _dot3(s, w_ref[...]) + b_ref[...]


def _ada(cvec, ada_w, ada_b):
    depth, d, n6 = ada_w.shape
    tn = next(t for t in (1024, 512, 256, 128) if n6 % t == 0)
    rows = cvec.shape[0]
    return pl.pallas_call(
        _ada_body,
        out_shape=jax.ShapeDtypeStruct((depth, rows, n6), F32),
        grid=(depth, n6 // tn),
        in_specs=[
            pl.BlockSpec((rows, d), lambda l, j: (0, 0)),
            pl.BlockSpec((None, d, tn), lambda l, j: (l, 0, j)),
            pl.BlockSpec((None, 1, tn), lambda l, j: (l, 0, j)),
        ],
        out_specs=pl.BlockSpec((None, rows, tn), lambda l, j: (l, 0, j)),
        compiler_params=_params(2, 40),
        name="ada",
    )(cvec, ada_w, ada_b.reshape(depth, 1, n6))


def _store_lane_tiles(u_ref, u):
    for k in range(u_ref.shape[0]):
        u_ref[k] = u[:, k * LANES:(k + 1) * LANES]


def _u_tiles(nb, n, d_ssm, tm, pseudo):
    k4 = d_ssm // LANES
    if pseudo:
        return (jax.ShapeDtypeStruct((1, k4, nb * n, LANES), F32),
                pl.BlockSpec((None, k4, tm, LANES), lambda b, i: (0, 0, b * (n // tm) + i, 0)))
    return (jax.ShapeDtypeStruct((nb, k4, n, LANES), F32),
            pl.BlockSpec((None, k4, tm, LANES), lambda b, i: (b, 0, i, 0)))


def _inproj_body(x_ref, mod_ref, g_ref, w_ref, cw_ref, gc_ref, conv_ref, u_ref, *, rowlen, d_conv, cn):
    x = x_ref[...]
    tm = x.shape[0]
    h = _rms(x) * g_ref[...]
    h = h * (1.0 + mod_ref[1:2, :]) + mod_ref[0:1, :]
    hb = h.astype(BF16)
    t = jnp.bitwise_and(lax.broadcasted_iota(I32, (tm, 1), 0), rowlen - 1)
    first = t == 0
    last = t == rowlen - 1
    for j in range(d_conv // cn):
        c0 = j * cn
        bg = _dot(hb, w_ref[:, c0:c0 + cn])
        cg = _dot(hb, w_ref[:, d_conv + c0:d_conv + c0 + cn])
        v = _dot(hb, w_ref[:, 2 * d_conv + c0:2 * d_conv + c0 + cn])
        z = cg * v
        zp = jnp.where(first, 0.0, pltpu.roll(z, 1, 0))
        zn = jnp.where(last, 0.0, pltpu.roll(z, tm - 1, 0))
        cw = cw_ref[:, c0:c0 + cn]
        y = bg * (cw[0:1, :] * zp + cw[1:2, :] * z + cw[2:3, :] * zn)
        for hd in range(cn // HEAD_DIM):
            lo = hd * HEAD_DIM
            yh = _rms(y[:, lo:lo + HEAD_DIM]) * gc_ref[:, c0 + lo:c0 + lo + HEAD_DIM]
            conv_ref[:, c0 + lo:c0 + lo + HEAD_DIM] = yh.astype(conv_ref.dtype)
    _store_lane_tiles(u_ref, _dot(hb, w_ref[:, 3 * d_conv:]))


def _inproj(x, mod4, l, mod_row, norm_g, w_bf, conv_w, gc, *, rowlen, tm, pseudo):
    nb, n, d = x.shape
    depth, _, d_in = w_bf.shape
    d_conv = conv_w.shape[-1]
    d_ssm = d_in - 3 * d_conv
    cn = min(512, d_conv)
    body = functools.partial(_inproj_body, rowlen=rowlen, d_conv=d_conv, cn=cn)
    u_shape, u_spec = _u_tiles(nb, n, d_ssm, tm, pseudo)
    return pl.pallas_call(
        body,
        out_shape=(jax.ShapeDtypeStruct((nb, n, d_conv), BF16), u_shape),
        grid=(nb, n // tm),
        in_specs=[
            pl.BlockSpec((None, tm, d), lambda b, i: (b, i, 0)),
            pl.BlockSpec((None, None, N_MOD, d), lambda b, i: (l, mod_row(b), 0, 0)),
            pl.BlockSpec((None, 1, d), lambda b, i: (l, 0, 0)),
            pl.BlockSpec((None, d, d_in), lambda b, i: (l, 0, 0), pipeline_mode=pl.Buffered(1)),
            pl.BlockSpec((None, 3, d_conv), lambda b, i: (l, 0, 0)),
            pl.BlockSpec((None, 1, d_conv), lambda b, i: (l, 0, 0)),
        ],
        out_specs=(pl.BlockSpec((None, tm, d_conv), lambda b, i: (b, i, 0)), u_spec),
        compiler_params=_params(2, 56),
        name="inproj",
    )(x, mod4, norm_g, w_bf, conv_w, gc)


def _uproj_body(x_ref, mod_ref, g_ref, w_ref, u_ref):
    h = _rms(x_ref[...]) * g_ref[...]
    h = h * (1.0 + mod_ref[1:2, :]) + mod_ref[0:1, :]
    _store_lane_tiles(u_ref, _dot(h.astype(BF16), w_ref[...]))


def _uproj(x, mod4, l, mod_row, norm_g, w_bf, d_ssm, *, tm):
    nb, n, d = x.shape
    d_in = w_bf.shape[-1]
    col_blk = (d_in - d_ssm) // d_ssm
    u_shape, u_spec = _u_tiles(nb, n, d_ssm, tm, True)
    return pl.pallas_call(
        _uproj_body,
        out_shape=u_shape,
        grid=(nb, n // tm),
        in_specs=[
            pl.BlockSpec((None, tm, d), lambda b, i: (b, i, 0)),
            pl.BlockSpec((None, None, N_MOD, d), lambda b, i: (l, mod_row(b), 0, 0)),
            pl.BlockSpec((None, 1, d), lambda b, i: (l, 0, 0)),
            pl.BlockSpec((None, d, d_ssm), lambda b, i: (l, 0, col_blk)),
        ],
        out_specs=u_spec,
        compiler_params=_params(2, 32),
        name="uproj",
    )(x, mod4, norm_g, w_bf)


N_LEVELS = 7
POW_ROWS = 24


def _s5prep_body(kk_ref, lr_ref, li_ref, ldt_ref, bc1_ref, bc2_ref, ca_ref, cb_ref, dv_ref,
                 wcat_ref, mtot_ref, v_ref, a1_ref, a2_ref):
    t_ = CHUNK
    kk = kk_ref[...]
    lane_blk = jnp.right_shift(lax.broadcasted_iota(I32, (SSM_GROUP, CHUNK_W), 1), 4)
    lane = lax.broadcasted_iota(I32, (1, LANES), 1)
    sgn = jnp.where(lane < LANES // 2, -1.0, 1.0)
    eye = jnp.where(lax.broadcasted_iota(I32, (LANES, LANES), 0)
                    == lax.broadcasted_iota(I32, (LANES, LANES), 1), 1.0, 0.0)
    mtot = jnp.zeros((CHUNK_W, CHUNK_W), F32)
    for d in range(2):
        lr = lr_ref[d]
        li = li_ref[d]
        dt = jnp.exp(ldt_ref[d])
        pm = jnp.exp(kk * (dt * lr))
        ang = kk * (dt * li)
        pr = pm * jnp.cos(ang)
        pi = pm * jnp.sin(ang)
        ar = pr[1:2, :]
        ai = pi[1:2, :]
        den = lr * lr + li * li
        nr = ar - 1.0
        kr = (nr * lr + ai * li) / den
        ki = (ai * lr - nr * li) / den
        bc1 = bc1_ref[d]
        bc2 = bc2_ref[d]
        bb1 = kr * bc1 + ki * bc2
        bb2 = kr * bc2 - ki * bc1
        ca = ca_ref[d]
        cb = cb_ref[d]
        cak = [ca * pr[k:k + 1, :] + cb * pi[k:k + 1, :] for k in range(t_ + 1)]
        lag_order = range(t_) if d == 0 else range(t_ - 1, -1, -1)
        cak_all = jnp.concatenate([cak[k] for k in lag_order], axis=0)
        kall_t = _dot3(bb1, cak_all, _NT)
        rows = []
        for j in range(t_):
            if d == 0:
                shift, keep = (SSM_GROUP * j) % CHUNK_W, lane_blk >= j
            else:
                shift, keep = (SSM_GROUP * (j + 1)) % CHUNK_W, lane_blk <= j
            r = pltpu.roll(kall_t, shift, 1) if shift else kall_t
            rows.append(jnp.where(keep, r, 0.0))
        mtot = mtot + jnp.concatenate(rows, axis=0)
        e_v = [i + 1 for i in range(t_)] if d == 0 else [t_ - i for i in range(t_)]
        v_t = jnp.concatenate([cak[e] for e in e_v], axis=0)
        v_ref[d] = _dot3(eye, v_t, _NT).astype(v_ref.dtype)
        e_w = [t_ - 1 - j for j in range(t_)] if d == 0 else list(range(t_))
        w = jnp.concatenate([bb1 * pr[e:e + 1, :] + bb2 * pi[e:e + 1, :] for e in e_w], axis=0)
        wcat_ref[:, d * LANES:(d + 1) * LANES] = w.astype(wcat_ref.dtype)
        a1_ref[d] = pr[t_:t_ + 8, :]
        a2_ref[d] = sgn * pi[t_:t_ + 8, :]
    diag = (lax.broadcasted_iota(I32, (CHUNK_W, CHUNK_W), 0)
            == lax.broadcasted_iota(I32, (CHUNK_W, CHUNK_W), 1))
    mtot = mtot + jnp.where(diag, dv_ref[...], 0.0)
    mtot_ref[...] = mtot.astype(mtot_ref.dtype)


def _s5prep(l, kk, lr_t, li_t, ldt, bc1, bc2, ca, cb, dv):
    g = lr_t.shape[2]
    spec5 = lambda r, c: pl.BlockSpec((None, 2, None, r, c), lambda i: (l, 0, i, 0, 0))
    return pl.pallas_call(
        _s5prep_body,
        out_shape=(jax.ShapeDtypeStruct((g, CHUNK_W, 2 * LANES), BF16),
                   jax.ShapeDtypeStruct((g, CHUNK_W, CHUNK_W), BF16),
                   jax.ShapeDtypeStruct((2, g, LANES, CHUNK_W), BF16),
                   jax.ShapeDtypeStruct((2, g, 8, LANES), F32),
                   jax.ShapeDtypeStruct((2, g, 8, LANES), F32)),
        grid=(g,),
        in_specs=[pl.BlockSpec((POW_ROWS, 1), lambda i: (0, 0)),
                  spec5(1, LANES), spec5(1, LANES), spec5(1, 1),
                  spec5(SSM_GROUP, LANES), spec5(SSM_GROUP, LANES),
                  spec5(SSM_GROUP, LANES), spec5(SSM_GROUP, LANES),
                  pl.BlockSpec((None, None, 1, CHUNK_W), lambda i: (l, i, 0, 0))],
        out_specs=(pl.BlockSpec((None, CHUNK_W, 2 * LANES), lambda i: (i, 0, 0)),
                   pl.BlockSpec((None, CHUNK_W, CHUNK_W), lambda i: (i, 0, 0)),
                   pl.BlockSpec((2, None, LANES, CHUNK_W), lambda i: (0, i, 0, 0)),
                   pl.BlockSpec((2, None, 8, LANES), lambda i: (0, i, 0, 0)),
                   pl.BlockSpec((2, None, 8, LANES), lambda i: (0, i, 0, 0))),
        compiler_params=_params(1, 32),
        name="s5prep",
    )(kk, lr_t, li_t, ldt, bc1, bc2, ca, cb, dv)


def _s5mix_body(*refs, seg, with_h0):
    if with_h0:
        u_ref, wcat_ref, m_ref, v_ref, a1_ref, a2_ref, h0_ref, y_ref = refs
    else:
        u_ref, wcat_ref, m_ref, v_ref, a1_ref, a2_ref, y_ref, fin_ref, fs_ref = refs
    k4, n, _ = u_ref.shape
    c = n // CHUNK
    gpt = LANES // SSM_GROUP
    b = pl.program_id(0)
    rowm = jnp.bitwise_and(lax.broadcasted_iota(I32, (c, 1), 0), seg - 1)
    shifts = [1 << j for j in range(seg.bit_length() - 1)]

    def swap(x):
        return pltpu.roll(x, LANES // 2, 1)

    def scan(s, d, g):
        a1 = a1_ref[d, g]
        a2 = a2_ref[d, g]
        h0 = h0_ref[d, g, pl.ds(b, 1), :] if with_h0 else 0.0
        if d == 0:
            x = jnp.where(rowm == 0, h0, pltpu.roll(s, 1, 0))
        else:
            x = jnp.where(rowm == seg - 1, h0, pltpu.roll(s, c - 1, 0))
        for j, sh in enumerate(shifts):
            if d == 0:
                xs = jnp.where(rowm >= sh, pltpu.roll(x, sh, 0), 0.0)
            else:
                xs = jnp.where(rowm < seg - sh, pltpu.roll(x, c - sh, 0), 0.0)
            x = x + a1[j:j + 1, :] * xs + a2[j:j + 1, :] * swap(xs)
        return x

    for k in range(k4):
        xk = jnp.concatenate([u_ref[k, pl.ds(t, c, stride=CHUNK), :] for t in range(CHUNK)], axis=0)
        xt = xk.T
        yts = []
        for gg in range(gpt):
            g = k * gpt + gg
            rg = jnp.concatenate([xt[gg * SSM_GROUP:(gg + 1) * SSM_GROUP, t * c:(t + 1) * c]
                                  for t in range(CHUNK)], axis=0)
            z = rg.T.astype(BF16)
            s = _dot(z, wcat_ref[g])
            y = _dot(z, m_ref[g])
            for d in range(2):
                sd = s[:, d * LANES:(d + 1) * LANES]
                hin = scan(sd, d, g)
                y = y + _dot(hin.astype(BF16), v_ref[d, g])
                if not with_h0:
                    a1 = a1_ref[d, g]
                    a2 = a2_ref[d, g]
                    fs_ref[...] = a1[0:1, :] * hin + a2[0:1, :] * swap(hin) + sd
                    first = seg - 1 if d == 0 else 0
                    fin_ref[d, g] = fs_ref[pl.ds(first, c // seg, stride=seg), :]
            yts.append(y.T)
        xto = jnp.concatenate(
            [jnp.concatenate([yt[t * SSM_GROUP:(t + 1) * SSM_GROUP, :] for yt in yts], axis=0)
             for t in range(CHUNK)], axis=1)
        xo = xto.T
        for t in range(CHUNK):
            y_ref[k, pl.ds(t, c, stride=CHUNK), :] = xo[t * c:(t + 1) * c, :]


def _s5mix(u4, wcat, mtot, v, a1, a2, *, seg, h0=None):
    nb, k4, n, _ = u4.shape
    g = wcat.shape[0]
    c = n // CHUNK
    with_h0 = h0 is not None
    body = functools.partial(_s5mix_body, seg=seg, with_h0=with_h0)
    tile = pl.BlockSpec((None, k4, n, LANES), lambda b: (b, 0, 0, 0))
    res = lambda shape: pl.BlockSpec(shape, lambda b: (0,) * len(shape), pipeline_mode=pl.Buffered(1))
    in_specs = [tile, res((g, CHUNK_W, 2 * LANES)), res((g, CHUNK_W, CHUNK_W)), res((2, g, LANES, CHUNK_W)),
                res((2, g, 8, LANES)), res((2, g, 8, LANES))]
    args = [u4, wcat, mtot, v, a1, a2]
    y_shape = jax.ShapeDtypeStruct((nb, k4, n, LANES), F32)
    if with_h0:
        in_specs.append(res((2, g, h0.shape[2], LANES)))
        args.append(h0)
        out_shape, out_specs, scratch = y_shape, tile, []
    else:
        out_shape = (y_shape, jax.ShapeDtypeStruct((2, g, c // seg, LANES), F32))
        out_specs = (tile, pl.BlockSpec((2, g, c // seg, LANES), lambda b: (0, 0, 0, 0)))
        scratch = [pltpu.VMEM((c, LANES), F32)]
    return pl.pallas_call(
        body,
        out_shape=out_shape,
        grid=(nb,),
        in_specs=in_specs,
        out_specs=out_specs,
        scratch_shapes=scratch,
        compiler_params=_params(1, 48),
        name="s5mix",
    )(*args)


def _postmix_body(*refs, d_conv, aliased):
    if aliased:
        refs = refs[1:]
    (cn_ref, y_ref, x_ref, mod_ref, wglu_ref, gs_ref, wout_ref, g2_ref, r2_ref, r1_ref,
     xo_ref, h2_ref, lg_ref) = refs
    yg = jax.nn.gelu(jnp.concatenate([y_ref[k] for k in range(y_ref.shape[0])], axis=1))
    z = _dot(yg.astype(BF16), wglu_ref[...])
    s = yg * jax.nn.sigmoid(z)
    d_ssm = s.shape[-1]
    mix = _dot(cn_ref[...], wout_ref[0:d_conv, :])
    for hd in range(d_ssm // HEAD_DIM):
        lo = hd * HEAD_DIM
        sn = _rms(s[:, lo:lo + HEAD_DIM]) * gs_ref[:, lo:lo + HEAD_DIM]
        mix = mix + _dot(sn.astype(BF16), wout_ref[d_conv + lo:d_conv + lo + HEAD_DIM, :])
    xn = x_ref[...] + mod_ref[2:3, :] * mix
    xo_ref[...] = xn
    h2 = _rms(xn) * g2_ref[...]
    h2 = h2 * (1.0 + mod_ref[4:5, :]) + mod_ref[3:4, :]
    hi, lo_ = _split_bf16(h2)
    tm, d = h2.shape
    for k in range(d // LANES):
        h2_ref[pl.ds(k, tm, stride=d // LANES), :] = h2[:, k * LANES:(k + 1) * LANES]
    d1 = _dot(hi, r2_ref[...])
    d2 = _dot(lo_, r1_ref[...])
    lg_ref[...] = d1[:, 0:LANES] + d1[:, LANES:2 * LANES] + d2


def _postmix(conv_n, y4, x, mod4, l, mod_row, wglu_bf, gs, wout_bf, norm2_g, r2, r1, *, tm, pseudo,
             table_tokens, table=None, tok0=0):
    nb, n, d = x.shape
    d_conv = conv_n.shape[-1]
    d_ssm = y4.shape[1] * LANES
    tok_rows = d // LANES
    aliased = table is not None
    body = functools.partial(_postmix_body, d_conv=d_conv, aliased=aliased)
    _, y_spec = _u_tiles(nb, n, d_ssm, tm, pseudo)
    tok = lambda w: pl.BlockSpec((None, tm, w), lambda b, i: (b, i, 0))
    lay = lambda r, c, **kw: pl.BlockSpec((None, r, c), lambda b, i: (l, 0, 0), **kw)
    in_specs = [tok(d_conv), y_spec, tok(d),
                pl.BlockSpec((None, None, N_MOD, d), lambda b, i: (l, mod_row(b), 0, 0)),
                lay(d_ssm, d_ssm, pipeline_mode=pl.Buffered(1)),
                lay(1, d_ssm),
                lay(d_conv + d_ssm, d, pipeline_mode=pl.Buffered(1)),
                lay(1, d),
                lay(d, 2 * LANES, pipeline_mode=pl.Buffered(1)),
                lay(d, LANES, pipeline_mode=pl.Buffered(1))]
    args = [conv_n, y4, x, mod4, wglu_bf, gs, wout_bf, norm2_g, r2, r1]
    if aliased:
        in_specs.insert(0, pl.BlockSpec(memory_space=pl.ANY))
        args.insert(0, table)
    blk0 = tok0 // tm
    return pl.pallas_call(
        body,
        out_shape=(jax.ShapeDtypeStruct((nb, n, d), F32),
                   jax.ShapeDtypeStruct((table_tokens * tok_rows, LANES), F32),
                   jax.ShapeDtypeStruct((nb, n, LANES), F32)),
        grid=(nb, n // tm),
        in_specs=in_specs,
        out_specs=(tok(d),
                   pl.BlockSpec((tm * tok_rows, LANES), lambda b, i: (blk0 + b * (n // tm) + i, 0)),
                   tok(LANES)),
        input_output_aliases={0: 1} if aliased else {},
        compiler_params=_params(2, 56),
        name="postmix",
    )(*args)


BISECT_STEPS = 32

def _route_body(lg_ref, pos_ref, wv_ref, cum_ref, tri_ref, *, cap):
    nb, ne, n = lg_ref.shape
    rc = min(256, n)
    for r0 in range(0, n, rc):
        ri = lax.broadcasted_iota(I32, (rc, n), 0) + r0
        ci = lax.broadcasted_iota(I32, (rc, n), 1)
        tri_ref[r0:r0 + rc, :] = jnp.where(ri < ci, 1.0, 0.0).astype(BF16)
    lg = lg_ref[...]
    e = jnp.exp(lg - jnp.max(lg, axis=1, keepdims=True))
    aff = e / jnp.sum(e, axis=1, keepdims=True)
    capf = float(cap)

    def enough(t):
        return jnp.sum(jnp.where(aff >= t, 1.0, 0.0), axis=2, keepdims=True) >= capf

    hi = jnp.full((nb, ne, 1), 2.0, F32)
    for j in range(6, -1, -1):
        cand = hi * (2.0 ** -(1 << j))
        hi = jnp.where(enough(cand), hi, cand)
    half = hi * 0.5
    lo = jnp.where(enough(half), half, 0.0)
    for _ in range(BISECT_STEPS):
        mid = (lo + hi) * 0.5
        ok = enough(mid)
        lo = jnp.where(ok, mid, lo)
        hi = jnp.where(ok, hi, mid)
    gt = aff >= hi
    eq = jnp.logical_and(aff >= lo, aff < hi)
    need = capf - jnp.sum(jnp.where(gt, 1.0, 0.0), axis=2, keepdims=True)
    tri = tri_ref[...]
    eq_rank = _dot(jnp.where(eq, 1.0, 0.0).astype(BF16).reshape(nb * ne, n), tri).reshape(nb, ne, n)
    sel = jnp.logical_or(gt, jnp.logical_and(eq, eq_rank < need))
    pos = _dot(jnp.where(sel, 1.0, 0.0).astype(BF16).reshape(nb * ne, n), tri).reshape(nb, ne, n)
    pos_ref[...] = jnp.where(sel, pos.astype(I32), -1)
    wv_ref[...] = jnp.where(sel, aff, 0.0)
    cum_ref[...] = pos.astype(I32)


def _route(logits_t, cap):
    nb, ne, n = logits_t.shape
    body = functools.partial(_route_body, cap=cap)
    full = pl.BlockSpec((nb, ne, n), lambda i: (0, 0, 0))
    return pl.pallas_call(
        body,
        out_shape=(jax.ShapeDtypeStruct((nb, ne, n), I32),
                   jax.ShapeDtypeStruct((nb, ne, n), F32),
                   jax.ShapeDtypeStruct((nb, ne, n), I32)),
        grid=(1,),
        in_specs=[full],
        out_specs=(full, full, full),
        scratch_shapes=[pltpu.VMEM((n, n), BF16)],
        compiler_params=_params(1, 48),
        name="route",
    )(logits_t)


def _slotidx_body(pos_ref, o_ref, *, cap):
    n, ne = pos_ref.shape
    pos = pos_ref[...]
    tok = lax.broadcasted_iota(I32, (n, 1), 0).astype(F32)
    slot = lax.broadcasted_iota(I32, (n, cap), 1)
    for e in range(ne):
        hit = jnp.where(slot == pos[:, e:e + 1], tok, 0.0)
        o_ref[e:e + 1, :] = jnp.sum(hit, axis=0, keepdims=True).astype(I32)


def _slotidx(pos_cols, cap):
    nb, n, ne = pos_cols.shape
    return pl.pallas_call(
        functools.partial(_slotidx_body, cap=cap),
        out_shape=jax.ShapeDtypeStruct((nb, ne, cap), I32),
        grid=(nb,),
        in_specs=[pl.BlockSpec((None, n, ne), lambda b: (b, 0, 0))],
        out_specs=pl.BlockSpec((None, ne, cap), lambda b: (b, 0, 0)),
        compiler_params=_params(1, 32),
        name="slotidx",
    )(pos_cols)


def _ffn_body(idx_ref, tab_ref, wg_ref, wu_ref, wd_ref, y_ref, xbuf_ref, xs_ref, acc_ref, sem, *, tok_rows, nf):
    f = pl.program_id(2)
    blk = pl.program_id(0) * pl.num_programs(1) + pl.program_id(1)
    nblk = pl.num_programs(0) * pl.num_programs(1)
    step = blk * nf + f
    mh = xs_ref.shape[1]
    ch = mh // nf
    cur = jnp.bitwise_and(blk, 1)
    ring = jnp.bitwise_and(step, 1)

    def token_copy(blk_i, c, j, ring_i):
        tok = idx_ref[blk_i * mh + c * ch + j]
        src = tab_ref.at[pl.ds(pl.multiple_of(tok * tok_rows, tok_rows), tok_rows), :]
        dst = xbuf_ref.at[ring_i, pl.ds(pl.multiple_of(j * tok_rows, tok_rows), tok_rows), :]
        return pltpu.make_async_copy(src, dst, sem.at[ring_i])

    def wait_chunk(ring_i):
        pltpu.make_async_copy(tab_ref.at[pl.ds(0, ch * tok_rows), :], xbuf_ref.at[ring_i], sem.at[ring_i]).wait()

    def convert(ring_i, xs_slot, c):
        rows = pl.ds(pl.multiple_of(c * ch, ch), ch)
        for k in range(tok_rows):
            w = xbuf_ref[ring_i, pl.ds(k, ch, stride=tok_rows), :]
            xs_ref[xs_slot, rows, k * LANES:(k + 1) * LANES] = w.astype(BF16)

    def issue_loop(blk_i, c, ring_i):
        def issue(j, carry):
            token_copy(blk_i, c, j, ring_i).start()
            return carry
        lax.fori_loop(0, ch, issue, 0)

    @pl.when(step == 0)
    def _():
        for c in range(nf):
            issue_loop(0, c, 0)
            wait_chunk(0)
            convert(0, 0, c)
        issue_loop(1 % nblk, 0, 1)

    q = step + nf + 1
    qb = q // nf
    qb = jnp.where(qb < nblk, qb, 0)
    for j in range(ch):
        token_copy(qb, q % nf, j, ring).start(priority=j % 2)

    xs = xs_ref[cur]
    g = _dot(xs, wg_ref[...].astype(BF16))
    u = _dot(xs, wu_ref[...].astype(BF16))
    act = (jax.nn.silu(g) * u).astype(BF16)
    acc = jnp.where(f == 0, 0.0, acc_ref[...]) + _dot(act, wd_ref[...].astype(BF16))
    acc_ref[...] = acc
    y_ref[...] = acc.astype(y_ref.dtype)

    wait_chunk(1 - ring)
    convert(1 - ring, 1 - cur, f)

    @pl.when(step == nblk * nf - 1)
    def _():
        wait_chunk(ring)


def _ffn(slot_tok, table, l, w_gate, w_up, w_down, *, tf, m_split):
    _, ne, d, dff = w_gate.shape
    tok_rows = d // LANES
    m = slot_tok.shape[0] // ne
    mh = m // m_split
    nf = dff // tf
    assert mh % nf == 0 and table.shape[1] == LANES
    body = functools.partial(_ffn_body, tok_rows=tok_rows, nf=nf)
    grid_spec = pltpu.PrefetchScalarGridSpec(
        num_scalar_prefetch=1,
        grid=(m_split, ne, nf),
        in_specs=[pl.BlockSpec(memory_space=pl.ANY),
                  pl.BlockSpec((None, None, d, tf), lambda h, e, f, idx: (l, e, 0, f)),
                  pl.BlockSpec((None, None, d, tf), lambda h, e, f, idx: (l, e, 0, f)),
                  pl.BlockSpec((None, None, tf, d), lambda h, e, f, idx: (l, e, f, 0))],
        out_specs=pl.BlockSpec((None, mh, d), lambda h, e, f, idx: (e, h, 0)),
        scratch_shapes=[pltpu.VMEM((2, mh // nf * tok_rows, LANES), F32),
                        pltpu.VMEM((2, mh, d), BF16),
                        pltpu.VMEM((mh, d), F32),
                        pltpu.SemaphoreType.DMA((2,))])
    return pl.pallas_call(
        body,
        out_shape=jax.ShapeDtypeStruct((ne, m, d), BF16),
        grid_spec=grid_spec,
        compiler_params=_params(3, 56),
        name="ffn",
    )(slot_tok, table, w_gate, w_up, w_down)


COMBINE_WIN = 64


def _combine_body(*refs, cap, final):
    if final:
        st_ref, y_ref, pos_ref, wv_ref, x_ref, mod_ref, gf_ref, o_ref = refs
    else:
        st_ref, y_ref, pos_ref, wv_ref, x_ref, mod_ref, o_ref = refs
    ne = y_ref.shape[0]
    tt = x_ref.shape[0]
    win = COMBINE_WIN
    nt1 = pl.num_programs(1) + 1
    base = pl.program_id(0) * ne * nt1 + pl.program_id(1)
    pos = pos_ref[...]
    wv = wv_ref[...]

    def finish(moe):
        xn = x_ref[...] + mod_ref[5:6, :] * moe
        if final:
            xn = _rms(xn) * gf_ref[...]
        o_ref[...] = xn

    w0s = []
    fits = None
    for e in range(ne):
        s0 = st_ref[base + e * nt1]
        s1 = st_ref[base + e * nt1 + 1]
        w0 = jnp.minimum(jnp.left_shift(jnp.right_shift(s0, 4), 4), cap - win)
        ok = s1 - w0 <= win
        fits = ok if fits is None else jnp.logical_and(fits, ok)
        w0s.append(pl.multiple_of(w0, 16))

    @pl.when(fits)
    def _():
        lane = lax.broadcasted_iota(I32, (tt, LANES), 1)
        low = lane < win
        pieces, rows = [], []
        for e in range(0, ne, 2):
            tgt = jnp.where(low, lane + w0s[e], lane + (w0s[e + 1] - win))
            p = jnp.where(low, pos[:, e:e + 1], pos[:, e + 1:e + 2])
            w = jnp.where(low, wv[:, e:e + 1], wv[:, e + 1:e + 2])
            pieces.append(jnp.where(tgt == p, w, 0.0).astype(BF16))
            rows.append(y_ref[e, pl.ds(w0s[e], win), :])
            rows.append(y_ref[e + 1, pl.ds(w0s[e + 1], win), :])
        finish(_dot(jnp.concatenate(pieces, axis=1), jnp.concatenate(rows, axis=0)))

    @pl.when(jnp.logical_not(fits))
    def _():
        slot = lax.broadcasted_iota(I32, (tt, cap), 1)
        pieces = [jnp.where(slot == pos[:, e:e + 1], wv[:, e:e + 1], 0.0).astype(BF16) for e in range(ne)]
        finish(_dot(jnp.concatenate(pieces, axis=1), y_ref[...].reshape(ne * cap, y_ref.shape[-1])))


def _combine(starts, y, blk0, pos_cols, wv_cols, x, mod4, l, mod_row, cap, final_g=None, *, tt):
    nb, n, d = x.shape
    ne = y.shape[0]
    final = final_g is not None
    assert 2 * COMBINE_WIN == LANES and ne % 2 == 0 and cap % 16 == 0 and cap >= COMBINE_WIN
    body = functools.partial(_combine_body, cap=cap, final=final)
    in_specs = [pl.BlockSpec((ne, cap, d), lambda b, i, st: (0, blk0 + b, 0)),
                pl.BlockSpec((None, tt, ne), lambda b, i, st: (b, i, 0)),
                pl.BlockSpec((None, tt, ne), lambda b, i, st: (b, i, 0)),
                pl.BlockSpec((None, tt, d), lambda b, i, st: (b, i, 0)),
                pl.BlockSpec((None, None, N_MOD, d), lambda b, i, st: (l, mod_row(b), 0, 0))]
    args = [starts, y, pos_cols, wv_cols, x, mod4]
    if final:
        in_specs.append(pl.BlockSpec((1, d), lambda b, i, st: (0, 0)))
        args.append(final_g)
    grid_spec = pltpu.PrefetchScalarGridSpec(
        num_scalar_prefetch=1,
        grid=(nb, n // tt),
        in_specs=in_specs,
        out_specs=pl.BlockSpec((None, tt, d), lambda b, i, st: (b, i, 0)))
    return pl.pallas_call(
        body,
        out_shape=jax.ShapeDtypeStruct((nb, n, d), F32),
        grid_spec=grid_spec,
        compiler_params=_params(2, 56),
        name="combine",
    )(*args)


def kernel(x, c, ctx, c_ctx, ada_w, ada_b, norm1_g, w_in, conv_w, ssm_lam_re, ssm_lam_im, ssm_log_dt,
           ssm_b_re, ssm_b_im, ssm_c_re, ssm_c_im, ssm_d, ssm_w_glu, out_norm_conv_g, out_norm_ssm_g,
           w_out, norm2_g, router_w, exp_w_gate, exp_w_up, exp_w_down, final_norm_g):
    bsz, n, d = x.shape
    nc = ctx.shape[1]
    depth = ada_w.shape[0]
    d_conv = conv_w.shape[-1]
    d_ssm = ssm_d.shape[-1]
    g = d_ssm // SSM_GROUP
    p = ssm_lam_re.shape[-1]
    ne = router_w.shape[-1]
    dff = exp_w_gate.shape[-1]
    cap = EC_FACTOR * n // ne
    cap_c = EC_FACTOR * nc // ne
    nbc = bsz * nc
    assert bsz == 8 and 2 * p == LANES and ssm_b_re.shape[-1] == SSM_GROUP
    assert nbc == n and bsz * cap_c == cap
    assert n // CHUNK == 1 << N_LEVELS and nc % CHUNK == 0 and (nc // CHUNK) & (nc // CHUNK - 1) == 0
    assert d_conv % HEAD_DIM == 0 and d_ssm % HEAD_DIM == 0 and (3 * d_conv) % d_ssm == 0
    assert N_MOD * d == ada_w.shape[-1] and ne <= LANES

    tm = min(512, n)
    tt = min(256, n)
    tf = min(256, dff)

    rows = 16
    cvec = jnp.zeros((rows, d), F32).at[:bsz].set(c).at[bsz].set(c_ctx)
    mod4 = _ada(cvec, ada_w, ada_b).reshape(depth, rows, N_MOD, d)
    lat_row = lambda b: b
    ctx_row = lambda b: bsz

    w_in_bf = w_in.astype(BF16)
    w_out_bf = w_out.astype(BF16)
    w_glu_bf = ssm_w_glu.astype(BF16)
    rw_hi, rw_lo = _split_bf16(router_w)
    zpad = jnp.zeros((depth, d, LANES - ne), BF16)
    r1 = jnp.concatenate([rw_hi, zpad], axis=-1)
    r2 = jnp.concatenate([rw_hi, zpad, rw_lo, zpad], axis=-1)

    tile2 = lambda a: jnp.concatenate([a, a], axis=-1)
    lr_t = tile2(ssm_lam_re)[:, :, :, None, :]
    li_t = tile2(ssm_lam_im)[:, :, :, None, :]
    ldt = ssm_log_dt[:, :, :, None, None]
    brt = jnp.swapaxes(ssm_b_re, -1, -2)
    bit = jnp.swapaxes(ssm_b_im, -1, -2)
    bc1 = jnp.concatenate([brt, bit], axis=-1)
    bc2 = jnp.concatenate([-bit, brt], axis=-1)
    ca = jnp.concatenate([ssm_c_re, -ssm_c_im], axis=-1)
    cb = jnp.concatenate([-ssm_c_im, -ssm_c_re], axis=-1)
    dv = jnp.tile(ssm_d.reshape(depth, g, 1, SSM_GROUP), (1, 1, 1, CHUNK))

    g1n = norm1_g.reshape(depth, 1, d)
    g2n = norm2_g.reshape(depth, 1, d)
    gcn = out_norm_conv_g.reshape(depth, 1, d_conv)
    gsn = out_norm_ssm_g.reshape(depth, 1, d_ssm)
    gfin = final_norm_g.reshape(1, d)

    kk = jnp.asarray([float(k) for k in range(CHUNK + 1)]
                     + [float(CHUNK << j) for j in range(1, N_LEVELS)] + [0.0], F32).reshape(POW_ROWS, 1)

    xl = x
    xc = ctx
    for l in range(depth):
        last = l == depth - 1
        conv_l, u_lat = _inproj(xl, mod4, l, lat_row, g1n, w_in_bf, conv_w, gcn, rowlen=GRID_W, tm=tm,
                                pseudo=False)
        if last:
            u_ctx = _uproj(xc, mod4, l, ctx_row, g1n, w_in_bf, d_ssm, tm=nc)
        else:
            conv_c, u_ctx = _inproj(xc, mod4, l, ctx_row, g1n, w_in_bf, conv_w, gcn, rowlen=nc, tm=nc,
                                    pseudo=True)

        wcat, mtot, vmat, a1, a2 = _s5prep(l, kk, lr_t, li_t, ldt, bc1, bc2, ca, cb, dv)
        y_ctx, h0 = _s5mix(u_ctx, wcat, mtot, vmat, a1, a2, seg=nc // CHUNK)
        y_lat = _s5mix(u_lat, wcat, mtot, vmat, a1, a2, seg=n // CHUNK, h0=h0)

        n_tab = bsz * n + (0 if last else nbc)
        table = None if last else jnp.zeros((n_tab * (d // LANES), LANES), F32)
        xl, table, lg = _postmix(conv_l, y_lat, xl, mod4, l, lat_row, w_glu_bf, gsn, w_out_bf, g2n, r2, r1,
                                 tm=tm, pseudo=False, table_tokens=n_tab, table=table)
        lg_t = jnp.swapaxes(lg[:, :, :ne], 1, 2)
        pos, wv, cum = _route(lg_t, cap)
        capcol = jnp.full((bsz, ne, 1), cap, I32)
        starts = jnp.concatenate([cum[:, :, ::tt], capcol], axis=2).reshape(-1)
        pos_cols = jnp.swapaxes(pos, 1, 2)
        wv_cols = jnp.swapaxes(wv, 1, 2)
        offs_n = (jnp.arange(bsz, dtype=I32) * n)[:, None, None]
        slot_tok = jnp.swapaxes(_slotidx(pos_cols, cap) + offs_n, 0, 1).reshape(ne, bsz * cap)

        if not last:
            xc, table, lgc = _postmix(conv_c, y_ctx, xc, mod4, l, ctx_row, w_glu_bf, gsn, w_out_bf, g2n,
                                      r2, r1, tm=nc, pseudo=True, table_tokens=n_tab, table=table, tok0=bsz * n)
            lgc_t = jnp.swapaxes(lgc[:, :, :ne], 1, 2)
            posc, wvc, cumc = _route(lgc_t, cap_c)
            offs = (jnp.arange(bsz, dtype=I32) * cap_c)[:, None, None]
            posc = jnp.where(posc >= 0, posc + offs, -1)
            posc_cols = jnp.swapaxes(posc, 1, 2).reshape(1, nbc, ne)
            wvc_cols = jnp.swapaxes(wvc, 1, 2).reshape(1, nbc, ne)
            startsc = jnp.swapaxes(cumc[:, :, ::tt] + offs, 0, 1).reshape(ne, -1)
            startsc = jnp.concatenate([startsc, jnp.full((ne, 1), cap, I32)], axis=1).reshape(-1)
            slot_tok = jnp.concatenate([slot_tok, _slotidx(posc_cols, cap)[0] + bsz * n], axis=1)

        m_split = 2
        mh = slot_tok.shape[1] // m_split
        slot_tok = jnp.swapaxes(slot_tok.reshape(ne, m_split, mh), 0, 1).reshape(-1)
        y_exp = _ffn(slot_tok, table, l, exp_w_gate, exp_w_up, exp_w_down, tf=tf, m_split=m_split)
        xl = _combine(starts, y_exp, 0, pos_cols, wv_cols, xl, mod4, l, lat_row, cap,
                      gfin if last else None, tt=tt)
        if not last:
            xc = _combine(startsc, y_exp, bsz, posc_cols, wvc_cols, xc.reshape(1, nbc, d), mod4, l, ctx_row, cap,
                          tt=tt).reshape(bsz, nc, d)
    return xl
```

```python
import functools

import jax
import jax.numpy as jnp
from jax import lax
from jax.experimental import pallas as pl
from jax.experimental.pallas import tpu as pltpu

F32 = jnp.float32
BF16 = jnp.bfloat16
I32 = jnp.int32

EPS = 1e-6
GRID_W = 64
HEAD_DIM = 128
SSM_GROUP = 16
EC_FACTOR = 2
N_MOD = 6
CHUNK = 16
LANES = 128
CHUNK_W = CHUNK * SSM_GROUP

_MIB = 1 << 20


def _params(n_axes, vmem_mib):
    return pltpu.CompilerParams(dimension_semantics=("arbitrary",) * n_axes,
                                vmem_limit_bytes=vmem_mib * _MIB)


def _split_bf16(a):
    hi = a.astype(BF16)
    lo = (a - hi.astype(F32)).astype(BF16)
    return hi, lo


_NN = (((1,), (0,)), ((), ()))
_NT = (((1,), (1,)), ((), ()))


def _dot(a, b, dims=_NN):
    return lax.dot_general(a, b, dims, preferred_element_type=F32)


def _dot3(a, b, dims=_NN):
    ah, al = _split_bf16(a)
    bh, bl = _split_bf16(b)
    return _dot(ah, bh, dims) + _dot(ah, bl, dims) + _dot(al, bh, dims)


def _rms(x):
    return x * lax.rsqrt(jnp.mean(x * x, axis=-1, keepdims=True) + EPS)


def _ada_body(c_ref, w_ref, b_ref, o_ref):
    s = jax.nn.silu(c_ref[...])
    o_ref[...] = _dot3(s, w_ref[...]) + b_ref[...]


def _ada(cvec, ada_w, ada_b):
    depth, d, n6 = ada_w.shape
    tn = next(t for t in (1024, 512, 256, 128) if n6 % t == 0)
    rows = cvec.shape[0]
    return pl.pallas_call(
        _ada_body,
        out_shape=jax.ShapeDtypeStruct((depth, rows, n6), F32),
        grid=(depth, n6 // tn),
        in_specs=[
            pl.BlockSpec((rows, d), lambda l, j: (0, 0)),
            pl.BlockSpec((None, d, tn), lambda l, j: (l, 0, j)),
            pl.BlockSpec((None, 1, tn), lambda l, j: (l, 0, j)),
        ],
        out_specs=pl.BlockSpec((None, rows, tn), lambda l, j: (l, 0, j)),
        compiler_params=_params(2, 40),
        name="ada",
    )(cvec, ada_w, ada_b.reshape(depth, 1, n6))


def _store_lane_tiles(u_ref, u):
    for k in range(u_ref.shape[0]):
        u_ref[k] = u[:, k * LANES:(k + 1) * LANES]


def _u_tiles(nb, n, d_ssm, tm, pseudo):
    k4 = d_ssm // LANES
    if pseudo:
        return (jax.ShapeDtypeStruct((1, k4, nb * n, LANES), F32),
                pl.BlockSpec((None, k4, tm, LANES), lambda b, i: (0, 0, b * (n // tm) + i, 0)))
    return (jax.ShapeDtypeStruct((nb, k4, n, LANES), F32),
            pl.BlockSpec((None, k4, tm, LANES), lambda b, i: (b, 0, i, 0)))


def _inproj_body(x_ref, mod_ref, g_ref, w_ref, cw_ref, gc_ref, conv_ref, u_ref, *, rowlen, d_conv, cn):
    x = x_ref[...]
    tm = x.shape[0]
    h = _rms(x) * g_ref[...]
    h = h * (1.0 + mod_ref[1:2, :]) + mod_ref[0:1, :]
    hb = h.astype(BF16)
    t = jnp.bitwise_and(lax.broadcasted_iota(I32, (tm, 1), 0), rowlen - 1)
    first = t == 0
    last = t == rowlen - 1
    for j in range(d_conv // cn):
        c0 = j * cn
        bg = _dot(hb, w_ref[:, c0:c0 + cn])
        cg = _dot(hb, w_ref[:, d_conv + c0:d_conv + c0 + cn])
        v = _dot(hb, w_ref[:, 2 * d_conv + c0:2 * d_conv + c0 + cn])
        z = cg * v
        zp = jnp.where(first, 0.0, pltpu.roll(z, 1, 0))
        zn = jnp.where(last, 0.0, pltpu.roll(z, tm - 1, 0))
        cw = cw_ref[:, c0:c0 + cn]
        y = bg * (cw[0:1, :] * zp + cw[1:2, :] * z + cw[2:3, :] * zn)
        for hd in range(cn // HEAD_DIM):
            lo = hd * HEAD_DIM
            yh = _rms(y[:, lo:lo + HEAD_DIM]) * gc_ref[:, c0 + lo:c0 + lo + HEAD_DIM]
            conv_ref[:, c0 + lo:c0 + lo + HEAD_DIM] = yh.astype(conv_ref.dtype)
    _store_lane_tiles(u_ref, _dot(hb, w_ref[:, 3 * d_conv:]))


def _inproj(x, mod4, l, mod_row, norm_g, w_bf, conv_w, gc, *, rowlen, tm, pseudo):
    nb, n, d = x.shape
    depth, _, d_in = w_bf.shape
    d_conv = conv_w.shape[-1]
    d_ssm = d_in - 3 * d_conv
    cn = min(512, d_conv)
    body = functools.partial(_inproj_body, rowlen=rowlen, d_conv=d_conv, cn=cn)
    u_shape, u_spec = _u_tiles(nb, n, d_ssm, tm, pseudo)
    return pl.pallas_call(
        body,
        out_shape=(jax.ShapeDtypeStruct((nb, n, d_conv), BF16), u_shape),
        grid=(nb, n // tm),
        in_specs=[
            pl.BlockSpec((None, tm, d), lambda b, i: (b, i, 0)),
            pl.BlockSpec((None, None, N_MOD, d), lambda b, i: (l, mod_row(b), 0, 0)),
            pl.BlockSpec((None, 1, d), lambda b, i: (l, 0, 0)),
            pl.BlockSpec((None, d, d_in), lambda b, i: (l, 0, 0), pipeline_mode=pl.Buffered(1)),
            pl.BlockSpec((None, 3, d_conv), lambda b, i: (l, 0, 0)),
            pl.BlockSpec((None, 1, d_conv), lambda b, i: (l, 0, 0)),
        ],
        out_specs=(pl.BlockSpec((None, tm, d_conv), lambda b, i: (b, i, 0)), u_spec),
        compiler_params=_params(2, 56),
        name="inproj",
    )(x, mod4, norm_g, w_bf, conv_w, gc)


def _uproj_body(x_ref, mod_ref, g_ref, w_ref, u_ref):
    h = _rms(x_ref[...]) * g_ref[...]
    h = h * (1.0 + mod_ref[1:2, :]) + mod_ref[0:1, :]
    _store_lane_tiles(u_ref, _dot(h.astype(BF16), w_ref[...]))


def _uproj(x, mod4, l, mod_row, norm_g, w_bf, d_ssm, *, tm):
    nb, n, d = x.shape
    d_in = w_bf.shape[-1]
    col_blk = (d_in - d_ssm) // d_ssm
    u_shape, u_spec = _u_tiles(nb, n, d_ssm, tm, True)
    return pl.pallas_call(
        _uproj_body,
        out_shape=u_shape,
        grid=(nb, n // tm),
        in_specs=[
            pl.BlockSpec((None, tm, d), lambda b, i: (b, i, 0)),
            pl.BlockSpec((None, None, N_MOD, d), lambda b, i: (l, mod_row(b), 0, 0)),
            pl.BlockSpec((None, 1, d), lambda b, i: (l, 0, 0)),
            pl.BlockSpec((None, d, d_ssm), lambda b, i: (l, 0, col_blk)),
        ],
        out_specs=u_spec,
        compiler_params=_params(2, 32),
        name="uproj",
    )(x, mod4, norm_g, w_bf)


N_LEVELS = 7
POW_ROWS = 24


def _s5prep_body(kk_ref, lr_ref, li_ref, ldt_ref, bc1_ref, bc2_ref, ca_ref, cb_ref, dv_ref,
                 wcat_ref, mtot_ref, v_ref, a1_ref, a2_ref):
    t_ = CHUNK
    kk = kk_ref[...]
    lane_blk = jnp.right_shift(lax.broadcasted_iota(I32, (SSM_GROUP, CHUNK_W), 1), 4)
    lane = lax.broadcasted_iota(I32, (1, LANES), 1)
    sgn = jnp.where(lane < LANES // 2, -1.0, 1.0)
    eye = jnp.where(lax.broadcasted_iota(I32, (LANES, LANES), 0)
                    == lax.broadcasted_iota(I32, (LANES, LANES), 1), 1.0, 0.0)
    mtot = jnp.zeros((CHUNK_W, CHUNK_W), F32)
    for d in range(2):
        lr = lr_ref[d]
        li = li_ref[d]
        dt = jnp.exp(ldt_ref[d])
        pm = jnp.exp(kk * (dt * lr))
        ang = kk * (dt * li)
        pr = pm * jnp.cos(ang)
        pi = pm * jnp.sin(ang)
        ar = pr[1:2, :]
        ai = pi[1:2, :]
        den = lr * lr + li * li
        nr = ar - 1.0
        kr = (nr * lr + ai * li) / den
        ki = (ai * lr - nr * li) / den
        bc1 = bc1_ref[d]
        bc2 = bc2_ref[d]
        bb1 = kr * bc1 + ki * bc2
        bb2 = kr * bc2 - ki * bc1
        ca = ca_ref[d]
        cb = cb_ref[d]
        cak = [ca * pr[k:k + 1, :] + cb * pi[k:k + 1, :] for k in range(t_ + 1)]
        lag_order = range(t_) if d == 0 else range(t_ - 1, -1, -1)
        cak_all = jnp.concatenate([cak[k] for k in lag_order], axis=0)
        kall_t = _dot3(bb1, cak_all, _NT)
        rows = []
        for j in range(t_):
            if d == 0:
                shift, keep = (SSM_GROUP * j) % CHUNK_W, lane_blk >= j
            else:
                shift, keep = (SSM_GROUP * (j + 1)) % CHUNK_W, lane_blk <= j
            r = pltpu.roll(kall_t, shift, 1) if shift else kall_t
            rows.append(jnp.where(keep, r, 0.0))
        mtot = mtot + jnp.concatenate(rows, axis=0)
        e_v = [i + 1 for i in range(t_)] if d == 0 else [t_ - i for i in range(t_)]
        v_t = jnp.concatenate([cak[e] for e in e_v], axis=0)
        v_ref[d] = _dot3(eye, v_t, _NT).astype(v_ref.dtype)
        e_w = [t_ - 1 - j for j in range(t_)] if d == 0 else list(range(t_))
        w = jnp.concatenate([bb1 * pr[e:e + 1, :] + bb2 * pi[e:e + 1, :] for e in e_w], axis=0)
        wcat_ref[:, d * LANES:(d + 1) * LANES] = w.astype(wcat_ref.dtype)
        a1_ref[d] = pr[t_:t_ + 8, :]
        a2_ref[d] = sgn * pi[t_:t_ + 8, :]
    diag = (lax.broadcasted_iota(I32, (CHUNK_W, CHUNK_W), 0)
            == lax.broadcasted_iota(I32, (CHUNK_W, CHUNK_W), 1))
    mtot = mtot + jnp.where(diag, dv_ref[...], 0.0)
    mtot_ref[...] = mtot.astype(mtot_ref.dtype)


def _s5prep(l, kk, lr_t, li_t, ldt, bc1, bc2, ca, cb, dv):
    g = lr_t.shape[2]
    spec5 = lambda r, c: pl.BlockSpec((None, 2, None, r, c), lambda i: (l, 0, i, 0, 0))
    return pl.pallas_call(
        _s5prep_body,
        out_shape=(jax.ShapeDtypeStruct((g, CHUNK_W, 2 * LANES), BF16),
                   jax.ShapeDtypeStruct((g, CHUNK_W, CHUNK_W), BF16),
                   jax.ShapeDtypeStruct((2, g, LANES, CHUNK_W), BF16),
                   jax.ShapeDtypeStruct((2, g, 8, LANES), F32),
                   jax.ShapeDtypeStruct((2, g, 8, LANES), F32)),
        grid=(g,),
        in_specs=[pl.BlockSpec((POW_ROWS, 1), lambda i: (0, 0)),
                  spec5(1, LANES), spec5(1, LANES), spec5(1, 1),
                  spec5(SSM_GROUP, LANES), spec5(SSM_GROUP, LANES),
                  spec5(SSM_GROUP, LANES), spec5(SSM_GROUP, LANES),
                  pl.BlockSpec((None, None, 1, CHUNK_W), lambda i: (l, i, 0, 0))],
        out_specs=(pl.BlockSpec((None, CHUNK_W, 2 * LANES), lambda i: (i, 0, 0)),
                   pl.BlockSpec((None, CHUNK_W, CHUNK_W), lambda i: (i, 0, 0)),
                   pl.BlockSpec((2, None, LANES, CHUNK_W), lambda i: (0, i, 0, 0)),
                   pl.BlockSpec((2, None, 8, LANES), lambda i: (0, i, 0, 0)),
                   pl.BlockSpec((2, None, 8, LANES), lambda i: (0, i, 0, 0))),
        compiler_params=_params(1, 32),
        name="s5prep",
    )(kk, lr_t, li_t, ldt, bc1, bc2, ca, cb, dv)


def _s5mix_body(*refs, seg, with_h0):
    if with_h0:
        u_ref, wcat_ref, m_ref, v_ref, a1_ref, a2_ref, h0_ref, y_ref = refs
    else:
        u_ref, wcat_ref, m_ref, v_ref, a1_ref, a2_ref, y_ref, fin_ref, fs_ref = refs
    k4, n, _ = u_ref.shape
    c = n // CHUNK
    gpt = LANES // SSM_GROUP
    b = pl.program_id(0)
    rowm = jnp.bitwise_and(lax.broadcasted_iota(I32, (c, 1), 0), seg - 1)
    shifts = [1 << j for j in range(seg.bit_length() - 1)]

    def swap(x):
        return pltpu.roll(x, LANES // 2, 1)

    def scan(s, d, g):
        a1 = a1_ref[d, g]
        a2 = a2_ref[d, g]
        h0 = h0_ref[d, g, pl.ds(b, 1), :] if with_h0 else 0.0
        if d == 0:
            x = jnp.where(rowm == 0, h0, pltpu.roll(s, 1, 0))
        else:
            x = jnp.where(rowm == seg - 1, h0, pltpu.roll(s, c - 1, 0))
        for j, sh in enumerate(shifts):
            if d == 0:
                xs = jnp.where(rowm >= sh, pltpu.roll(x, sh, 0), 0.0)
            else:
                xs = jnp.where(rowm < seg - sh, pltpu.roll(x, c - sh, 0), 0.0)
            x = x + a1[j:j + 1, :] * xs + a2[j:j + 1, :] * swap(xs)
        return x

    for k in range(k4):
        xk = jnp.concatenate([u_ref[k, pl.ds(t, c, stride=CHUNK), :] for t in range(CHUNK)], axis=0)
        xt = xk.T
        yts = []
        for gg in range(gpt):
            g = k * gpt + gg
            rg = jnp.concatenate([xt[gg * SSM_GROUP:(gg + 1) * SSM_GROUP, t * c:(t + 1) * c]
                                  for t in range(CHUNK)], axis=0)
            z = rg.T.astype(BF16)
            s = _dot(z, wcat_ref[g])
            y = _dot(z, m_ref[g])
            for d in range(2):
                sd = s[:, d * LANES:(d + 1) * LANES]
                hin = scan(sd, d, g)
                y = y + _dot(hin.astype(BF16), v_ref[d, g])
                if not with_h0:
                    a1 = a1_ref[d, g]
                    a2 = a2_ref[d, g]
                    fs_ref[...] = a1[0:1, :] * hin + a2[0:1, :] * swap(hin) + sd
                    first = seg - 1 if d == 0 else 0
                    fin_ref[d, g] = fs_ref[pl.ds(first, c // seg, stride=seg), :]
            yts.append(y.T)
        xto = jnp.concatenate(
            [jnp.concatenate([yt[t * SSM_GROUP:(t + 1) * SSM_GROUP, :] for yt in yts], axis=0)
             for t in range(CHUNK)], axis=1)
        xo = xto.T
        for t in range(CHUNK):
            y_ref[k, pl.ds(t, c, stride=CHUNK), :] = xo[t * c:(t + 1) * c, :]


def _s5mix(u4, wcat, mtot, v, a1, a2, *, seg, h0=None):
    nb, k4, n, _ = u4.shape
    g = wcat.shape[0]
    c = n // CHUNK
    with_h0 = h0 is not None
    body = functools.partial(_s5mix_body, seg=seg, with_h0=with_h0)
    tile = pl.BlockSpec((None, k4, n, LANES), lambda b: (b, 0, 0, 0))
    res = lambda shape: pl.BlockSpec(shape, lambda b: (0,) * len(shape), pipeline_mode=pl.Buffered(1))
    in_specs = [tile, res((g, CHUNK_W, 2 * LANES)), res((g, CHUNK_W, CHUNK_W)), res((2, g, LANES, CHUNK_W)),
                res((2, g, 8, LANES)), res((2, g, 8, LANES))]
    args = [u4, wcat, mtot, v, a1, a2]
    y_shape = jax.ShapeDtypeStruct((nb, k4, n, LANES), F32)
    if with_h0:
        in_specs.append(res((2, g, h0.shape[2], LANES)))
        args.append(h0)
        out_shape, out_specs, scratch = y_shape, tile, []
    else:
        out_shape = (y_shape, jax.ShapeDtypeStruct((2, g, c // seg, LANES), F32))
        out_specs = (tile, pl.BlockSpec((2, g, c // seg, LANES), lambda b: (0, 0, 0, 0)))
        scratch = [pltpu.VMEM((c, LANES), F32)]
    return pl.pallas_call(
        body,
        out_shape=out_shape,
        grid=(nb,),
        in_specs=in_specs,
        out_specs=out_specs,
        scratch_shapes=scratch,
        compiler_params=_params(1, 48),
        name="s5mix",
    )(*args)


def _postmix_body(*refs, d_conv, aliased):
    if aliased:
        refs = refs[1:]
    (cn_ref, y_ref, x_ref, mod_ref, wglu_ref, gs_ref, wout_ref, g2_ref, r2_ref, r1_ref,
     xo_ref, h2_ref, lg_ref) = refs
    yg = jax.nn.gelu(jnp.concatenate([y_ref[k] for k in range(y_ref.shape[0])], axis=1))
    z = _dot(yg.astype(BF16), wglu_ref[...])
    s = yg * jax.nn.sigmoid(z)
    d_ssm = s.shape[-1]
    mix = _dot(cn_ref[...], wout_ref[0:d_conv, :])
    for hd in range(d_ssm // HEAD_DIM):
        lo = hd * HEAD_DIM
        sn = _rms(s[:, lo:lo + HEAD_DIM]) * gs_ref[:, lo:lo + HEAD_DIM]
        mix = mix + _dot(sn.astype(BF16), wout_ref[d_conv + lo:d_conv + lo + HEAD_DIM, :])
    xn = x_ref[...] + mod_ref[2:3, :] * mix
    xo_ref[...] = xn
    h2 = _rms(xn) * g2_ref[...]
    h2 = h2 * (1.0 + mod_ref[4:5, :]) + mod_ref[3:4, :]
    hi, lo_ = _split_bf16(h2)
    tm, d = h2.shape
    for k in range(d // LANES):
        h2_ref[pl.ds(k, tm, stride=d // LANES), :] = h2[:, k * LANES:(k + 1) * LANES]
    d1 = _dot(hi, r2_ref[...])
    d2 = _dot(lo_, r1_ref[...])
    lg_ref[...] = d1[:, 0:LANES] + d1[:, LANES:2 * LANES] + d2


def _postmix(conv_n, y4, x, mod4, l, mod_row, wglu_bf, gs, wout_bf, norm2_g, r2, r1, *, tm, pseudo,
             table_tokens, table=None, tok0=0):
    nb, n, d = x.shape
    d_conv = conv_n.shape[-1]
    d_ssm = y4.shape[1] * LANES
    tok_rows = d // LANES
    aliased = table is not None
    body = functools.partial(_postmix_body, d_conv=d_conv, aliased=aliased)
    _, y_spec = _u_tiles(nb, n, d_ssm, tm, pseudo)
    tok = lambda w: pl.BlockSpec((None, tm, w), lambda b, i: (b, i, 0))
    lay = lambda r, c, **kw: pl.BlockSpec((None, r, c), lambda b, i: (l, 0, 0), **kw)
    in_specs = [tok(d_conv), y_spec, tok(d),
                pl.BlockSpec((None, None, N_MOD, d), lambda b, i: (l, mod_row(b), 0, 0)),
                lay(d_ssm, d_ssm, pipeline_mode=pl.Buffered(1)),
                lay(1, d_ssm),
                lay(d_conv + d_ssm, d, pipeline_mode=pl.Buffered(1)),
                lay(1, d),
                lay(d, 2 * LANES, pipeline_mode=pl.Buffered(1)),
                lay(d, LANES, pipeline_mode=pl.Buffered(1))]
    args = [conv_n, y4, x, mod4, wglu_bf, gs, wout_bf, norm2_g, r2, r1]
    if aliased:
        in_specs.insert(0, pl.BlockSpec(memory_space=pl.ANY))
        args.insert(0, table)
    blk0 = tok0 // tm
    return pl.pallas_call(
        body,
        out_shape=(jax.ShapeDtypeStruct((nb, n, d), F32),
                   jax.ShapeDtypeStruct((table_tokens * tok_rows, LANES), F32),
                   jax.ShapeDtypeStruct((nb, n, LANES), F32)),
        grid=(nb, n // tm),
        in_specs=in_specs,
        out_specs=(tok(d),
                   pl.BlockSpec((tm * tok_rows, LANES), lambda b, i: (blk0 + b * (n // tm) + i, 0)),
                   tok(LANES)),
        input_output_aliases={0: 1} if aliased else {},
        compiler_params=_params(2, 56),
        name="postmix",
    )(*args)


BISECT_STEPS = 32

def _route_body(lg_ref, pos_ref, wv_ref, cum_ref, tri_ref, *, cap):
    nb, ne, n = lg_ref.shape
    rc = min(256, n)
    for r0 in range(0, n, rc):
        ri = lax.broadcasted_iota(I32, (rc, n), 0) + r0
        ci = lax.broadcasted_iota(I32, (rc, n), 1)
        tri_ref[r0:r0 + rc, :] = jnp.where(ri < ci, 1.0, 0.0).astype(BF16)
    lg = lg_ref[...]
    e = jnp.exp(lg - jnp.max(lg, axis=1, keepdims=True))
    aff = e / jnp.sum(e, axis=1, keepdims=True)
    capf = float(cap)

    def enough(t):
        return jnp.sum(jnp.where(aff >= t, 1.0, 0.0), axis=2, keepdims=True) >= capf

    hi = jnp.full((nb, ne, 1), 2.0, F32)
    for j in range(6, -1, -1):
        cand = hi * (2.0 ** -(1 << j))
        hi = jnp.where(enough(cand), hi, cand)
    half = hi * 0.5
    lo = jnp.where(enough(half), half, 0.0)
    for _ in range(BISECT_STEPS):
        mid = (lo + hi) * 0.5
        ok = enough(mid)
        lo = jnp.where(ok, mid, lo)
        hi = jnp.where(ok, hi, mid)
    gt = aff >= hi
    eq = jnp.logical_and(aff >= lo, aff < hi)
    need = capf - jnp.sum(jnp.where(gt, 1.0, 0.0), axis=2, keepdims=True)
    tri = tri_ref[...]
    eq_rank = _dot(jnp.where(eq, 1.0, 0.0).astype(BF16).reshape(nb * ne, n), tri).reshape(nb, ne, n)
    sel = jnp.logical_or(gt, jnp.logical_and(eq, eq_rank < need))
    pos = _dot(jnp.where(sel, 1.0, 0.0).astype(BF16).reshape(nb * ne, n), tri).reshape(nb, ne, n)
    pos_ref[...] = jnp.where(sel, pos.astype(I32), -1)
    wv_ref[...] = jnp.where(sel, aff, 0.0)
    cum_ref[...] = pos.astype(I32)


def _route(logits_t, cap):
    nb, ne, n = logits_t.shape
    body = functools.partial(_route_body, cap=cap)
    full = pl.BlockSpec((nb, ne, n), lambda i: (0, 0, 0))
    return pl.pallas_call(
        body,
        out_shape=(jax.ShapeDtypeStruct((nb, ne, n), I32),
                   jax.ShapeDtypeStruct((nb, ne, n), F32),
                   jax.ShapeDtypeStruct((nb, ne, n), I32)),
        grid=(1,),
        in_specs=[full],
        out_specs=(full, full, full),
        scratch_shapes=[pltpu.VMEM((n, n), BF16)],
        compiler_params=_params(1, 48),
        name="route",
    )(logits_t)


def _slotidx_body(pos_ref, o_ref, *, cap):
    n, ne = pos_ref.shape
    pos = pos_ref[...]
    tok = lax.broadcasted_iota(I32, (n, 1), 0).astype(F32)
    slot = lax.broadcasted_iota(I32, (n, cap), 1)
    for e in range(ne):
        hit = jnp.where(slot == pos[:, e:e + 1], tok, 0.0)
        o_ref[e:e + 1, :] = jnp.sum(hit, axis=0, keepdims=True).astype(I32)


def _slotidx(pos_cols, cap):
    nb, n, ne = pos_cols.shape
    return pl.pallas_call(
        functools.partial(_slotidx_body, cap=cap),
        out_shape=jax.ShapeDtypeStruct((nb, ne, cap), I32),
        grid=(nb,),
        in_specs=[pl.BlockSpec((None, n, ne), lambda b: (b, 0, 0))],
        out_specs=pl.BlockSpec((None, ne, cap), lambda b: (b, 0, 0)),
        compiler_params=_params(1, 32),
        name="slotidx",
    )(pos_cols)


def _ffn_body(idx_ref, tab_ref, wg_ref, wu_ref, wd_ref, y_ref, xbuf_ref, xs_ref, acc_ref, sem, *, tok_rows, nf):
    f = pl.program_id(2)
    blk = pl.program_id(0) * pl.num_programs(1) + pl.program_id(1)
    nblk = pl.num_programs(0) * pl.num_programs(1)
    step = blk * nf + f
    mh = xs_ref.shape[1]
    ch = mh // nf
    cur = jnp.bitwise_and(blk, 1)
    ring = jnp.bitwise_and(step, 1)

    def token_copy(blk_i, c, j, ring_i):
        tok = idx_ref[blk_i * mh + c * ch + j]
        src = tab_ref.at[pl.ds(pl.multiple_of(tok * tok_rows, tok_rows), tok_rows), :]
        dst = xbuf_ref.at[ring_i, pl.ds(pl.multiple_of(j * tok_rows, tok_rows), tok_rows), :]
        return pltpu.make_async_copy(src, dst, sem.at[ring_i])

    def wait_chunk(ring_i):
        pltpu.make_async_copy(tab_ref.at[pl.ds(0, ch * tok_rows), :], xbuf_ref.at[ring_i], sem.at[ring_i]).wait()

    def convert(ring_i, xs_slot, c):
        rows = pl.ds(pl.multiple_of(c * ch, ch), ch)
        for k in range(tok_rows):
            w = xbuf_ref[ring_i, pl.ds(k, ch, stride=tok_rows), :]
            xs_ref[xs_slot, rows, k * LANES:(k + 1) * LANES] = w.astype(BF16)

    def issue_loop(blk_i, c, ring_i):
        def issue(j, carry):
            token_copy(blk_i, c, j, ring_i).start()
            return carry
        lax.fori_loop(0, ch, issue, 0)

    @pl.when(step == 0)
    def _():
        for c in range(nf):
            issue_loop(0, c, 0)
            wait_chunk(0)
            convert(0, 0, c)
        issue_loop(1 % nblk, 0, 1)

    @pl.when(f == 0)
    def _():
        acc_ref[...] = jnp.zeros_like(acc_ref)

    q = step + nf + 1
    qb = q // nf
    qb = jnp.where(qb < nblk, qb, 0)
    for j in range(ch):
        token_copy(qb, q % nf, j, ring).start(priority=j % 2)

    xs = xs_ref[cur]
    g = _dot(xs, wg_ref[...].astype(BF16))
    u = _dot(xs, wu_ref[...].astype(BF16))
    act = (jax.nn.silu(g) * u).astype(BF16)
    acc_ref[...] += _dot(act, wd_ref[...].astype(BF16))

    wait_chunk(1 - ring)
    convert(1 - ring, 1 - cur, f)

    @pl.when(f == nf - 1)
    def _():
        y_ref[...] = acc_ref[...].astype(y_ref.dtype)

    @pl.when(step == nblk * nf - 1)
    def _():
        wait_chunk(ring)


def _ffn(slot_tok, table, l, w_gate, w_up, w_down, *, tf, m_split):
    _, ne, d, dff = w_gate.shape
    tok_rows = d // LANES
    m = slot_tok.shape[0] // ne
    mh = m // m_split
    nf = dff // tf
    assert mh % nf == 0 and table.shape[1] == LANES
    body = functools.partial(_ffn_body, tok_rows=tok_rows, nf=nf)
    grid_spec = pltpu.PrefetchScalarGridSpec(
        num_scalar_prefetch=1,
        grid=(m_split, ne, nf),
        in_specs=[pl.BlockSpec(memory_space=pl.ANY),
                  pl.BlockSpec((None, None, d, tf), lambda h, e, f, idx: (l, e, 0, f)),
                  pl.BlockSpec((None, None, d, tf), lambda h, e, f, idx: (l, e, 0, f)),
                  pl.BlockSpec((None, None, tf, d), lambda h, e, f, idx: (l, e, f, 0))],
        out_specs=pl.BlockSpec((None, mh, d), lambda h, e, f, idx: (e, h, 0)),
        scratch_shapes=[pltpu.VMEM((2, mh // nf * tok_rows, LANES), F32),
                        pltpu.VMEM((2, mh, d), BF16),
                        pltpu.VMEM((mh, d), F32),
                        pltpu.SemaphoreType.DMA((2,))])
    return pl.pallas_call(
        body,
        out_shape=jax.ShapeDtypeStruct((ne, m, d), BF16),
        grid_spec=grid_spec,
        compiler_params=_params(3, 56),
        name="ffn",
    )(slot_tok, table, w_gate, w_up, w_down)


COMBINE_WIN = 64


def _combine_body(*refs, cap, final):
    if final:
        st_ref, y_ref, pos_ref, wv_ref, x_ref, mod_ref, gf_ref, o_ref = refs
    else:
        st_ref, y_ref, pos_ref, wv_ref, x_ref, mod_ref, o_ref = refs
    ne = y_ref.shape[0]
    tt = x_ref.shape[0]
    win = COMBINE_WIN
    nt1 = pl.num_programs(1) + 1
    base = pl.program_id(0) * ne * nt1 + pl.program_id(1)
    pos = pos_ref[...]
    wv = wv_ref[...]

    def finish(moe):
        xn = x_ref[...] + mod_ref[5:6, :] * moe
        if final:
            xn = _rms(xn) * gf_ref[...]
        o_ref[...] = xn

    w0s = []
    fits = None
    for e in range(ne):
        s0 = st_ref[base + e * nt1]
        s1 = st_ref[base + e * nt1 + 1]
        w0 = jnp.minimum(jnp.left_shift(jnp.right_shift(s0, 4), 4), cap - win)
        ok = s1 - w0 <= win
        fits = ok if fits is None else jnp.logical_and(fits, ok)
        w0s.append(pl.multiple_of(w0, 16))

    @pl.when(fits)
    def _():
        lane = lax.broadcasted_iota(I32, (tt, LANES), 1)
        low = lane < win
        pieces, rows = [], []
        for e in range(0, ne, 2):
            tgt = jnp.where(low, lane + w0s[e], lane + (w0s[e + 1] - win))
            p = jnp.where(low, pos[:, e:e + 1], pos[:, e + 1:e + 2])
            w = jnp.where(low, wv[:, e:e + 1], wv[:, e + 1:e + 2])
            pieces.append(jnp.where(tgt == p, w, 0.0).astype(BF16))
            rows.append(y_ref[e, pl.ds(w0s[e], win), :])
            rows.append(y_ref[e + 1, pl.ds(w0s[e + 1], win), :])
        finish(_dot(jnp.concatenate(pieces, axis=1), jnp.concatenate(rows, axis=0)))

    @pl.when(jnp.logical_not(fits))
    def _():
        slot = lax.broadcasted_iota(I32, (tt, cap), 1)
        pieces = [jnp.where(slot == pos[:, e:e + 1], wv[:, e:e + 1], 0.0).astype(BF16) for e in range(ne)]
        finish(_dot(jnp.concatenate(pieces, axis=1), y_ref[...].reshape(ne * cap, y_ref.shape[-1])))


def _combine(starts, y, blk0, pos_cols, wv_cols, x, mod4, l, mod_row, cap, final_g=None, *, tt):
    nb, n, d = x.shape
    ne = y.shape[0]
    final = final_g is not None
    assert 2 * COMBINE_WIN == LANES and ne % 2 == 0 and cap % 16 == 0 and cap >= COMBINE_WIN
    body = functools.partial(_combine_body, cap=cap, final=final)
    in_specs = [pl.BlockSpec((ne, cap, d), lambda b, i, st: (0, blk0 + b, 0)),
                pl.BlockSpec((None, tt, ne), lambda b, i, st: (b, i, 0)),
                pl.BlockSpec((None, tt, ne), lambda b, i, st: (b, i, 0)),
                pl.BlockSpec((None, tt, d), lambda b, i, st: (b, i, 0)),
                pl.BlockSpec((None, None, N_MOD, d), lambda b, i, st: (l, mod_row(b), 0, 0))]
    args = [starts, y, pos_cols, wv_cols, x, mod4]
    if final:
        in_specs.append(pl.BlockSpec((1, d), lambda b, i, st: (0, 0)))
        args.append(final_g)
    grid_spec = pltpu.PrefetchScalarGridSpec(
        num_scalar_prefetch=1,
        grid=(nb, n // tt),
        in_specs=in_specs,
        out_specs=pl.BlockSpec((None, tt, d), lambda b, i, st: (b, i, 0)))
    return pl.pallas_call(
        body,
        out_shape=jax.ShapeDtypeStruct((nb, n, d), F32),
        grid_spec=grid_spec,
        compiler_params=_params(2, 56),
        name="combine",
    )(*args)


def kernel(x, c, ctx, c_ctx, ada_w, ada_b, norm1_g, w_in, conv_w, ssm_lam_re, ssm_lam_im, ssm_log_dt,
           ssm_b_re, ssm_b_im, ssm_c_re, ssm_c_im, ssm_d, ssm_w_glu, out_norm_conv_g, out_norm_ssm_g,
           w_out, norm2_g, router_w, exp_w_gate, exp_w_up, exp_w_down, final_norm_g):
    bsz, n, d = x.shape
    nc = ctx.shape[1]
    depth = ada_w.shape[0]
    d_conv = conv_w.shape[-1]
    d_ssm = ssm_d.shape[-1]
    g = d_ssm // SSM_GROUP
    p = ssm_lam_re.shape[-1]
    ne = router_w.shape[-1]
    dff = exp_w_gate.shape[-1]
    cap = EC_FACTOR * n // ne
    cap_c = EC_FACTOR * nc // ne
    nbc = bsz * nc
    assert bsz == 8 and 2 * p == LANES and ssm_b_re.shape[-1] == SSM_GROUP
    assert nbc == n and bsz * cap_c == cap
    assert n // CHUNK == 1 << N_LEVELS and nc % CHUNK == 0 and (nc // CHUNK) & (nc // CHUNK - 1) == 0
    assert d_conv % HEAD_DIM == 0 and d_ssm % HEAD_DIM == 0 and (3 * d_conv) % d_ssm == 0
    assert N_MOD * d == ada_w.shape[-1] and ne <= LANES

    tm = min(512, n)
    tt = min(256, n)
    tf = min(256, dff)

    rows = 16
    cvec = jnp.zeros((rows, d), F32).at[:bsz].set(c).at[bsz].set(c_ctx)
    mod4 = _ada(cvec, ada_w, ada_b).reshape(depth, rows, N_MOD, d)
    lat_row = lambda b: b
    ctx_row = lambda b: bsz

    w_in_bf = w_in.astype(BF16)
    w_out_bf = w_out.astype(BF16)
    w_glu_bf = ssm_w_glu.astype(BF16)
    rw_hi, rw_lo = _split_bf16(router_w)
    zpad = jnp.zeros((depth, d, LANES - ne), BF16)
    r1 = jnp.concatenate([rw_hi, zpad], axis=-1)
    r2 = jnp.concatenate([rw_hi, zpad, rw_lo, zpad], axis=-1)

    tile2 = lambda a: jnp.concatenate([a, a], axis=-1)
    lr_t = tile2(ssm_lam_re)[:, :, :, None, :]
    li_t = tile2(ssm_lam_im)[:, :, :, None, :]
    ldt = ssm_log_dt[:, :, :, None, None]
    brt = jnp.swapaxes(ssm_b_re, -1, -2)
    bit = jnp.swapaxes(ssm_b_im, -1, -2)
    bc1 = jnp.concatenate([brt, bit], axis=-1)
    bc2 = jnp.concatenate([-bit, brt], axis=-1)
    ca = jnp.concatenate([ssm_c_re, -ssm_c_im], axis=-1)
    cb = jnp.concatenate([-ssm_c_im, -ssm_c_re], axis=-1)
    dv = jnp.tile(ssm_d.reshape(depth, g, 1, SSM_GROUP), (1, 1, 1, CHUNK))

    g1n = norm1_g.reshape(depth, 1, d)
    g2n = norm2_g.reshape(depth, 1, d)
    gcn = out_norm_conv_g.reshape(depth, 1, d_conv)
    gsn = out_norm_ssm_g.reshape(depth, 1, d_ssm)
    gfin = final_norm_g.reshape(1, d)

    kk = jnp.asarray([float(k) for k in range(CHUNK + 1)]
                     + [float(CHUNK << j) for j in range(1, N_LEVELS)] + [0.0], F32).reshape(POW_ROWS, 1)

    xl = x
    xc = ctx
    for l in range(depth):
        last = l == depth - 1
        conv_l, u_lat = _inproj(xl, mod4, l, lat_row, g1n, w_in_bf, conv_w, gcn, rowlen=GRID_W, tm=tm,
                                pseudo=False)
        if last:
            u_ctx = _uproj(xc, mod4, l, ctx_row, g1n, w_in_bf, d_ssm, tm=nc)
        else:
            conv_c, u_ctx = _inproj(xc, mod4, l, ctx_row, g1n, w_in_bf, conv_w, gcn, rowlen=nc, tm=nc,
                                    pseudo=True)

        wcat, mtot, vmat, a1, a2 = _s5prep(l, kk, lr_t, li_t, ldt, bc1, bc2, ca, cb, dv)
        y_ctx, h0 = _s5mix(u_ctx, wcat, mtot, vmat, a1, a2, seg=nc // CHUNK)
        y_lat = _s5mix(u_lat, wcat, mtot, vmat, a1, a2, seg=n // CHUNK, h0=h0)

        n_tab = bsz * n + (0 if last else nbc)
        table = None if last else jnp.zeros((n_tab * (d // LANES), LANES), F32)
        xl, table, lg = _postmix(conv_l, y_lat, xl, mod4, l, lat_row, w_glu_bf, gsn, w_out_bf, g2n, r2, r1,
                                 tm=tm, pseudo=False, table_tokens=n_tab, table=table)
        lg_t = jnp.swapaxes(lg[:, :, :ne], 1, 2)
        pos, wv, cum = _route(lg_t, cap)
        capcol = jnp.full((bsz, ne, 1), cap, I32)
        starts = jnp.concatenate([cum[:, :, ::tt], capcol], axis=2).reshape(-1)
        pos_cols = jnp.swapaxes(pos, 1, 2)
        wv_cols = jnp.swapaxes(wv, 1, 2)
        offs_n = (jnp.arange(bsz, dtype=I32) * n)[:, None, None]
        slot_tok = jnp.swapaxes(_slotidx(pos_cols, cap) + offs_n, 0, 1).reshape(ne, bsz * cap)

        if not last:
            xc, table, lgc = _postmix(conv_c, y_ctx, xc, mod4, l, ctx_row, w_glu_bf, gsn, w_out_bf, g2n,
                                      r2, r1, tm=nc, pseudo=True, table_tokens=n_tab, table=table, tok0=bsz * n)
            lgc_t = jnp.swapaxes(lgc[:, :, :ne], 1, 2)
            posc, wvc, cumc = _route(lgc_t, cap_c)
            offs = (jnp.arange(bsz, dtype=I32) * cap_c)[:, None, None]
            posc = jnp.where(posc >= 0, posc + offs, -1)
            posc_cols = jnp.swapaxes(posc, 1, 2).reshape(1, nbc, ne)
            wvc_cols = jnp.swapaxes(wvc, 1, 2).reshape(1, nbc, ne)
            startsc = jnp.swapaxes(cumc[:, :, ::tt] + offs, 0, 1).reshape(ne, -1)
            startsc = jnp.concatenate([startsc, jnp.full((ne, 1), cap, I32)], axis=1).reshape(-1)
            slot_tok = jnp.concatenate([slot_tok, _slotidx(posc_cols, cap)[0] + bsz * n], axis=1)

        m_split = 2
        mh = slot_tok.shape[1] // m_split
        slot_tok = jnp.swapaxes(slot_tok.reshape(ne, m_split, mh), 0, 1).reshape(-1)
        y_exp = _ffn(slot_tok, table, l, exp_w_gate, exp_w_up, exp_w_down, tf=tf, m_split=m_split)
        xl = _combine(starts, y_exp, 0, pos_cols, wv_cols, xl, mod4, l, lat_row, cap,
                      gfin if last else None, tt=tt)
        if not last:
            xc = _combine(startsc, y_exp, bsz, posc_cols, wvc_cols, xc.reshape(1, nbc, d), mod4, l, ctx_row, cap,
                          tt=tt).reshape(bsz, nc, d)
    return xl
```

```python
import functools

import jax
import jax.numpy as jnp
from jax import lax
from jax.experimental import pallas as pl
from jax.experimental.pallas import tpu as pltpu

F32 = jnp.float32
BF16 = jnp.bfloat16
I32 = jnp.int32

EPS = 1e-6
GRID_W = 64
HEAD_DIM = 128
SSM_GROUP = 16
EC_FACTOR = 2
N_MOD = 6
CHUNK = 16
LANES = 128
CHUNK_W = CHUNK * SSM_GROUP

_MIB = 1 << 20


def _params(n_axes, vmem_mib):
    return pltpu.CompilerParams(dimension_semantics=("arbitrary",) * n_axes,
                                vmem_limit_bytes=vmem_mib * _MIB)


def _split_bf16(a):
    hi = a.astype(BF16)
    lo = (a - hi.astype(F32)).astype(BF16)
    return hi, lo


_NN = (((1,), (0,)), ((), ()))
_NT = (((1,), (1,)), ((), ()))


def _dot(a, b, dims=_NN):
    return lax.dot_general(a, b, dims, preferred_element_type=F32)


def _dot3(a, b, dims=_NN):
    ah, al = _split_bf16(a)
    bh, bl = _split_bf16(b)
    return _dot(ah, bh, dims) + _dot(ah, bl, dims) + _dot(al, bh, dims)


def _rms(x):
    return x * lax.rsqrt(jnp.mean(x * x, axis=-1, keepdims=True) + EPS)


def _ada_body(c_ref, w_ref, b_ref, o_ref):
    s = jax.nn.silu(c_ref[...])
    o_ref[...] = _dot3(s, w_ref[...]) + b_ref[...]


def _ada(cvec, ada_w, ada_b):
    depth, d, n6 = ada_w.shape
    tn = next(t for t in (1024, 512, 256, 128) if n6 % t == 0)
    rows = cvec.shape[0]
    return pl.pallas_call(
        _ada_body,
        out_shape=jax.ShapeDtypeStruct((depth, rows, n6), F32),
        grid=(depth, n6 // tn),
        in_specs=[
            pl.BlockSpec((rows, d), lambda l, j: (0, 0)),
            pl.BlockSpec((None, d, tn), lambda l, j: (l, 0, j)),
            pl.BlockSpec((None, 1, tn), lambda l, j: (l, 0, j)),
        ],
        out_specs=pl.BlockSpec((None, rows, tn), lambda l, j: (l, 0, j)),
        compiler_params=_params(2, 40),
        name="ada",
    )(cvec, ada_w, ada_b.reshape(depth, 1, n6))


def _store_lane_tiles(u_ref, u):
    for k in range(u_ref.shape[0]):
        u_ref[k] = u[:, k * LANES:(k + 1) * LANES]


def _u_tiles(nb, n, d_ssm, tm, pseudo):
    k4 = d_ssm // LANES
    if pseudo:
        return (jax.ShapeDtypeStruct((1, k4, nb * n, LANES), F32),
                pl.BlockSpec((None, k4, tm, LANES), lambda b, i: (0, 0, b * (n // tm) + i, 0)))
    return (jax.ShapeDtypeStruct((nb, k4, n, LANES), F32),
            pl.BlockSpec((None, k4, tm, LANES), lambda b, i: (b, 0, i, 0)))


def _inproj_body(x_ref, mod_ref, g_ref, w_ref, cw_ref, gc_ref, conv_ref, u_ref, *, rowlen, d_conv, cn):
    x = x_ref[...]
    tm = x.shape[0]
    h = _rms(x) * g_ref[...]
    h = h * (1.0 + mod_ref[1:2, :]) + mod_ref[0:1, :]
    hb = h.astype(BF16)
    t = jnp.bitwise_and(lax.broadcasted_iota(I32, (tm, 1), 0), rowlen - 1)
    first = t == 0
    last = t == rowlen - 1
    for j in range(d_conv // cn):
        c0 = j * cn
        bg = _dot(hb, w_ref[:, c0:c0 + cn])
        cg = _dot(hb, w_ref[:, d_conv + c0:d_conv + c0 + cn])
        v = _dot(hb, w_ref[:, 2 * d_conv + c0:2 * d_conv + c0 + cn])
        z = cg * v
        zp = jnp.where(first, 0.0, pltpu.roll(z, 1, 0))
        zn = jnp.where(last, 0.0, pltpu.roll(z, tm - 1, 0))
        cw = cw_ref[:, c0:c0 + cn]
        y = bg * (cw[0:1, :] * zp + cw[1:2, :] * z + cw[2:3, :] * zn)
        for hd in range(cn // HEAD_DIM):
            lo = hd * HEAD_DIM
            yh = _rms(y[:, lo:lo + HEAD_DIM]) * gc_ref[:, c0 + lo:c0 + lo + HEAD_DIM]
            conv_ref[:, c0 + lo:c0 + lo + HEAD_DIM] = yh.astype(conv_ref.dtype)
    _store_lane_tiles(u_ref, _dot(hb, w_ref[:, 3 * d_conv:]))


def _inproj(x, mod4, l, mod_row, norm_g, w_bf, conv_w, gc, *, rowlen, tm, pseudo):
    nb, n, d = x.shape
    depth, _, d_in = w_bf.shape
    d_conv = conv_w.shape[-1]
    d_ssm = d_in - 3 * d_conv
    cn = min(512, d_conv)
    body = functools.partial(_inproj_body, rowlen=rowlen, d_conv=d_conv, cn=cn)
    u_shape, u_spec = _u_tiles(nb, n, d_ssm, tm, pseudo)
    return pl.pallas_call(
        body,
        out_shape=(jax.ShapeDtypeStruct((nb, n, d_conv), BF16), u_shape),
        grid=(nb, n // tm),
        in_specs=[
            pl.BlockSpec((None, tm, d), lambda b, i: (b, i, 0)),
            pl.BlockSpec((None, None, N_MOD, d), lambda b, i: (l, mod_row(b), 0, 0)),
            pl.BlockSpec((None, 1, d), lambda b, i: (l, 0, 0)),
            pl.BlockSpec((None, d, d_in), lambda b, i: (l, 0, 0), pipeline_mode=pl.Buffered(1)),
            pl.BlockSpec((None, 3, d_conv), lambda b, i: (l, 0, 0)),
            pl.BlockSpec((None, 1, d_conv), lambda b, i: (l, 0, 0)),
        ],
        out_specs=(pl.BlockSpec((None, tm, d_conv), lambda b, i: (b, i, 0)), u_spec),
        compiler_params=_params(2, 56),
        name="inproj",
    )(x, mod4, norm_g, w_bf, conv_w, gc)


def _uproj_body(x_ref, mod_ref, g_ref, w_ref, u_ref):
    h = _rms(x_ref[...]) * g_ref[...]
    h = h * (1.0 + mod_ref[1:2, :]) + mod_ref[0:1, :]
    _store_lane_tiles(u_ref, _dot(h.astype(BF16), w_ref[...]))


def _uproj(x, mod4, l, mod_row, norm_g, w_bf, d_ssm, *, tm):
    nb, n, d = x.shape
    d_in = w_bf.shape[-1]
    col_blk = (d_in - d_ssm) // d_ssm
    u_shape, u_spec = _u_tiles(nb, n, d_ssm, tm, True)
    return pl.pallas_call(
        _uproj_body,
        out_shape=u_shape,
        grid=(nb, n // tm),
        in_specs=[
            pl.BlockSpec((None, tm, d), lambda b, i: (b, i, 0)),
            pl.BlockSpec((None, None, N_MOD, d), lambda b, i: (l, mod_row(b), 0, 0)),
            pl.BlockSpec((None, 1, d), lambda b, i: (l, 0, 0)),
            pl.BlockSpec((None, d, d_ssm), lambda b, i: (l, 0, col_blk)),
        ],
        out_specs=u_spec,
        compiler_params=_params(2, 32),
        name="uproj",
    )(x, mod4, norm_g, w_bf)


N_LEVELS = 7
POW_ROWS = 24


def _s5prep_body(kk_ref, lr_ref, li_ref, ldt_ref, bc1_ref, bc2_ref, ca_ref, cb_ref, dv_ref,
                 wcat_ref, mtot_ref, v_ref, a1_ref, a2_ref):
    t_ = CHUNK
    kk = kk_ref[...]
    lane_blk = jnp.right_shift(lax.broadcasted_iota(I32, (SSM_GROUP, CHUNK_W), 1), 4)
    lane = lax.broadcasted_iota(I32, (1, LANES), 1)
    sgn = jnp.where(lane < LANES // 2, -1.0, 1.0)
    eye = jnp.where(lax.broadcasted_iota(I32, (LANES, LANES), 0)
                    == lax.broadcasted_iota(I32, (LANES, LANES), 1), 1.0, 0.0)
    mtot = jnp.zeros((CHUNK_W, CHUNK_W), F32)
    for d in range(2):
        lr = lr_ref[d]
        li = li_ref[d]
        dt = jnp.exp(ldt_ref[d])
        pm = jnp.exp(kk * (dt * lr))
        ang = kk * (dt * li)
        pr = pm * jnp.cos(ang)
        pi = pm * jnp.sin(ang)
        ar = pr[1:2, :]
        ai = pi[1:2, :]
        den = lr * lr + li * li
        nr = ar - 1.0
        kr = (nr * lr + ai * li) / den
        ki = (ai * lr - nr * li) / den
        bc1 = bc1_ref[d]
        bc2 = bc2_ref[d]
        bb1 = kr * bc1 + ki * bc2
        bb2 = kr * bc2 - ki * bc1
        ca = ca_ref[d]
        cb = cb_ref[d]
        cak = [ca * pr[k:k + 1, :] + cb * pi[k:k + 1, :] for k in range(t_ + 1)]
        lag_order = range(t_) if d == 0 else range(t_ - 1, -1, -1)
        cak_all = jnp.concatenate([cak[k] for k in lag_order], axis=0)
        kall_t = _dot3(bb1, cak_all, _NT)
        rows = []
        for j in range(t_):
            if d == 0:
                shift, keep = (SSM_GROUP * j) % CHUNK_W, lane_blk >= j
            else:
                shift, keep = (SSM_GROUP * (j + 1)) % CHUNK_W, lane_blk <= j
            r = pltpu.roll(kall_t, shift, 1) if shift else kall_t
            rows.append(jnp.where(keep, r, 0.0))
        mtot = mtot + jnp.concatenate(rows, axis=0)
        e_v = [i + 1 for i in range(t_)] if d == 0 else [t_ - i for i in range(t_)]
        v_t = jnp.concatenate([cak[e] for e in e_v], axis=0)
        v_ref[d] = _dot3(eye, v_t, _NT).astype(v_ref.dtype)
        e_w = [t_ - 1 - j for j in range(t_)] if d == 0 else list(range(t_))
        w = jnp.concatenate([bb1 * pr[e:e + 1, :] + bb2 * pi[e:e + 1, :] for e in e_w], axis=0)
        wcat_ref[:, d * LANES:(d + 1) * LANES] = w.astype(wcat_ref.dtype)
        a1_ref[d] = pr[t_:t_ + 8, :]
        a2_ref[d] = sgn * pi[t_:t_ + 8, :]
    diag = (lax.broadcasted_iota(I32, (CHUNK_W, CHUNK_W), 0)
            == lax.broadcasted_iota(I32, (CHUNK_W, CHUNK_W), 1))
    mtot = mtot + jnp.where(diag, dv_ref[...], 0.0)
    mtot_ref[...] = mtot.astype(mtot_ref.dtype)


def _s5prep(l, kk, lr_t, li_t, ldt, bc1, bc2, ca, cb, dv):
    g = lr_t.shape[2]
    spec5 = lambda r, c: pl.BlockSpec((None, 2, None, r, c), lambda i: (l, 0, i, 0, 0))
    return pl.pallas_call(
        _s5prep_body,
        out_shape=(jax.ShapeDtypeStruct((g, CHUNK_W, 2 * LANES), BF16),
                   jax.ShapeDtypeStruct((g, CHUNK_W, CHUNK_W), BF16),
                   jax.ShapeDtypeStruct((2, g, LANES, CHUNK_W), BF16),
                   jax.ShapeDtypeStruct((2, g, 8, LANES), F32),
                   jax.ShapeDtypeStruct((2, g, 8, LANES), F32)),
        grid=(g,),
        in_specs=[pl.BlockSpec((POW_ROWS, 1), lambda i: (0, 0)),
                  spec5(1, LANES), spec5(1, LANES), spec5(1, 1),
                  spec5(SSM_GROUP, LANES), spec5(SSM_GROUP, LANES),
                  spec5(SSM_GROUP, LANES), spec5(SSM_GROUP, LANES),
                  pl.BlockSpec((None, None, 1, CHUNK_W), lambda i: (l, i, 0, 0))],
        out_specs=(pl.BlockSpec((None, CHUNK_W, 2 * LANES), lambda i: (i, 0, 0)),
                   pl.BlockSpec((None, CHUNK_W, CHUNK_W), lambda i: (i, 0, 0)),
                   pl.BlockSpec((2, None, LANES, CHUNK_W), lambda i: (0, i, 0, 0)),
                   pl.BlockSpec((2, None, 8, LANES), lambda i: (0, i, 0, 0)),
                   pl.BlockSpec((2, None, 8, LANES), lambda i: (0, i, 0, 0))),
        compiler_params=_params(1, 32),
        name="s5prep",
    )(kk, lr_t, li_t, ldt, bc1, bc2, ca, cb, dv)


def _s5mix_body(*refs, seg, with_h0):
    if with_h0:
        u_ref, wcat_ref, m_ref, v_ref, a1_ref, a2_ref, h0_ref, y_ref = refs
    else:
        u_ref, wcat_ref, m_ref, v_ref, a1_ref, a2_ref, y_ref, fin_ref, fs_ref = refs
    k4, n, _ = u_ref.shape
    c = n // CHUNK
    gpt = LANES // SSM_GROUP
    b = pl.program_id(0)
    rowm = jnp.bitwise_and(lax.broadcasted_iota(I32, (1, c, 1), 1), seg - 1)
    shifts = [1 << j for j in range(seg.bit_length() - 1)]

    def swap(x):
        return pltpu.roll(x, LANES // 2, 2)

    def scan(s, d, g0):
        a1 = a1_ref[d, g0:g0 + gpt]
        a2 = a2_ref[d, g0:g0 + gpt]
        if with_h0:
            h0 = h0_ref[d, g0:g0 + gpt, pl.ds(b, 1), :]
        else:
            h0 = 0.0
        if d == 0:
            x = jnp.where(rowm == 0, h0, pltpu.roll(s, 1, 1))
        else:
            x = jnp.where(rowm == seg - 1, h0, pltpu.roll(s, c - 1, 1))
        for j, sh in enumerate(shifts):
            if d == 0:
                xs = jnp.where(rowm >= sh, pltpu.roll(x, sh, 1), 0.0)
            else:
                xs = jnp.where(rowm < seg - sh, pltpu.roll(x, c - sh, 1), 0.0)
            x = x + a1[:, j:j + 1, :] * xs + a2[:, j:j + 1, :] * swap(xs)
        return x

    for k in range(k4):
        g0 = k * gpt
        xk = jnp.concatenate([u_ref[k, pl.ds(t, c, stride=CHUNK), :] for t in range(CHUNK)], axis=0)
        xt = xk.T
        zs, ss = [], []
        for gg in range(gpt):
            rg = jnp.concatenate([xt[gg * SSM_GROUP:(gg + 1) * SSM_GROUP, t * c:(t + 1) * c]
                                  for t in range(CHUNK)], axis=0)
            z = rg.T.astype(BF16)
            zs.append(z)
            ss.append(_dot(z, wcat_ref[g0 + gg]))
        hins = []
        for d in range(2):
            sd = jnp.stack([s[:, d * LANES:(d + 1) * LANES] for s in ss], axis=0)
            hin = scan(sd, d, g0)
            hins.append(hin)
            if not with_h0:
                a1 = a1_ref[d, g0:g0 + gpt]
                a2 = a2_ref[d, g0:g0 + gpt]
                fin = a1[:, 0:1, :] * hin + a2[:, 0:1, :] * swap(hin) + sd
                first = seg - 1 if d == 0 else 0
                for gg in range(gpt):
                    fs_ref[...] = fin[gg]
                    fin_ref[d, g0 + gg] = fs_ref[pl.ds(first, c // seg, stride=seg), :]
        yts = []
        for gg in range(gpt):
            y = _dot(zs[gg], m_ref[g0 + gg])
            for d in range(2):
                y = y + _dot(hins[d][gg].astype(BF16), v_ref[d, g0 + gg])
            yts.append(y.T)
        xto = jnp.concatenate(
            [jnp.concatenate([yt[t * SSM_GROUP:(t + 1) * SSM_GROUP, :] for yt in yts], axis=0)
             for t in range(CHUNK)], axis=1)
        xo = xto.T
        for t in range(CHUNK):
            y_ref[k, pl.ds(t, c, stride=CHUNK), :] = xo[t * c:(t + 1) * c, :]


def _s5mix(u4, wcat, mtot, v, a1, a2, *, seg, h0=None):
    nb, k4, n, _ = u4.shape
    g = wcat.shape[0]
    c = n // CHUNK
    with_h0 = h0 is not None
    body = functools.partial(_s5mix_body, seg=seg, with_h0=with_h0)
    tile = pl.BlockSpec((None, k4, n, LANES), lambda b: (b, 0, 0, 0))
    res = lambda shape: pl.BlockSpec(shape, lambda b: (0,) * len(shape), pipeline_mode=pl.Buffered(1))
    in_specs = [tile, res((g, CHUNK_W, 2 * LANES)), res((g, CHUNK_W, CHUNK_W)), res((2, g, LANES, CHUNK_W)),
                res((2, g, 8, LANES)), res((2, g, 8, LANES))]
    args = [u4, wcat, mtot, v, a1, a2]
    y_shape = jax.ShapeDtypeStruct((nb, k4, n, LANES), F32)
    if with_h0:
        in_specs.append(res((2, g, h0.shape[2], LANES)))
        args.append(h0)
        out_shape, out_specs, scratch = y_shape, tile, []
    else:
        out_shape = (y_shape, jax.ShapeDtypeStruct((2, g, c // seg, LANES), F32))
        out_specs = (tile, pl.BlockSpec((2, g, c // seg, LANES), lambda b: (0, 0, 0, 0)))
        scratch = [pltpu.VMEM((c, LANES), F32)]
    return pl.pallas_call(
        body,
        out_shape=out_shape,
        grid=(nb,),
        in_specs=in_specs,
        out_specs=out_specs,
        scratch_shapes=scratch,
        compiler_params=_params(1, 48),
        name="s5mix",
    )(*args)


def _postmix_body(*refs, d_conv, aliased):
    if aliased:
        refs = refs[1:]
    (cn_ref, y_ref, x_ref, mod_ref, wglu_ref, gs_ref, wout_ref, g2_ref, r2_ref, r1_ref,
     xo_ref, h2_ref, lg_ref) = refs
    yg = jax.nn.gelu(jnp.concatenate([y_ref[k] for k in range(y_ref.shape[0])], axis=1))
    z = _dot(yg.astype(BF16), wglu_ref[...])
    s = yg * jax.nn.sigmoid(z)
    d_ssm = s.shape[-1]
    mix = _dot(cn_ref[...], wout_ref[0:d_conv, :])
    for hd in range(d_ssm // HEAD_DIM):
        lo = hd * HEAD_DIM
        sn = _rms(s[:, lo:lo + HEAD_DIM]) * gs_ref[:, lo:lo + HEAD_DIM]
        mix = mix + _dot(sn.astype(BF16), wout_ref[d_conv + lo:d_conv + lo + HEAD_DIM, :])
    xn = x_ref[...] + mod_ref[2:3, :] * mix
    xo_ref[...] = xn
    h2 = _rms(xn) * g2_ref[...]
    h2 = h2 * (1.0 + mod_ref[4:5, :]) + mod_ref[3:4, :]
    hi, lo_ = _split_bf16(h2)
    tm, d = h2.shape
    for k in range(d // LANES):
        h2_ref[pl.ds(k, tm, stride=d // LANES), :] = h2[:, k * LANES:(k + 1) * LANES]
    d1 = _dot(hi, r2_ref[...])
    d2 = _dot(lo_, r1_ref[...])
    lg_ref[...] = d1[:, 0:LANES] + d1[:, LANES:2 * LANES] + d2


def _postmix(conv_n, y4, x, mod4, l, mod_row, wglu_bf, gs, wout_bf, norm2_g, r2, r1, *, tm, pseudo,
             table_tokens, table=None, tok0=0):
    nb, n, d = x.shape
    d_conv = conv_n.shape[-1]
    d_ssm = y4.shape[1] * LANES
    tok_rows = d // LANES
    aliased = table is not None
    body = functools.partial(_postmix_body, d_conv=d_conv, aliased=aliased)
    _, y_spec = _u_tiles(nb, n, d_ssm, tm, pseudo)
    tok = lambda w: pl.BlockSpec((None, tm, w), lambda b, i: (b, i, 0))
    lay = lambda r, c, **kw: pl.BlockSpec((None, r, c), lambda b, i: (l, 0, 0), **kw)
    in_specs = [tok(d_conv), y_spec, tok(d),
                pl.BlockSpec((None, None, N_MOD, d), lambda b, i: (l, mod_row(b), 0, 0)),
                lay(d_ssm, d_ssm, pipeline_mode=pl.Buffered(1)),
                lay(1, d_ssm),
                lay(d_conv + d_ssm, d, pipeline_mode=pl.Buffered(1)),
                lay(1, d),
                lay(d, 2 * LANES, pipeline_mode=pl.Buffered(1)),
                lay(d, LANES, pipeline_mode=pl.Buffered(1))]
    args = [conv_n, y4, x, mod4, wglu_bf, gs, wout_bf, norm2_g, r2, r1]
    if aliased:
        in_specs.insert(0, pl.BlockSpec(memory_space=pl.ANY))
        args.insert(0, table)
    blk0 = tok0 // tm
    return pl.pallas_call(
        body,
        out_shape=(jax.ShapeDtypeStruct((nb, n, d), F32),
                   jax.ShapeDtypeStruct((table_tokens * tok_rows, LANES), F32),
                   jax.ShapeDtypeStruct((nb, n, LANES), F32)),
        grid=(nb, n // tm),
        in_specs=in_specs,
        out_specs=(tok(d),
                   pl.BlockSpec((tm * tok_rows, LANES), lambda b, i: (blk0 + b * (n // tm) + i, 0)),
                   tok(LANES)),
        input_output_aliases={0: 1} if aliased else {},
        compiler_params=_params(2, 56),
        name="postmix",
    )(*args)


BISECT_STEPS = 32

def _route_body(lg_ref, pos_ref, wv_ref, cum_ref, tri_ref, *, cap):
    nb, ne, n = lg_ref.shape
    rc = min(256, n)
    for r0 in range(0, n, rc):
        ri = lax.broadcasted_iota(I32, (rc, n), 0) + r0
        ci = lax.broadcasted_iota(I32, (rc, n), 1)
        tri_ref[r0:r0 + rc, :] = jnp.where(ri < ci, 1.0, 0.0).astype(BF16)
    lg = lg_ref[...]
    e = jnp.exp(lg - jnp.max(lg, axis=1, keepdims=True))
    aff = e / jnp.sum(e, axis=1, keepdims=True)
    capf = float(cap)

    def enough(t):
        return jnp.sum(jnp.where(aff >= t, 1.0, 0.0), axis=2, keepdims=True) >= capf

    hi = jnp.full((nb, ne, 1), 2.0, F32)
    for j in range(6, -1, -1):
        cand = hi * (2.0 ** -(1 << j))
        hi = jnp.where(enough(cand), hi, cand)
    half = hi * 0.5
    lo = jnp.where(enough(half), half, 0.0)
    for _ in range(BISECT_STEPS):
        mid = (lo + hi) * 0.5
        ok = enough(mid)
        lo = jnp.where(ok, mid, lo)
        hi = jnp.where(ok, hi, mid)
    gt = aff >= hi
    eq = jnp.logical_and(aff >= lo, aff < hi)
    need = capf - jnp.sum(jnp.where(gt, 1.0, 0.0), axis=2, keepdims=True)
    tri = tri_ref[...]
    eq_rank = _dot(jnp.where(eq, 1.0, 0.0).astype(BF16).reshape(nb * ne, n), tri).reshape(nb, ne, n)
    sel = jnp.logical_or(gt, jnp.logical_and(eq, eq_rank < need))
    pos = _dot(jnp.where(sel, 1.0, 0.0).astype(BF16).reshape(nb * ne, n), tri).reshape(nb, ne, n)
    pos_ref[...] = jnp.where(sel, pos.astype(I32), -1)
    wv_ref[...] = jnp.where(sel, aff, 0.0)
    cum_ref[...] = pos.astype(I32)


def _route(logits_t, cap):
    nb, ne, n = logits_t.shape
    body = functools.partial(_route_body, cap=cap)
    full = pl.BlockSpec((nb, ne, n), lambda i: (0, 0, 0))
    return pl.pallas_call(
        body,
        out_shape=(jax.ShapeDtypeStruct((nb, ne, n), I32),
                   jax.ShapeDtypeStruct((nb, ne, n), F32),
                   jax.ShapeDtypeStruct((nb, ne, n), I32)),
        grid=(1,),
        in_specs=[full],
        out_specs=(full, full, full),
        scratch_shapes=[pltpu.VMEM((n, n), BF16)],
        compiler_params=_params(1, 48),
        name="route",
    )(logits_t)


def _slotidx_body(pos_ref, o_ref, *, cap):
    n, ne = pos_ref.shape
    pos = pos_ref[...]
    tok = lax.broadcasted_iota(I32, (n, 1), 0).astype(F32)
    slot = lax.broadcasted_iota(I32, (n, cap), 1)
    for e in range(ne):
        hit = jnp.where(slot == pos[:, e:e + 1], tok, 0.0)
        o_ref[e:e + 1, :] = jnp.sum(hit, axis=0, keepdims=True).astype(I32)


def _slotidx(pos_cols, cap):
    nb, n, ne = pos_cols.shape
    return pl.pallas_call(
        functools.partial(_slotidx_body, cap=cap),
        out_shape=jax.ShapeDtypeStruct((nb, ne, cap), I32),
        grid=(nb,),
        in_specs=[pl.BlockSpec((None, n, ne), lambda b: (b, 0, 0))],
        out_specs=pl.BlockSpec((None, ne, cap), lambda b: (b, 0, 0)),
        compiler_params=_params(1, 32),
        name="slotidx",
    )(pos_cols)


def _ffn_body(idx_ref, tab_ref, wg_ref, wu_ref, wd_ref, y_ref, xbuf_ref, xs_ref, acc_ref, sem, *, tok_rows, nf):
    f = pl.program_id(2)
    blk = pl.program_id(0) * pl.num_programs(1) + pl.program_id(1)
    nblk = pl.num_programs(0) * pl.num_programs(1)
    step = blk * nf + f
    mh = xs_ref.shape[1]
    ch = mh // nf
    cur = jnp.bitwise_and(blk, 1)
    ring = jnp.bitwise_and(step, 1)

    def token_copy(blk_i, c, j, ring_i):
        tok = idx_ref[blk_i * mh + c * ch + j]
        src = tab_ref.at[pl.ds(pl.multiple_of(tok * tok_rows, tok_rows), tok_rows), :]
        dst = xbuf_ref.at[ring_i, pl.ds(pl.multiple_of(j * tok_rows, tok_rows), tok_rows), :]
        return pltpu.make_async_copy(src, dst, sem.at[ring_i])

    def wait_chunk(ring_i):
        pltpu.make_async_copy(tab_ref.at[pl.ds(0, ch * tok_rows), :], xbuf_ref.at[ring_i], sem.at[ring_i]).wait()

    def convert(ring_i, xs_slot, c):
        rows = pl.ds(pl.multiple_of(c * ch, ch), ch)
        for k in range(tok_rows):
            w = xbuf_ref[ring_i, pl.ds(k, ch, stride=tok_rows), :]
            xs_ref[xs_slot, rows, k * LANES:(k + 1) * LANES] = w.astype(BF16)

    def issue_loop(blk_i, c, ring_i):
        def issue(j, carry):
            token_copy(blk_i, c, j, ring_i).start()
            return carry
        lax.fori_loop(0, ch, issue, 0)

    @pl.when(step == 0)
    def _():
        for c in range(nf):
            issue_loop(0, c, 0)
            wait_chunk(0)
            convert(0, 0, c)
        issue_loop(1 % nblk, 0, 1)

    @pl.when(f == 0)
    def _():
        acc_ref[...] = jnp.zeros_like(acc_ref)

    q = step + nf + 1
    qb = q // nf
    qb = jnp.where(qb < nblk, qb, 0)
    for j in range(ch):
        token_copy(qb, q % nf, j, ring).start(priority=j % 2)

    xs = xs_ref[cur]
    g = _dot(xs, wg_ref[...].astype(BF16))
    u = _dot(xs, wu_ref[...].astype(BF16))
    act = (jax.nn.silu(g) * u).astype(BF16)
    acc_ref[...] += _dot(act, wd_ref[...].astype(BF16))

    wait_chunk(1 - ring)
    convert(1 - ring, 1 - cur, f)

    @pl.when(f == nf - 1)
    def _():
        y_ref[...] = acc_ref[...].astype(y_ref.dtype)

    @pl.when(step == nblk * nf - 1)
    def _():
        wait_chunk(ring)


def _ffn(slot_tok, table, l, w_gate, w_up, w_down, *, tf, m_split):
    _, ne, d, dff = w_gate.shape
    tok_rows = d // LANES
    m = slot_tok.shape[0] // ne
    mh = m // m_split
    nf = dff // tf
    assert mh % nf == 0 and table.shape[1] == LANES
    body = functools.partial(_ffn_body, tok_rows=tok_rows, nf=nf)
    grid_spec = pltpu.PrefetchScalarGridSpec(
        num_scalar_prefetch=1,
        grid=(m_split, ne, nf),
        in_specs=[pl.BlockSpec(memory_space=pl.ANY),
                  pl.BlockSpec((None, None, d, tf), lambda h, e, f, idx: (l, e, 0, f)),
                  pl.BlockSpec((None, None, d, tf), lambda h, e, f, idx: (l, e, 0, f)),
                  pl.BlockSpec((None, None, tf, d), lambda h, e, f, idx: (l, e, f, 0))],
        out_specs=pl.BlockSpec((None, mh, d), lambda h, e, f, idx: (e, h, 0)),
        scratch_shapes=[pltpu.VMEM((2, mh // nf * tok_rows, LANES), F32),
                        pltpu.VMEM((2, mh, d), BF16),
                        pltpu.VMEM((mh, d), F32),
                        pltpu.SemaphoreType.DMA((2,))])
    return pl.pallas_call(
        body,
        out_shape=jax.ShapeDtypeStruct((ne, m, d), BF16),
        grid_spec=grid_spec,
        compiler_params=_params(3, 56),
        name="ffn",
    )(slot_tok, table, w_gate, w_up, w_down)


COMBINE_WIN = 64


def _combine_body(*refs, cap, final):
    if final:
        st_ref, y_ref, pos_ref, wv_ref, x_ref, mod_ref, gf_ref, o_ref = refs
    else:
        st_ref, y_ref, pos_ref, wv_ref, x_ref, mod_ref, o_ref = refs
    ne = y_ref.shape[0]
    tt = x_ref.shape[0]
    win = COMBINE_WIN
    nt1 = pl.num_programs(1) + 1
    base = pl.program_id(0) * ne * nt1 + pl.program_id(1)
    pos = pos_ref[...]
    wv = wv_ref[...]

    def finish(moe):
        xn = x_ref[...] + mod_ref[5:6, :] * moe
        if final:
            xn = _rms(xn) * gf_ref[...]
        o_ref[...] = xn

    w0s = []
    fits = None
    for e in range(ne):
        s0 = st_ref[base + e * nt1]
        s1 = st_ref[base + e * nt1 + 1]
        w0 = jnp.minimum(jnp.left_shift(jnp.right_shift(s0, 4), 4), cap - win)
        ok = s1 - w0 <= win
        fits = ok if fits is None else jnp.logical_and(fits, ok)
        w0s.append(pl.multiple_of(w0, 16))

    @pl.when(fits)
    def _():
        lane = lax.broadcasted_iota(I32, (tt, LANES), 1)
        low = lane < win
        pieces, rows = [], []
        for e in range(0, ne, 2):
            tgt = jnp.where(low, lane + w0s[e], lane + (w0s[e + 1] - win))
            p = jnp.where(low, pos[:, e:e + 1], pos[:, e + 1:e + 2])
            w = jnp.where(low, wv[:, e:e + 1], wv[:, e + 1:e + 2])
            pieces.append(jnp.where(tgt == p, w, 0.0).astype(BF16))
            rows.append(y_ref[e, pl.ds(w0s[e], win), :])
            rows.append(y_ref[e + 1, pl.ds(w0s[e + 1], win), :])
        finish(_dot(jnp.concatenate(pieces, axis=1), jnp.concatenate(rows, axis=0)))

    @pl.when(jnp.logical_not(fits))
    def _():
        slot = lax.broadcasted_iota(I32, (tt, cap), 1)
        pieces = [jnp.where(slot == pos[:, e:e + 1], wv[:, e:e + 1], 0.0).astype(BF16) for e in range(ne)]
        finish(_dot(jnp.concatenate(pieces, axis=1), y_ref[...].reshape(ne * cap, y_ref.shape[-1])))


def _combine(starts, y, blk0, pos_cols, wv_cols, x, mod4, l, mod_row, cap, final_g=None, *, tt):
    nb, n, d = x.shape
    ne = y.shape[0]
    final = final_g is not None
    assert 2 * COMBINE_WIN == LANES and ne % 2 == 0 and cap % 16 == 0 and cap >= COMBINE_WIN
    body = functools.partial(_combine_body, cap=cap, final=final)
    in_specs = [pl.BlockSpec((ne, cap, d), lambda b, i, st: (0, blk0 + b, 0)),
                pl.BlockSpec((None, tt, ne), lambda b, i, st: (b, i, 0)),
                pl.BlockSpec((None, tt, ne), lambda b, i, st: (b, i, 0)),
                pl.BlockSpec((None, tt, d), lambda b, i, st: (b, i, 0)),
                pl.BlockSpec((None, None, N_MOD, d), lambda b, i, st: (l, mod_row(b), 0, 0))]
    args = [starts, y, pos_cols, wv_cols, x, mod4]
    if final:
        in_specs.append(pl.BlockSpec((1, d), lambda b, i, st: (0, 0)))
        args.append(final_g)
    grid_spec = pltpu.PrefetchScalarGridSpec(
        num_scalar_prefetch=1,
        grid=(nb, n // tt),
        in_specs=in_specs,
        out_specs=pl.BlockSpec((None, tt, d), lambda b, i, st: (b, i, 0)))
    return pl.pallas_call(
        body,
        out_shape=jax.ShapeDtypeStruct((nb, n, d), F32),
        grid_spec=grid_spec,
        compiler_params=_params(2, 56),
        name="combine",
    )(*args)


def kernel(x, c, ctx, c_ctx, ada_w, ada_b, norm1_g, w_in, conv_w, ssm_lam_re, ssm_lam_im, ssm_log_dt,
           ssm_b_re, ssm_b_im, ssm_c_re, ssm_c_im, ssm_d, ssm_w_glu, out_norm_conv_g, out_norm_ssm_g,
           w_out, norm2_g, router_w, exp_w_gate, exp_w_up, exp_w_down, final_norm_g):
    bsz, n, d = x.shape
    nc = ctx.shape[1]
    depth = ada_w.shape[0]
    d_conv = conv_w.shape[-1]
    d_ssm = ssm_d.shape[-1]
    g = d_ssm // SSM_GROUP
    p = ssm_lam_re.shape[-1]
    ne = router_w.shape[-1]
    dff = exp_w_gate.shape[-1]
    cap = EC_FACTOR * n // ne
    cap_c = EC_FACTOR * nc // ne
    nbc = bsz * nc
    assert bsz == 8 and 2 * p == LANES and ssm_b_re.shape[-1] == SSM_GROUP
    assert nbc == n and bsz * cap_c == cap
    assert n // CHUNK == 1 << N_LEVELS and nc % CHUNK == 0 and (nc // CHUNK) & (nc // CHUNK - 1) == 0
    assert d_conv % HEAD_DIM == 0 and d_ssm % HEAD_DIM == 0 and (3 * d_conv) % d_ssm == 0
    assert N_MOD * d == ada_w.shape[-1] and ne <= LANES

    tm = min(512, n)
    tt = min(256, n)
    tf = min(256, dff)

    rows = 16
    cvec = jnp.zeros((rows, d), F32).at[:bsz].set(c).at[bsz].set(c_ctx)
    mod4 = _ada(cvec, ada_w, ada_b).reshape(depth, rows, N_MOD, d)
    lat_row = lambda b: b
    ctx_row = lambda b: bsz

    w_in_bf = w_in.astype(BF16)
    w_out_bf = w_out.astype(BF16)
    w_glu_bf = ssm_w_glu.astype(BF16)
    rw_hi, rw_lo = _split_bf16(router_w)
    zpad = jnp.zeros((depth, d, LANES - ne), BF16)
    r1 = jnp.concatenate([rw_hi, zpad], axis=-1)
    r2 = jnp.concatenate([rw_hi, zpad, rw_lo, zpad], axis=-1)

    tile2 = lambda a: jnp.concatenate([a, a], axis=-1)
    lr_t = tile2(ssm_lam_re)[:, :, :, None, :]
    li_t = tile2(ssm_lam_im)[:, :, :, None, :]
    ldt = ssm_log_dt[:, :, :, None, None]
    brt = jnp.swapaxes(ssm_b_re, -1, -2)
    bit = jnp.swapaxes(ssm_b_im, -1, -2)
    bc1 = jnp.concatenate([brt, bit], axis=-1)
    bc2 = jnp.concatenate([-bit, brt], axis=-1)
    ca = jnp.concatenate([ssm_c_re, -ssm_c_im], axis=-1)
    cb = jnp.concatenate([-ssm_c_im, -ssm_c_re], axis=-1)
    dv = jnp.tile(ssm_d.reshape(depth, g, 1, SSM_GROUP), (1, 1, 1, CHUNK))

    g1n = norm1_g.reshape(depth, 1, d)
    g2n = norm2_g.reshape(depth, 1, d)
    gcn = out_norm_conv_g.reshape(depth, 1, d_conv)
    gsn = out_norm_ssm_g.reshape(depth, 1, d_ssm)
    gfin = final_norm_g.reshape(1, d)

    kk = jnp.asarray([float(k) for k in range(CHUNK + 1)]
                     + [float(CHUNK << j) for j in range(1, N_LEVELS)] + [0.0], F32).reshape(POW_ROWS, 1)

    xl = x
    xc = ctx
    for l in range(depth):
        last = l == depth - 1
        conv_l, u_lat = _inproj(xl, mod4, l, lat_row, g1n, w_in_bf, conv_w, gcn, rowlen=GRID_W, tm=tm,
                                pseudo=False)
        if last:
            u_ctx = _uproj(xc, mod4, l, ctx_row, g1n, w_in_bf, d_ssm, tm=nc)
        else:
            conv_c, u_ctx = _inproj(xc, mod4, l, ctx_row, g1n, w_in_bf, conv_w, gcn, rowlen=nc, tm=nc,
                                    pseudo=True)

        wcat, mtot, vmat, a1, a2 = _s5prep(l, kk, lr_t, li_t, ldt, bc1, bc2, ca, cb, dv)
        y_ctx, h0 = _s5mix(u_ctx, wcat, mtot, vmat, a1, a2, seg=nc // CHUNK)
        y_lat = _s5mix(u_lat, wcat, mtot, vmat, a1, a2, seg=n // CHUNK, h0=h0)

        n_tab = bsz * n + (0 if last else nbc)
        table = None if last else jnp.zeros((n_tab * (d // LANES), LANES), F32)
        xl, table, lg = _postmix(conv_l, y_lat, xl, mod4, l, lat_row, w_glu_bf, gsn, w_out_bf, g2n, r2, r1,
                                 tm=tm, pseudo=False, table_tokens=n_tab, table=table)
        lg_t = jnp.swapaxes(lg[:, :, :ne], 1, 2)
        pos, wv, cum = _route(lg_t, cap)
        capcol = jnp.full((bsz, ne, 1), cap, I32)
        starts = jnp.concatenate([cum[:, :, ::tt], capcol], axis=2).reshape(-1)
        pos_cols = jnp.swapaxes(pos, 1, 2)
        wv_cols = jnp.swapaxes(wv, 1, 2)
        offs_n = (jnp.arange(bsz, dtype=I32) * n)[:, None, None]
        slot_tok = jnp.swapaxes(_slotidx(pos_cols, cap) + offs_n, 0, 1).reshape(ne, bsz * cap)

        if not last:
            xc, table, lgc = _postmix(conv_c, y_ctx, xc, mod4, l, ctx_row, w_glu_bf, gsn, w_out_bf, g2n,
                                      r2, r1, tm=nc, pseudo=True, table_tokens=n_tab, table=table, tok0=bsz * n)
            lgc_t = jnp.swapaxes(lgc[:, :, :ne], 1, 2)
            posc, wvc, cumc = _route(lgc_t, cap_c)
            offs = (jnp.arange(bsz, dtype=I32) * cap_c)[:, None, None]
            posc = jnp.where(posc >= 0, posc + offs, -1)
            posc_cols = jnp.swapaxes(posc, 1, 2).reshape(1, nbc, ne)
            wvc_cols = jnp.swapaxes(wvc, 1, 2).reshape(1, nbc, ne)
            startsc = jnp.swapaxes(cumc[:, :, ::tt] + offs, 0, 1).reshape(ne, -1)
            startsc = jnp.concatenate([startsc, jnp.full((ne, 1), cap, I32)], axis=1).reshape(-1)
            slot_tok = jnp.concatenate([slot_tok, _slotidx(posc_cols, cap)[0] + bsz * n], axis=1)

        m_split = 2
        mh = slot_tok.shape[1] // m_split
        slot_tok = jnp.swapaxes(slot_tok.reshape(ne, m_split, mh), 0, 1).reshape(-1)
        y_exp = _ffn(slot_tok, table, l, exp_w_gate, exp_w_up, exp_w_down, tf=tf, m_split=m_split)
        xl = _combine(starts, y_exp, 0, pos_cols, wv_cols, xl, mod4, l, lat_row, cap,
                      gfin if last else None, tt=tt)
        if not last:
            xc = _combine(startsc, y_exp, bsz, posc_cols, wvc_cols, xc.reshape(1, nbc, d), mod4, l, ctx_row, cap,
                          tt=tt).reshape(bsz, nc, d)
    return xl
```

```python
import functools

import jax
import jax.numpy as jnp
from jax import lax
from jax.experimental import pallas as pl
from jax.experimental.pallas import tpu as pltpu

F32 = jnp.float32
BF16 = jnp.bfloat16
I32 = jnp.int32

EPS = 1e-6
GRID_W = 64
HEAD_DIM = 128
SSM_GROUP = 16
EC_FACTOR = 2
N_MOD = 6
CHUNK = 16
LANES = 128
CHUNK_W = CHUNK * SSM_GROUP

_MIB = 1 << 20


def _params(n_axes, vmem_mib):
    return pltpu.CompilerParams(dimension_semantics=("arbitrary",) * n_axes,
                                vmem_limit_bytes=vmem_mib * _MIB)


def _split_bf16(a):
    hi = a.astype(BF16)
    lo = (a - hi.astype(F32)).astype(BF16)
    return hi, lo


_NN = (((1,), (0,)), ((), ()))
_NT = (((1,), (1,)), ((), ()))


def _dot(a, b, dims=_NN):
    return lax.dot_general(a, b, dims, preferred_element_type=F32)


def _dot3(a, b, dims=_NN):
    ah, al = _split_bf16(a)
    bh, bl = _split_bf16(b)
    return _dot(ah, bh, dims) + _dot(ah, bl, dims) + _dot(al, bh, dims)


def _rms(x):
    return x * lax.rsqrt(jnp.mean(x * x, axis=-1, keepdims=True) + EPS)


def _ada_body(c_ref, w_ref, b_ref, o_ref):
    s = jax.nn.silu(c_ref[...])
    o_ref[...] = _dot3(s, w_ref[...]) + b_ref[...]


def _ada(cvec, ada_w, ada_b):
    depth, d, n6 = ada_w.shape
    tn = next(t for t in (1024, 512, 256, 128) if n6 % t == 0)
    rows = cvec.shape[0]
    return pl.pallas_call(
        _ada_body,
        out_shape=jax.ShapeDtypeStruct((depth, rows, n6), F32),
        grid=(depth, n6 // tn),
        in_specs=[
            pl.BlockSpec((rows, d), lambda l, j: (0, 0)),
            pl.BlockSpec((None, d, tn), lambda l, j: (l, 0, j)),
            pl.BlockSpec((None, 1, tn), lambda l, j: (l, 0, j)),
        ],
        out_specs=pl.BlockSpec((None, rows, tn), lambda l, j: (l, 0, j)),
        compiler_params=_params(2, 40),
        name="ada",
    )(cvec, ada_w, ada_b.reshape(depth, 1, n6))


def _store_lane_tiles(u_ref, u):
    for k in range(u_ref.shape[0]):
        u_ref[k] = u[:, k * LANES:(k + 1) * LANES]


def _u_tiles(nb, n, d_ssm, tm, pseudo):
    k4 = d_ssm // LANES
    if pseudo:
        return (jax.ShapeDtypeStruct((1, k4, nb * n, LANES), F32),
                pl.BlockSpec((None, k4, tm, LANES), lambda b, i: (0, 0, b * (n // tm) + i, 0)))
    return (jax.ShapeDtypeStruct((nb, k4, n, LANES), F32),
            pl.BlockSpec((None, k4, tm, LANES), lambda b, i: (b, 0, i, 0)))


def _inproj_body(x_ref, mod_ref, g_ref, w_ref, cw_ref, gc_ref, conv_ref, u_ref, *, rowlen, d_conv, cn):
    x = x_ref[...]
    tm = x.shape[0]
    h = _rms(x) * g_ref[...]
    h = h * (1.0 + mod_ref[1:2, :]) + mod_ref[0:1, :]
    hb = h.astype(BF16)
    t = jnp.bitwise_and(lax.broadcasted_iota(I32, (tm, 1), 0), rowlen - 1)
    first = t == 0
    last = t == rowlen - 1
    for j in range(d_conv // cn):
        c0 = j * cn
        bg = _dot(hb, w_ref[:, c0:c0 + cn])
        cg = _dot(hb, w_ref[:, d_conv + c0:d_conv + c0 + cn])
        v = _dot(hb, w_ref[:, 2 * d_conv + c0:2 * d_conv + c0 + cn])
        z = cg * v
        zp = jnp.where(first, 0.0, pltpu.roll(z, 1, 0))
        zn = jnp.where(last, 0.0, pltpu.roll(z, tm - 1, 0))
        cw = cw_ref[:, c0:c0 + cn]
        y = bg * (cw[0:1, :] * zp + cw[1:2, :] * z + cw[2:3, :] * zn)
        for hd in range(cn // HEAD_DIM):
            lo = hd * HEAD_DIM
            yh = _rms(y[:, lo:lo + HEAD_DIM]) * gc_ref[:, c0 + lo:c0 + lo + HEAD_DIM]
            conv_ref[:, c0 + lo:c0 + lo + HEAD_DIM] = yh.astype(conv_ref.dtype)
    _store_lane_tiles(u_ref, _dot(hb, w_ref[:, 3 * d_conv:]))


def _inproj(x, mod4, l, mod_row, norm_g, w_bf, conv_w, gc, *, rowlen, tm, pseudo):
    nb, n, d = x.shape
    depth, _, d_in = w_bf.shape
    d_conv = conv_w.shape[-1]
    d_ssm = d_in - 3 * d_conv
    cn = min(512, d_conv)
    body = functools.partial(_inproj_body, rowlen=rowlen, d_conv=d_conv, cn=cn)
    u_shape, u_spec = _u_tiles(nb, n, d_ssm, tm, pseudo)
    return pl.pallas_call(
        body,
        out_shape=(jax.ShapeDtypeStruct((nb, n, d_conv), BF16), u_shape),
        grid=(nb, n // tm),
        in_specs=[
            pl.BlockSpec((None, tm, d), lambda b, i: (b, i, 0)),
            pl.BlockSpec((None, None, N_MOD, d), lambda b, i: (l, mod_row(b), 0, 0)),
            pl.BlockSpec((None, 1, d), lambda b, i: (l, 0, 0)),
            pl.BlockSpec((None, d, d_in), lambda b, i: (l, 0, 0), pipeline_mode=pl.Buffered(1)),
            pl.BlockSpec((None, 3, d_conv), lambda b, i: (l, 0, 0)),
            pl.BlockSpec((None, 1, d_conv), lambda b, i: (l, 0, 0)),
        ],
        out_specs=(pl.BlockSpec((None, tm, d_conv), lambda b, i: (b, i, 0)), u_spec),
        compiler_params=_params(2, 56),
        name="inproj",
    )(x, mod4, norm_g, w_bf, conv_w, gc)


def _uproj_body(x_ref, mod_ref, g_ref, w_ref, u_ref):
    h = _rms(x_ref[...]) * g_ref[...]
    h = h * (1.0 + mod_ref[1:2, :]) + mod_ref[0:1, :]
    _store_lane_tiles(u_ref, _dot(h.astype(BF16), w_ref[...]))


def _uproj(x, mod4, l, mod_row, norm_g, w_bf, d_ssm, *, tm):
    nb, n, d = x.shape
    d_in = w_bf.shape[-1]
    col_blk = (d_in - d_ssm) // d_ssm
    u_shape, u_spec = _u_tiles(nb, n, d_ssm, tm, True)
    return pl.pallas_call(
        _uproj_body,
        out_shape=u_shape,
        grid=(nb, n // tm),
        in_specs=[
            pl.BlockSpec((None, tm, d), lambda b, i: (b, i, 0)),
            pl.BlockSpec((None, None, N_MOD, d), lambda b, i: (l, mod_row(b), 0, 0)),
            pl.BlockSpec((None, 1, d), lambda b, i: (l, 0, 0)),
            pl.BlockSpec((None, d, d_ssm), lambda b, i: (l, 0, col_blk)),
        ],
        out_specs=u_spec,
        compiler_params=_params(2, 32),
        name="uproj",
    )(x, mod4, norm_g, w_bf)


N_LEVELS = 7
POW_ROWS = 24


def _s5prep_body(kk_ref, lr_ref, li_ref, ldt_ref, bc1_ref, bc2_ref, ca_ref, cb_ref, dv_ref,
                 wcat_ref, mtot_ref, v_ref, a1_ref, a2_ref):
    t_ = CHUNK
    kk = kk_ref[...]
    lane_blk = jnp.right_shift(lax.broadcasted_iota(I32, (SSM_GROUP, CHUNK_W), 1), 4)
    lane = lax.broadcasted_iota(I32, (1, LANES), 1)
    sgn = jnp.where(lane < LANES // 2, -1.0, 1.0)
    eye = jnp.where(lax.broadcasted_iota(I32, (LANES, LANES), 0)
                    == lax.broadcasted_iota(I32, (LANES, LANES), 1), 1.0, 0.0)
    mtot = jnp.zeros((CHUNK_W, CHUNK_W), F32)
    for d in range(2):
        lr = lr_ref[d]
        li = li_ref[d]
        dt = jnp.exp(ldt_ref[d])
        pm = jnp.exp(kk * (dt * lr))
        ang = kk * (dt * li)
        pr = pm * jnp.cos(ang)
        pi = pm * jnp.sin(ang)
        ar = pr[1:2, :]
        ai = pi[1:2, :]
        den = lr * lr + li * li
        nr = ar - 1.0
        kr = (nr * lr + ai * li) / den
        ki = (ai * lr - nr * li) / den
        bc1 = bc1_ref[d]
        bc2 = bc2_ref[d]
        bb1 = kr * bc1 + ki * bc2
        bb2 = kr * bc2 - ki * bc1
        ca = ca_ref[d]
        cb = cb_ref[d]
        cak = [ca * pr[k:k + 1, :] + cb * pi[k:k + 1, :] for k in range(t_ + 1)]
        lag_order = range(t_) if d == 0 else range(t_ - 1, -1, -1)
        cak_all = jnp.concatenate([cak[k] for k in lag_order], axis=0)
        kall_t = _dot3(bb1, cak_all, _NT)
        rows = []
        for j in range(t_):
            if d == 0:
                shift, keep = (SSM_GROUP * j) % CHUNK_W, lane_blk >= j
            else:
                shift, keep = (SSM_GROUP * (j + 1)) % CHUNK_W, lane_blk <= j
            r = pltpu.roll(kall_t, shift, 1) if shift else kall_t
            rows.append(jnp.where(keep, r, 0.0))
        mtot = mtot + jnp.concatenate(rows, axis=0)
        e_v = [i + 1 for i in range(t_)] if d == 0 else [t_ - i for i in range(t_)]
        v_t = jnp.concatenate([cak[e] for e in e_v], axis=0)
        v_ref[d] = _dot3(eye, v_t, _NT).astype(v_ref.dtype)
        e_w = [t_ - 1 - j for j in range(t_)] if d == 0 else list(range(t_))
        w = jnp.concatenate([bb1 * pr[e:e + 1, :] + bb2 * pi[e:e + 1, :] for e in e_w], axis=0)
        wcat_ref[:, d * LANES:(d + 1) * LANES] = w.astype(wcat_ref.dtype)
        a1_ref[d] = pr[t_:t_ + 8, :]
        a2_ref[d] = sgn * pi[t_:t_ + 8, :]
    diag = (lax.broadcasted_iota(I32, (CHUNK_W, CHUNK_W), 0)
            == lax.broadcasted_iota(I32, (CHUNK_W, CHUNK_W), 1))
    mtot = mtot + jnp.where(diag, dv_ref[...], 0.0)
    mtot_ref[...] = mtot.astype(mtot_ref.dtype)


def _s5prep(l, kk, lr_t, li_t, ldt, bc1, bc2, ca, cb, dv):
    g = lr_t.shape[2]
    spec5 = lambda r, c: pl.BlockSpec((None, 2, None, r, c), lambda i: (l, 0, i, 0, 0))
    return pl.pallas_call(
        _s5prep_body,
        out_shape=(jax.ShapeDtypeStruct((g, CHUNK_W, 2 * LANES), BF16),
                   jax.ShapeDtypeStruct((g, CHUNK_W, CHUNK_W), BF16),
                   jax.ShapeDtypeStruct((2, g, LANES, CHUNK_W), BF16),
                   jax.ShapeDtypeStruct((2, g, 8, LANES), F32),
                   jax.ShapeDtypeStruct((2, g, 8, LANES), F32)),
        grid=(g,),
        in_specs=[pl.BlockSpec((POW_ROWS, 1), lambda i: (0, 0)),
                  spec5(1, LANES), spec5(1, LANES), spec5(1, 1),
                  spec5(SSM_GROUP, LANES), spec5(SSM_GROUP, LANES),
                  spec5(SSM_GROUP, LANES), spec5(SSM_GROUP, LANES),
                  pl.BlockSpec((None, None, 1, CHUNK_W), lambda i: (l, i, 0, 0))],
        out_specs=(pl.BlockSpec((None, CHUNK_W, 2 * LANES), lambda i: (i, 0, 0)),
                   pl.BlockSpec((None, CHUNK_W, CHUNK_W), lambda i: (i, 0, 0)),
                   pl.BlockSpec((2, None, LANES, CHUNK_W), lambda i: (0, i, 0, 0)),
                   pl.BlockSpec((2, None, 8, LANES), lambda i: (0, i, 0, 0)),
                   pl.BlockSpec((2, None, 8, LANES), lambda i: (0, i, 0, 0))),
        compiler_params=_params(1, 32),
        name="s5prep",
    )(kk, lr_t, li_t, ldt, bc1, bc2, ca, cb, dv)


def _s5mix_body(*refs, seg, with_h0):
    if with_h0:
        u_ref, wcat_ref, m_ref, v_ref, a1_ref, a2_ref, h0_ref, y_ref = refs
    else:
        u_ref, wcat_ref, m_ref, v_ref, a1_ref, a2_ref, y_ref, fin_ref, fs_ref = refs
    k4, n, _ = u_ref.shape
    c = n // CHUNK
    gpt = LANES // SSM_GROUP
    b = pl.program_id(0)
    rowm = jnp.bitwise_and(lax.broadcasted_iota(I32, (1, c, 1), 1), seg - 1)
    shifts = [1 << j for j in range(seg.bit_length() - 1)]

    def swap(x):
        return pltpu.roll(x, LANES // 2, 2)

    def scan(s, d, g0):
        a1 = a1_ref[d, g0:g0 + gpt]
        a2 = a2_ref[d, g0:g0 + gpt]
        if with_h0:
            h0 = h0_ref[d, g0:g0 + gpt, pl.ds(b, 1), :]
        else:
            h0 = 0.0
        if d == 0:
            x = jnp.where(rowm == 0, h0, pltpu.roll(s, 1, 1))
        else:
            x = jnp.where(rowm == seg - 1, h0, pltpu.roll(s, c - 1, 1))
        for j, sh in enumerate(shifts):
            if d == 0:
                xs = jnp.where(rowm >= sh, pltpu.roll(x, sh, 1), 0.0)
            else:
                xs = jnp.where(rowm < seg - sh, pltpu.roll(x, c - sh, 1), 0.0)
            x = x + a1[:, j:j + 1, :] * xs + a2[:, j:j + 1, :] * swap(xs)
        return x

    for k in range(k4):
        g0 = k * gpt
        xk = jnp.concatenate([u_ref[k, pl.ds(t, c, stride=CHUNK), :] for t in range(CHUNK)], axis=0)
        xt = xk.T
        zs, ss = [], []
        for gg in range(gpt):
            rg = jnp.concatenate([xt[gg * SSM_GROUP:(gg + 1) * SSM_GROUP, t * c:(t + 1) * c]
                                  for t in range(CHUNK)], axis=0)
            z = rg.T.astype(BF16)
            zs.append(z)
            ss.append(_dot(z, wcat_ref[g0 + gg]))
        hins = []
        for d in range(2):
            sd = jnp.stack([s[:, d * LANES:(d + 1) * LANES] for s in ss], axis=0)
            hin = scan(sd, d, g0)
            hins.append(hin)
            if not with_h0:
                a1 = a1_ref[d, g0:g0 + gpt]
                a2 = a2_ref[d, g0:g0 + gpt]
                fin = a1[:, 0:1, :] * hin + a2[:, 0:1, :] * swap(hin) + sd
                first = seg - 1 if d == 0 else 0
                for gg in range(gpt):
                    fs_ref[...] = fin[gg]
                    fin_ref[d, g0 + gg] = fs_ref[pl.ds(first, c // seg, stride=seg), :]
        yts = []
        for gg in range(gpt):
            lhs = jnp.concatenate([zs[gg], hins[0][gg].astype(BF16), hins[1][gg].astype(BF16)], axis=1)
            rhs = jnp.concatenate([m_ref[g0 + gg], v_ref[0, g0 + gg], v_ref[1, g0 + gg]], axis=0)
            yts.append(_dot(lhs, rhs).T)
        xto = jnp.concatenate(
            [jnp.concatenate([yt[t * SSM_GROUP:(t + 1) * SSM_GROUP, :] for yt in yts], axis=0)
             for t in range(CHUNK)], axis=1)
        xo = xto.T
        for t in range(CHUNK):
            y_ref[k, pl.ds(t, c, stride=CHUNK), :] = xo[t * c:(t + 1) * c, :]


def _s5mix(u4, wcat, mtot, v, a1, a2, *, seg, h0=None):
    nb, k4, n, _ = u4.shape
    g = wcat.shape[0]
    c = n // CHUNK
    with_h0 = h0 is not None
    body = functools.partial(_s5mix_body, seg=seg, with_h0=with_h0)
    tile = pl.BlockSpec((None, k4, n, LANES), lambda b: (b, 0, 0, 0))
    res = lambda shape: pl.BlockSpec(shape, lambda b: (0,) * len(shape), pipeline_mode=pl.Buffered(1))
    in_specs = [tile, res((g, CHUNK_W, 2 * LANES)), res((g, CHUNK_W, CHUNK_W)), res((2, g, LANES, CHUNK_W)),
                res((2, g, 8, LANES)), res((2, g, 8, LANES))]
    args = [u4, wcat, mtot, v, a1, a2]
    y_shape = jax.ShapeDtypeStruct((nb, k4, n, LANES), F32)
    if with_h0:
        in_specs.append(res((2, g, h0.shape[2], LANES)))
        args.append(h0)
        out_shape, out_specs, scratch = y_shape, tile, []
    else:
        out_shape = (y_shape, jax.ShapeDtypeStruct((2, g, c // seg, LANES), F32))
        out_specs = (tile, pl.BlockSpec((2, g, c // seg, LANES), lambda b: (0, 0, 0, 0)))
        scratch = [pltpu.VMEM((c, LANES), F32)]
    return pl.pallas_call(
        body,
        out_shape=out_shape,
        grid=(nb,),
        in_specs=in_specs,
        out_specs=out_specs,
        scratch_shapes=scratch,
        compiler_params=_params(1, 48),
        name="s5mix",
    )(*args)


def _postmix_body(*refs, d_conv, aliased):
    if aliased:
        refs = refs[1:]
    (cn_ref, y_ref, x_ref, mod_ref, wglu_ref, gs_ref, wout_ref, g2_ref, r2_ref, r1_ref,
     xo_ref, h2_ref, lg_ref) = refs
    yg = jax.nn.gelu(jnp.concatenate([y_ref[k] for k in range(y_ref.shape[0])], axis=1))
    z = _dot(yg.astype(BF16), wglu_ref[...])
    s = yg * jax.nn.sigmoid(z)
    d_ssm = s.shape[-1]
    sn = [(_rms(s[:, lo:lo + HEAD_DIM]) * gs_ref[:, lo:lo + HEAD_DIM]).astype(BF16)
          for lo in range(0, d_ssm, HEAD_DIM)]
    mix = _dot(cn_ref[...], wout_ref[0:d_conv, :]) + _dot(jnp.concatenate(sn, axis=1), wout_ref[d_conv:, :])
    xn = x_ref[...] + mod_ref[2:3, :] * mix
    xo_ref[...] = xn
    h2 = _rms(xn) * g2_ref[...]
    h2 = h2 * (1.0 + mod_ref[4:5, :]) + mod_ref[3:4, :]
    hi, lo_ = _split_bf16(h2)
    tm, d = h2.shape
    for k in range(d // LANES):
        h2_ref[pl.ds(k, tm, stride=d // LANES), :] = h2[:, k * LANES:(k + 1) * LANES]
    d1 = _dot(hi, r2_ref[...])
    d2 = _dot(lo_, r1_ref[...])
    lg_ref[...] = d1[:, 0:LANES] + d1[:, LANES:2 * LANES] + d2


def _postmix(conv_n, y4, x, mod4, l, mod_row, wglu_bf, gs, wout_bf, norm2_g, r2, r1, *, tm, pseudo,
             table_tokens, table=None, tok0=0):
    nb, n, d = x.shape
    d_conv = conv_n.shape[-1]
    d_ssm = y4.shape[1] * LANES
    tok_rows = d // LANES
    aliased = table is not None
    body = functools.partial(_postmix_body, d_conv=d_conv, aliased=aliased)
    _, y_spec = _u_tiles(nb, n, d_ssm, tm, pseudo)
    tok = lambda w: pl.BlockSpec((None, tm, w), lambda b, i: (b, i, 0))
    lay = lambda r, c, **kw: pl.BlockSpec((None, r, c), lambda b, i: (l, 0, 0), **kw)
    in_specs = [tok(d_conv), y_spec, tok(d),
                pl.BlockSpec((None, None, N_MOD, d), lambda b, i: (l, mod_row(b), 0, 0)),
                lay(d_ssm, d_ssm, pipeline_mode=pl.Buffered(1)),
                lay(1, d_ssm),
                lay(d_conv + d_ssm, d, pipeline_mode=pl.Buffered(1)),
                lay(1, d),
                lay(d, 2 * LANES, pipeline_mode=pl.Buffered(1)),
                lay(d, LANES, pipeline_mode=pl.Buffered(1))]
    args = [conv_n, y4, x, mod4, wglu_bf, gs, wout_bf, norm2_g, r2, r1]
    if aliased:
        in_specs.insert(0, pl.BlockSpec(memory_space=pl.ANY))
        args.insert(0, table)
    blk0 = tok0 // tm
    return pl.pallas_call(
        body,
        out_shape=(jax.ShapeDtypeStruct((nb, n, d), F32),
                   jax.ShapeDtypeStruct((table_tokens * tok_rows, LANES), F32),
                   jax.ShapeDtypeStruct((nb, n, LANES), F32)),
        grid=(nb, n // tm),
        in_specs=in_specs,
        out_specs=(tok(d),
                   pl.BlockSpec((tm * tok_rows, LANES), lambda b, i: (blk0 + b * (n // tm) + i, 0)),
                   tok(LANES)),
        input_output_aliases={0: 1} if aliased else {},
        compiler_params=_params(2, 56),
        name="postmix",
    )(*args)


BISECT_STEPS = 32

def _route_body(lg_ref, pos_ref, wv_ref, cum_ref, tri_ref, *, cap):
    nb, ne, n = lg_ref.shape
    rc = min(256, n)
    for r0 in range(0, n, rc):
        ri = lax.broadcasted_iota(I32, (rc, n), 0) + r0
        ci = lax.broadcasted_iota(I32, (rc, n), 1)
        tri_ref[r0:r0 + rc, :] = jnp.where(ri < ci, 1.0, 0.0).astype(BF16)
    lg = lg_ref[...]
    e = jnp.exp(lg - jnp.max(lg, axis=1, keepdims=True))
    aff = e / jnp.sum(e, axis=1, keepdims=True)
    capf = float(cap)

    def enough(t):
        return jnp.sum(jnp.where(aff >= t, 1.0, 0.0), axis=2, keepdims=True) >= capf

    hi = jnp.full((nb, ne, 1), 2.0, F32)
    for j in range(6, -1, -1):
        cand = hi * (2.0 ** -(1 << j))
        hi = jnp.where(enough(cand), hi, cand)
    half = hi * 0.5
    lo = jnp.where(enough(half), half, 0.0)
    for _ in range(BISECT_STEPS):
        mid = (lo + hi) * 0.5
        ok = enough(mid)
        lo = jnp.where(ok, mid, lo)
        hi = jnp.where(ok, hi, mid)
    gt = aff >= hi
    eq = jnp.logical_and(aff >= lo, aff < hi)
    need = capf - jnp.sum(jnp.where(gt, 1.0, 0.0), axis=2, keepdims=True)
    tri = tri_ref[...]
    eq_rank = _dot(jnp.where(eq, 1.0, 0.0).astype(BF16).reshape(nb * ne, n), tri).reshape(nb, ne, n)
    sel = jnp.logical_or(gt, jnp.logical_and(eq, eq_rank < need))
    pos = _dot(jnp.where(sel, 1.0, 0.0).astype(BF16).reshape(nb * ne, n), tri).reshape(nb, ne, n)
    pos_ref[...] = jnp.where(sel, pos.astype(I32), -1)
    wv_ref[...] = jnp.where(sel, aff, 0.0)
    cum_ref[...] = pos.astype(I32)


def _route(logits_t, cap):
    nb, ne, n = logits_t.shape
    body = functools.partial(_route_body, cap=cap)
    full = pl.BlockSpec((nb, ne, n), lambda i: (0, 0, 0))
    return pl.pallas_call(
        body,
        out_shape=(jax.ShapeDtypeStruct((nb, ne, n), I32),
                   jax.ShapeDtypeStruct((nb, ne, n), F32),
                   jax.ShapeDtypeStruct((nb, ne, n), I32)),
        grid=(1,),
        in_specs=[full],
        out_specs=(full, full, full),
        scratch_shapes=[pltpu.VMEM((n, n), BF16)],
        compiler_params=_params(1, 48),
        name="route",
    )(logits_t)


def _slotidx_body(pos_ref, o_ref, *, cap):
    n, ne = pos_ref.shape
    pos = pos_ref[...]
    tok = lax.broadcasted_iota(I32, (n, 1), 0).astype(F32)
    slot = lax.broadcasted_iota(I32, (n, cap), 1)
    for e in range(ne):
        hit = jnp.where(slot == pos[:, e:e + 1], tok, 0.0)
        o_ref[e:e + 1, :] = jnp.sum(hit, axis=0, keepdims=True).astype(I32)


def _slotidx(pos_cols, cap):
    nb, n, ne = pos_cols.shape
    return pl.pallas_call(
        functools.partial(_slotidx_body, cap=cap),
        out_shape=jax.ShapeDtypeStruct((nb, ne, cap), I32),
        grid=(nb,),
        in_specs=[pl.BlockSpec((None, n, ne), lambda b: (b, 0, 0))],
        out_specs=pl.BlockSpec((None, ne, cap), lambda b: (b, 0, 0)),
        compiler_params=_params(1, 32),
        name="slotidx",
    )(pos_cols)


def _ffn_body(idx_ref, tab_ref, wg_ref, wu_ref, wd_ref, y_ref, xbuf_ref, xs_ref, acc_ref, sem, *, tok_rows, nf):
    f = pl.program_id(2)
    blk = pl.program_id(0) * pl.num_programs(1) + pl.program_id(1)
    nblk = pl.num_programs(0) * pl.num_programs(1)
    step = blk * nf + f
    mh = xs_ref.shape[1]
    ch = mh // nf
    cur = jnp.bitwise_and(blk, 1)
    ring = jnp.bitwise_and(step, 1)

    def token_copy(blk_i, c, j, ring_i):
        tok = idx_ref[blk_i * mh + c * ch + j]
        src = tab_ref.at[pl.ds(pl.multiple_of(tok * tok_rows, tok_rows), tok_rows), :]
        dst = xbuf_ref.at[ring_i, pl.ds(pl.multiple_of(j * tok_rows, tok_rows), tok_rows), :]
        return pltpu.make_async_copy(src, dst, sem.at[ring_i])

    def wait_chunk(ring_i):
        pltpu.make_async_copy(tab_ref.at[pl.ds(0, ch * tok_rows), :], xbuf_ref.at[ring_i], sem.at[ring_i]).wait()

    def convert(ring_i, xs_slot, c):
        rows = pl.ds(pl.multiple_of(c * ch, ch), ch)
        for k in range(tok_rows):
            w = xbuf_ref[ring_i, pl.ds(k, ch, stride=tok_rows), :]
            xs_ref[xs_slot, rows, k * LANES:(k + 1) * LANES] = w.astype(BF16)

    def issue_loop(blk_i, c, ring_i):
        def issue(j, carry):
            token_copy(blk_i, c, j, ring_i).start()
            return carry
        lax.fori_loop(0, ch, issue, 0)

    @pl.when(step == 0)
    def _():
        for c in range(nf):
            issue_loop(0, c, 0)
            wait_chunk(0)
            convert(0, 0, c)
        issue_loop(1 % nblk, 0, 1)

    @pl.when(f == 0)
    def _():
        acc_ref[...] = jnp.zeros_like(acc_ref)

    q = step + nf + 1
    qb = q // nf
    qb = jnp.where(qb < nblk, qb, 0)
    for j in range(ch):
        token_copy(qb, q % nf, j, ring).start(priority=j % 2)

    xs = xs_ref[cur]
    g = _dot(xs, wg_ref[...].astype(BF16))
    u = _dot(xs, wu_ref[...].astype(BF16))
    act = (jax.nn.silu(g) * u).astype(BF16)
    acc_ref[...] += _dot(act, wd_ref[...].astype(BF16))

    wait_chunk(1 - ring)
    convert(1 - ring, 1 - cur, f)

    @pl.when(f == nf - 1)
    def _():
        y_ref[...] = acc_ref[...].astype(y_ref.dtype)

    @pl.when(step == nblk * nf - 1)
    def _():
        wait_chunk(ring)


def _ffn(slot_tok, table, l, w_gate, w_up, w_down, *, tf, m_split):
    _, ne, d, dff = w_gate.shape
    tok_rows = d // LANES
    m = slot_tok.shape[0] // ne
    mh = m // m_split
    nf = dff // tf
    assert mh % nf == 0 and table.shape[1] == LANES
    body = functools.partial(_ffn_body, tok_rows=tok_rows, nf=nf)
    grid_spec = pltpu.PrefetchScalarGridSpec(
        num_scalar_prefetch=1,
        grid=(m_split, ne, nf),
        in_specs=[pl.BlockSpec(memory_space=pl.ANY),
                  pl.BlockSpec((None, None, d, tf), lambda h, e, f, idx: (l, e, 0, f)),
                  pl.BlockSpec((None, None, d, tf), lambda h, e, f, idx: (l, e, 0, f)),
                  pl.BlockSpec((None, None, tf, d), lambda h, e, f, idx: (l, e, f, 0))],
        out_specs=pl.BlockSpec((None, mh, d), lambda h, e, f, idx: (e, h, 0)),
        scratch_shapes=[pltpu.VMEM((2, mh // nf * tok_rows, LANES), F32),
                        pltpu.VMEM((2, mh, d), BF16),
                        pltpu.VMEM((mh, d), F32),
                        pltpu.SemaphoreType.DMA((2,))])
    return pl.pallas_call(
        body,
        out_shape=jax.ShapeDtypeStruct((ne, m, d), BF16),
        grid_spec=grid_spec,
        compiler_params=_params(3, 56),
        name="ffn",
    )(slot_tok, table, w_gate, w_up, w_down)


COMBINE_WIN = 64


def _combine_body(*refs, cap, final):
    if final:
        st_ref, y_ref, pos_ref, wv_ref, x_ref, mod_ref, gf_ref, o_ref = refs
    else:
        st_ref, y_ref, pos_ref, wv_ref, x_ref, mod_ref, o_ref = refs
    ne = y_ref.shape[0]
    tt = x_ref.shape[0]
    win = COMBINE_WIN
    nt1 = pl.num_programs(1) + 1
    base = pl.program_id(0) * ne * nt1 + pl.program_id(1)
    pos = pos_ref[...]
    wv = wv_ref[...]

    def finish(moe):
        xn = x_ref[...] + mod_ref[5:6, :] * moe
        if final:
            xn = _rms(xn) * gf_ref[...]
        o_ref[...] = xn

    w0s = []
    fits = None
    for e in range(ne):
        s0 = st_ref[base + e * nt1]
        s1 = st_ref[base + e * nt1 + 1]
        w0 = jnp.minimum(jnp.left_shift(jnp.right_shift(s0, 4), 4), cap - win)
        ok = s1 - w0 <= win
        fits = ok if fits is None else jnp.logical_and(fits, ok)
        w0s.append(pl.multiple_of(w0, 16))

    @pl.when(fits)
    def _():
        lane = lax.broadcasted_iota(I32, (tt, LANES), 1)
        low = lane < win
        pieces, rows = [], []
        for e in range(0, ne, 2):
            tgt = jnp.where(low, lane + w0s[e], lane + (w0s[e + 1] - win))
            p = jnp.where(low, pos[:, e:e + 1], pos[:, e + 1:e + 2])
            w = jnp.where(low, wv[:, e:e + 1], wv[:, e + 1:e + 2])
            pieces.append(jnp.where(tgt == p, w, 0.0).astype(BF16))
            rows.append(y_ref[e, pl.ds(w0s[e], win), :])
            rows.append(y_ref[e + 1, pl.ds(w0s[e + 1], win), :])
        finish(_dot(jnp.concatenate(pieces, axis=1), jnp.concatenate(rows, axis=0)))

    @pl.when(jnp.logical_not(fits))
    def _():
        slot = lax.broadcasted_iota(I32, (tt, cap), 1)
        pieces = [jnp.where(slot == pos[:, e:e + 1], wv[:, e:e + 1], 0.0).astype(BF16) for e in range(ne)]
        finish(_dot(jnp.concatenate(pieces, axis=1), y_ref[...].reshape(ne * cap, y_ref.shape[-1])))


def _combine(starts, y, blk0, pos_cols, wv_cols, x, mod4, l, mod_row, cap, final_g=None, *, tt):
    nb, n, d = x.shape
    ne = y.shape[0]
    final = final_g is not None
    assert 2 * COMBINE_WIN == LANES and ne % 2 == 0 and cap % 16 == 0 and cap >= COMBINE_WIN
    body = functools.partial(_combine_body, cap=cap, final=final)
    in_specs = [pl.BlockSpec((ne, cap, d), lambda b, i, st: (0, blk0 + b, 0)),
                pl.BlockSpec((None, tt, ne), lambda b, i, st: (b, i, 0)),
                pl.BlockSpec((None, tt, ne), lambda b, i, st: (b, i, 0)),
                pl.BlockSpec((None, tt, d), lambda b, i, st: (b, i, 0)),
                pl.BlockSpec((None, None, N_MOD, d), lambda b, i, st: (l, mod_row(b), 0, 0))]
    args = [starts, y, pos_cols, wv_cols, x, mod4]
    if final:
        in_specs.append(pl.BlockSpec((1, d), lambda b, i, st: (0, 0)))
        args.append(final_g)
    grid_spec = pltpu.PrefetchScalarGridSpec(
        num_scalar_prefetch=1,
        grid=(nb, n // tt),
        in_specs=in_specs,
        out_specs=pl.BlockSpec((None, tt, d), lambda b, i, st: (b, i, 0)))
    return pl.pallas_call(
        body,
        out_shape=jax.ShapeDtypeStruct((nb, n, d), F32),
        grid_spec=grid_spec,
        compiler_params=_params(2, 56),
        name="combine",
    )(*args)


def kernel(x, c, ctx, c_ctx, ada_w, ada_b, norm1_g, w_in, conv_w, ssm_lam_re, ssm_lam_im, ssm_log_dt,
           ssm_b_re, ssm_b_im, ssm_c_re, ssm_c_im, ssm_d, ssm_w_glu, out_norm_conv_g, out_norm_ssm_g,
           w_out, norm2_g, router_w, exp_w_gate, exp_w_up, exp_w_down, final_norm_g):
    bsz, n, d = x.shape
    nc = ctx.shape[1]
    depth = ada_w.shape[0]
    d_conv = conv_w.shape[-1]
    d_ssm = ssm_d.shape[-1]
    g = d_ssm // SSM_GROUP
    p = ssm_lam_re.shape[-1]
    ne = router_w.shape[-1]
    dff = exp_w_gate.shape[-1]
    cap = EC_FACTOR * n // ne
    cap_c = EC_FACTOR * nc // ne
    nbc = bsz * nc
    assert bsz == 8 and 2 * p == LANES and ssm_b_re.shape[-1] == SSM_GROUP
    assert nbc == n and bsz * cap_c == cap
    assert n // CHUNK == 1 << N_LEVELS and nc % CHUNK == 0 and (nc // CHUNK) & (nc // CHUNK - 1) == 0
    assert d_conv % HEAD_DIM == 0 and d_ssm % HEAD_DIM == 0 and (3 * d_conv) % d_ssm == 0
    assert N_MOD * d == ada_w.shape[-1] and ne <= LANES

    tm = min(512, n)
    tt = min(256, n)
    tf = min(256, dff)

    rows = 16
    cvec = jnp.zeros((rows, d), F32).at[:bsz].set(c).at[bsz].set(c_ctx)
    mod4 = _ada(cvec, ada_w, ada_b).reshape(depth, rows, N_MOD, d)
    lat_row = lambda b: b
    ctx_row = lambda b: bsz

    w_in_bf = w_in.astype(BF16)
    w_out_bf = w_out.astype(BF16)
    w_glu_bf = ssm_w_glu.astype(BF16)
    rw_hi, rw_lo = _split_bf16(router_w)
    zpad = jnp.zeros((depth, d, LANES - ne), BF16)
    r1 = jnp.concatenate([rw_hi, zpad], axis=-1)
    r2 = jnp.concatenate([rw_hi, zpad, rw_lo, zpad], axis=-1)

    tile2 = lambda a: jnp.concatenate([a, a], axis=-1)
    lr_t = tile2(ssm_lam_re)[:, :, :, None, :]
    li_t = tile2(ssm_lam_im)[:, :, :, None, :]
    ldt = ssm_log_dt[:, :, :, None, None]
    brt = jnp.swapaxes(ssm_b_re, -1, -2)
    bit = jnp.swapaxes(ssm_b_im, -1, -2)
    bc1 = jnp.concatenate([brt, bit], axis=-1)
    bc2 = jnp.concatenate([-bit, brt], axis=-1)
    ca = jnp.concatenate([ssm_c_re, -ssm_c_im], axis=-1)
    cb = jnp.concatenate([-ssm_c_im, -ssm_c_re], axis=-1)
    dv = jnp.tile(ssm_d.reshape(depth, g, 1, SSM_GROUP), (1, 1, 1, CHUNK))

    g1n = norm1_g.reshape(depth, 1, d)
    g2n = norm2_g.reshape(depth, 1, d)
    gcn = out_norm_conv_g.reshape(depth, 1, d_conv)
    gsn = out_norm_ssm_g.reshape(depth, 1, d_ssm)
    gfin = final_norm_g.reshape(1, d)

    kk = jnp.asarray([float(k) for k in range(CHUNK + 1)]
                     + [float(CHUNK << j) for j in range(1, N_LEVELS)] + [0.0], F32).reshape(POW_ROWS, 1)

    xl = x
    xc = ctx
    for l in range(depth):
        last = l == depth - 1
        conv_l, u_lat = _inproj(xl, mod4, l, lat_row, g1n, w_in_bf, conv_w, gcn, rowlen=GRID_W, tm=tm,
                                pseudo=False)
        if last:
            u_ctx = _uproj(xc, mod4, l, ctx_row, g1n, w_in_bf, d_ssm, tm=nc)
        else:
            conv_c, u_ctx = _inproj(xc, mod4, l, ctx_row, g1n, w_in_bf, conv_w, gcn, rowlen=nc, tm=nc,
                                    pseudo=True)

        wcat, mtot, vmat, a1, a2 = _s5prep(l, kk, lr_t, li_t, ldt, bc1, bc2, ca, cb, dv)
        y_ctx, h0 = _s5mix(u_ctx, wcat, mtot, vmat, a1, a2, seg=nc // CHUNK)
        y_lat = _s5mix(u_lat, wcat, mtot, vmat, a1, a2, seg=n // CHUNK, h0=h0)

        n_tab = bsz * n + (0 if last else nbc)
        table = None if last else jnp.zeros((n_tab * (d // LANES), LANES), F32)
        xl, table, lg = _postmix(conv_l, y_lat, xl, mod4, l, lat_row, w_glu_bf, gsn, w_out_bf, g2n, r2, r1,
                                 tm=tm, pseudo=False, table_tokens=n_tab, table=table)
        lg_t = jnp.swapaxes(lg[:, :, :ne], 1, 2)
        pos, wv, cum = _route(lg_t, cap)
        capcol = jnp.full((bsz, ne, 1), cap, I32)
        starts = jnp.concatenate([cum[:, :, ::tt], capcol], axis=2).reshape(-1)
        pos_cols = jnp.swapaxes(pos, 1, 2)
        wv_cols = jnp.swapaxes(wv, 1, 2)
        offs_n = (jnp.arange(bsz, dtype=I32) * n)[:, None, None]
        slot_tok = jnp.swapaxes(_slotidx(pos_cols, cap) + offs_n, 0, 1).reshape(ne, bsz * cap)

        if not last:
            xc, table, lgc = _postmix(conv_c, y_ctx, xc, mod4, l, ctx_row, w_glu_bf, gsn, w_out_bf, g2n,
                                      r2, r1, tm=nc, pseudo=True, table_tokens=n_tab, table=table, tok0=bsz * n)
            lgc_t = jnp.swapaxes(lgc[:, :, :ne], 1, 2)
            posc, wvc, cumc = _route(lgc_t, cap_c)
            offs = (jnp.arange(bsz, dtype=I32) * cap_c)[:, None, None]
            posc = jnp.where(posc >= 0, posc + offs, -1)
            posc_cols = jnp.swapaxes(posc, 1, 2).reshape(1, nbc, ne)
            wvc_cols = jnp.swapaxes(wvc, 1, 2).reshape(1, nbc, ne)
            startsc = jnp.swapaxes(cumc[:, :, ::tt] + offs, 0, 1).reshape(ne, -1)
            startsc = jnp.concatenate([startsc, jnp.full((ne, 1), cap, I32)], axis=1).reshape(-1)
            slot_tok = jnp.concatenate([slot_tok, _slotidx(posc_cols, cap)[0] + bsz * n], axis=1)

        m_split = 2
        mh = slot_tok.shape[1] // m_split
        slot_tok = jnp.swapaxes(slot_tok.reshape(ne, m_split, mh), 0, 1).reshape(-1)
        y_exp = _ffn(slot_tok, table, l, exp_w_gate, exp_w_up, exp_w_down, tf=tf, m_split=m_split)
        xl = _combine(starts, y_exp, 0, pos_cols, wv_cols, xl, mod4, l, lat_row, cap,
                      gfin if last else None, tt=tt)
        if not last:
            xc = _combine(startsc, y_exp, bsz, posc_cols, wvc_cols, xc.reshape(1, nbc, d), mod4, l, ctx_row, cap,
                          tt=tt).reshape(bsz, nc, d)
    return xl
```

```python
import functools

import jax
import jax.numpy as jnp
from jax import lax
from jax.experimental import pallas as pl
from jax.experimental.pallas import tpu as pltpu

F32 = jnp.float32
BF16 = jnp.bfloat16
I32 = jnp.int32

EPS = 1e-6
GRID_W = 64
HEAD_DIM = 128
SSM_GROUP = 16
EC_FACTOR = 2
N_MOD = 6
CHUNK = 16
LANES = 128
CHUNK_W = CHUNK * SSM_GROUP

_MIB = 1 << 20


def _params(n_axes, vmem_mib):
    return pltpu.CompilerParams(dimension_semantics=("arbitrary",) * n_axes,
                                vmem_limit_bytes=vmem_mib * _MIB)


def _split_bf16(a):
    hi = a.astype(BF16)
    lo = (a - hi.astype(F32)).astype(BF16)
    return hi, lo


_NN = (((1,), (0,)), ((), ()))
_NT = (((1,), (1,)), ((), ()))


def _dot(a, b, dims=_NN):
    return lax.dot_general(a, b, dims, preferred_element_type=F32)


def _dot3(a, b, dims=_NN):
    ah, al = _split_bf16(a)
    bh, bl = _split_bf16(b)
    return _dot(ah, bh, dims) + _dot(ah, bl, dims) + _dot(al, bh, dims)


def _rms(x):
    return x * lax.rsqrt(jnp.mean(x * x, axis=-1, keepdims=True) + EPS)


def _ada_body(c_ref, w_ref, b_ref, o_ref):
    s = jax.nn.silu(c_ref[...])
    o_ref[...] = _dot3(s, w_ref[...]) + b_ref[...]


def _ada(cvec, ada_w, ada_b):
    depth, d, n6 = ada_w.shape
    tn = next(t for t in (1024, 512, 256, 128) if n6 % t == 0)
    rows = cvec.shape[0]
    return pl.pallas_call(
        _ada_body,
        out_shape=jax.ShapeDtypeStruct((depth, rows, n6), F32),
        grid=(depth, n6 // tn),
        in_specs=[
            pl.BlockSpec((rows, d), lambda l, j: (0, 0)),
            pl.BlockSpec((None, d, tn), lambda l, j: (l, 0, j)),
            pl.BlockSpec((None, 1, tn), lambda l, j: (l, 0, j)),
        ],
        out_specs=pl.BlockSpec((None, rows, tn), lambda l, j: (l, 0, j)),
        compiler_params=_params(2, 40),
        name="ada",
    )(cvec, ada_w, ada_b.reshape(depth, 1, n6))


def _store_lane_tiles(u_ref, u):
    for k in range(u_ref.shape[0]):
        u_ref[k] = u[:, k * LANES:(k + 1) * LANES]


def _u_tiles(nb, n, d_ssm, tm, pseudo):
    k4 = d_ssm // LANES
    if pseudo:
        return (jax.ShapeDtypeStruct((1, k4, nb * n, LANES), F32),
                pl.BlockSpec((None, k4, tm, LANES), lambda b, i: (0, 0, b * (n // tm) + i, 0)))
    return (jax.ShapeDtypeStruct((nb, k4, n, LANES), F32),
            pl.BlockSpec((None, k4, tm, LANES), lambda b, i: (b, 0, i, 0)))


def _inproj_body(x_ref, mod_ref, g_ref, w_ref, cw_ref, gc_ref, conv_ref, u_ref, *, rowlen, d_conv, cn):
    x = x_ref[...]
    tm = x.shape[0]
    h = _rms(x) * g_ref[...]
    h = h * (1.0 + mod_ref[1:2, :]) + mod_ref[0:1, :]
    hb = h.astype(BF16)
    t = jnp.bitwise_and(lax.broadcasted_iota(I32, (tm, 1), 0), rowlen - 1)
    first = t == 0
    last = t == rowlen - 1
    for j in range(d_conv // cn):
        c0 = j * cn
        bg = _dot(hb, w_ref[:, c0:c0 + cn])
        cg = _dot(hb, w_ref[:, d_conv + c0:d_conv + c0 + cn])
        v = _dot(hb, w_ref[:, 2 * d_conv + c0:2 * d_conv + c0 + cn])
        z = cg * v
        zp = jnp.where(first, 0.0, pltpu.roll(z, 1, 0))
        zn = jnp.where(last, 0.0, pltpu.roll(z, tm - 1, 0))
        cw = cw_ref[:, c0:c0 + cn]
        y = bg * (cw[0:1, :] * zp + cw[1:2, :] * z + cw[2:3, :] * zn)
        for hd in range(cn // HEAD_DIM):
            lo = hd * HEAD_DIM
            yh = _rms(y[:, lo:lo + HEAD_DIM]) * gc_ref[:, c0 + lo:c0 + lo + HEAD_DIM]
            conv_ref[:, c0 + lo:c0 + lo + HEAD_DIM] = yh.astype(conv_ref.dtype)
    _store_lane_tiles(u_ref, _dot(hb, w_ref[:, 3 * d_conv:]))


def _inproj(x, mod4, l, mod_row, norm_g, w_bf, conv_w, gc, *, rowlen, tm, pseudo):
    nb, n, d = x.shape
    depth, _, d_in = w_bf.shape
    d_conv = conv_w.shape[-1]
    d_ssm = d_in - 3 * d_conv
    cn = min(512, d_conv)
    body = functools.partial(_inproj_body, rowlen=rowlen, d_conv=d_conv, cn=cn)
    u_shape, u_spec = _u_tiles(nb, n, d_ssm, tm, pseudo)
    return pl.pallas_call(
        body,
        out_shape=(jax.ShapeDtypeStruct((nb, n, d_conv), BF16), u_shape),
        grid=(nb, n // tm),
        in_specs=[
            pl.BlockSpec((None, tm, d), lambda b, i: (b, i, 0)),
            pl.BlockSpec((None, None, N_MOD, d), lambda b, i: (l, mod_row(b), 0, 0)),
            pl.BlockSpec((None, 1, d), lambda b, i: (l, 0, 0)),
            pl.BlockSpec((None, d, d_in), lambda b, i: (l, 0, 0), pipeline_mode=pl.Buffered(1)),
            pl.BlockSpec((None, 3, d_conv), lambda b, i: (l, 0, 0)),
            pl.BlockSpec((None, 1, d_conv), lambda b, i: (l, 0, 0)),
        ],
        out_specs=(pl.BlockSpec((None, tm, d_conv), lambda b, i: (b, i, 0)), u_spec),
        compiler_params=_params(2, 56),
        name="inproj",
    )(x, mod4, norm_g, w_bf, conv_w, gc)


def _uproj_body(x_ref, mod_ref, g_ref, w_ref, u_ref):
    h = _rms(x_ref[...]) * g_ref[...]
    h = h * (1.0 + mod_ref[1:2, :]) + mod_ref[0:1, :]
    _store_lane_tiles(u_ref, _dot(h.astype(BF16), w_ref[...]))


def _uproj(x, mod4, l, mod_row, norm_g, w_bf, d_ssm, *, tm):
    nb, n, d = x.shape
    d_in = w_bf.shape[-1]
    col_blk = (d_in - d_ssm) // d_ssm
    u_shape, u_spec = _u_tiles(nb, n, d_ssm, tm, True)
    return pl.pallas_call(
        _uproj_body,
        out_shape=u_shape,
        grid=(nb, n // tm),
        in_specs=[
            pl.BlockSpec((None, tm, d), lambda b, i: (b, i, 0)),
            pl.BlockSpec((None, None, N_MOD, d), lambda b, i: (l, mod_row(b), 0, 0)),
            pl.BlockSpec((None, 1, d), lambda b, i: (l, 0, 0)),
            pl.BlockSpec((None, d, d_ssm), lambda b, i: (l, 0, col_blk)),
        ],
        out_specs=u_spec,
        compiler_params=_params(2, 32),
        name="uproj",
    )(x, mod4, norm_g, w_bf)


N_LEVELS = 7
POW_ROWS = 24


def _s5prep_body(kk_ref, lr_ref, li_ref, ldt_ref, bc1_ref, bc2_ref, ca_ref, cb_ref, dv_ref,
                 wcat_ref, mtot_ref, v_ref, a1_ref, a2_ref):
    t_ = CHUNK
    kk = kk_ref[...]
    lane_blk = jnp.right_shift(lax.broadcasted_iota(I32, (SSM_GROUP, CHUNK_W), 1), 4)
    lane = lax.broadcasted_iota(I32, (1, LANES), 1)
    sgn = jnp.where(lane < LANES // 2, -1.0, 1.0)
    eye = jnp.where(lax.broadcasted_iota(I32, (LANES, LANES), 0)
                    == lax.broadcasted_iota(I32, (LANES, LANES), 1), 1.0, 0.0)
    mtot = jnp.zeros((CHUNK_W, CHUNK_W), F32)
    for d in range(2):
        lr = lr_ref[d]
        li = li_ref[d]
        dt = jnp.exp(ldt_ref[d])
        pm = jnp.exp(kk * (dt * lr))
        ang = kk * (dt * li)
        pr = pm * jnp.cos(ang)
        pi = pm * jnp.sin(ang)
        ar = pr[1:2, :]
        ai = pi[1:2, :]
        den = lr * lr + li * li
        nr = ar - 1.0
        kr = (nr * lr + ai * li) / den
        ki = (ai * lr - nr * li) / den
        bc1 = bc1_ref[d]
        bc2 = bc2_ref[d]
        bb1 = kr * bc1 + ki * bc2
        bb2 = kr * bc2 - ki * bc1
        ca = ca_ref[d]
        cb = cb_ref[d]
        cak = [ca * pr[k:k + 1, :] + cb * pi[k:k + 1, :] for k in range(t_ + 1)]
        lag_order = range(t_) if d == 0 else range(t_ - 1, -1, -1)
        cak_all = jnp.concatenate([cak[k] for k in lag_order], axis=0)
        kall_t = _dot3(bb1, cak_all, _NT)
        rows = []
        for j in range(t_):
            if d == 0:
                shift, keep = (SSM_GROUP * j) % CHUNK_W, lane_blk >= j
            else:
                shift, keep = (SSM_GROUP * (j + 1)) % CHUNK_W, lane_blk <= j
            r = pltpu.roll(kall_t, shift, 1) if shift else kall_t
            rows.append(jnp.where(keep, r, 0.0))
        mtot = mtot + jnp.concatenate(rows, axis=0)
        e_v = [i + 1 for i in range(t_)] if d == 0 else [t_ - i for i in range(t_)]
        v_t = jnp.concatenate([cak[e] for e in e_v], axis=0)
        v_ref[d] = _dot3(eye, v_t, _NT).astype(v_ref.dtype)
        e_w = [t_ - 1 - j for j in range(t_)] if d == 0 else list(range(t_))
        w = jnp.concatenate([bb1 * pr[e:e + 1, :] + bb2 * pi[e:e + 1, :] for e in e_w], axis=0)
        wcat_ref[:, d * LANES:(d + 1) * LANES] = w.astype(wcat_ref.dtype)
        a1_ref[d] = pr[t_:t_ + 8, :]
        a2_ref[d] = sgn * pi[t_:t_ + 8, :]
    diag = (lax.broadcasted_iota(I32, (CHUNK_W, CHUNK_W), 0)
            == lax.broadcasted_iota(I32, (CHUNK_W, CHUNK_W), 1))
    mtot = mtot + jnp.where(diag, dv_ref[...], 0.0)
    mtot_ref[...] = mtot.astype(mtot_ref.dtype)


def _s5prep(l, kk, lr_t, li_t, ldt, bc1, bc2, ca, cb, dv):
    g = lr_t.shape[2]
    spec5 = lambda r, c: pl.BlockSpec((None, 2, None, r, c), lambda i: (l, 0, i, 0, 0))
    return pl.pallas_call(
        _s5prep_body,
        out_shape=(jax.ShapeDtypeStruct((g, CHUNK_W, 2 * LANES), BF16),
                   jax.ShapeDtypeStruct((g, CHUNK_W, CHUNK_W), BF16),
                   jax.ShapeDtypeStruct((2, g, LANES, CHUNK_W), BF16),
                   jax.ShapeDtypeStruct((2, g, 8, LANES), F32),
                   jax.ShapeDtypeStruct((2, g, 8, LANES), F32)),
        grid=(g,),
        in_specs=[pl.BlockSpec((POW_ROWS, 1), lambda i: (0, 0)),
                  spec5(1, LANES), spec5(1, LANES), spec5(1, 1),
                  spec5(SSM_GROUP, LANES), spec5(SSM_GROUP, LANES),
                  spec5(SSM_GROUP, LANES), spec5(SSM_GROUP, LANES),
                  pl.BlockSpec((None, None, 1, CHUNK_W), lambda i: (l, i, 0, 0))],
        out_specs=(pl.BlockSpec((None, CHUNK_W, 2 * LANES), lambda i: (i, 0, 0)),
                   pl.BlockSpec((None, CHUNK_W, CHUNK_W), lambda i: (i, 0, 0)),
                   pl.BlockSpec((2, None, LANES, CHUNK_W), lambda i: (0, i, 0, 0)),
                   pl.BlockSpec((2, None, 8, LANES), lambda i: (0, i, 0, 0)),
                   pl.BlockSpec((2, None, 8, LANES), lambda i: (0, i, 0, 0))),
        compiler_params=_params(1, 32),
        name="s5prep",
    )(kk, lr_t, li_t, ldt, bc1, bc2, ca, cb, dv)


def _s5mix_body(*refs, seg, with_h0):
    if with_h0:
        u_ref, wcat_ref, m_ref, v_ref, a1_ref, a2_ref, h0_ref, y_ref = refs
    else:
        u_ref, wcat_ref, m_ref, v_ref, a1_ref, a2_ref, y_ref, fin_ref, fs_ref = refs
    k4, n, _ = u_ref.shape
    c = n // CHUNK
    gpt = LANES // SSM_GROUP
    b = pl.program_id(0)
    rowm = jnp.bitwise_and(lax.broadcasted_iota(I32, (1, c, 1), 1), seg - 1)
    shifts = [1 << j for j in range(seg.bit_length() - 1)]

    def swap(x):
        return pltpu.roll(x, LANES // 2, 2)

    def scan(s, d, g0):
        a1 = a1_ref[d, g0:g0 + gpt]
        a2 = a2_ref[d, g0:g0 + gpt]
        if with_h0:
            h0 = h0_ref[d, g0:g0 + gpt, pl.ds(b, 1), :]
        else:
            h0 = 0.0
        if d == 0:
            x = jnp.where(rowm == 0, h0, pltpu.roll(s, 1, 1))
        else:
            x = jnp.where(rowm == seg - 1, h0, pltpu.roll(s, c - 1, 1))
        for j, sh in enumerate(shifts):
            if d == 0:
                xs = jnp.where(rowm >= sh, pltpu.roll(x, sh, 1), 0.0)
            else:
                xs = jnp.where(rowm < seg - sh, pltpu.roll(x, c - sh, 1), 0.0)
            x = x + a1[:, j:j + 1, :] * xs + a2[:, j:j + 1, :] * swap(xs)
        return x

    for k in range(k4):
        g0 = k * gpt
        xk = jnp.concatenate([u_ref[k, pl.ds(t, c, stride=CHUNK), :] for t in range(CHUNK)], axis=0)
        xt = xk.T
        zs, ss = [], []
        for gg in range(gpt):
            rg = jnp.concatenate([xt[gg * SSM_GROUP:(gg + 1) * SSM_GROUP, t * c:(t + 1) * c]
                                  for t in range(CHUNK)], axis=0)
            z = rg.T.astype(BF16)
            zs.append(z)
            ss.append(_dot(z, wcat_ref[g0 + gg]))
        hins = []
        for d in range(2):
            sd = jnp.stack([s[:, d * LANES:(d + 1) * LANES] for s in ss], axis=0)
            hin = scan(sd, d, g0)
            hins.append(hin)
            if not with_h0:
                a1 = a1_ref[d, g0:g0 + gpt]
                a2 = a2_ref[d, g0:g0 + gpt]
                fin = a1[:, 0:1, :] * hin + a2[:, 0:1, :] * swap(hin) + sd
                first = seg - 1 if d == 0 else 0
                for gg in range(gpt):
                    fs_ref[...] = fin[gg]
                    fin_ref[d, g0 + gg] = fs_ref[pl.ds(first, c // seg, stride=seg), :]
        yts = []
        for gg in range(gpt):
            lhs = jnp.concatenate([zs[gg], hins[0][gg].astype(BF16), hins[1][gg].astype(BF16)], axis=1)
            rhs = jnp.concatenate([m_ref[g0 + gg], v_ref[0, g0 + gg], v_ref[1, g0 + gg]], axis=0)
            yts.append(_dot(lhs, rhs).T)
        xto = jnp.concatenate(
            [jnp.concatenate([yt[t * SSM_GROUP:(t + 1) * SSM_GROUP, :] for yt in yts], axis=0)
             for t in range(CHUNK)], axis=1)
        xo = xto.T
        for t in range(CHUNK):
            y_ref[k, pl.ds(t, c, stride=CHUNK), :] = xo[t * c:(t + 1) * c, :]


def _s5mix(u4, wcat, mtot, v, a1, a2, *, seg, h0=None):
    nb, k4, n, _ = u4.shape
    g = wcat.shape[0]
    c = n // CHUNK
    with_h0 = h0 is not None
    body = functools.partial(_s5mix_body, seg=seg, with_h0=with_h0)
    tile = pl.BlockSpec((None, k4, n, LANES), lambda b: (b, 0, 0, 0))
    res = lambda shape: pl.BlockSpec(shape, lambda b: (0,) * len(shape), pipeline_mode=pl.Buffered(1))
    in_specs = [tile, res((g, CHUNK_W, 2 * LANES)), res((g, CHUNK_W, CHUNK_W)), res((2, g, LANES, CHUNK_W)),
                res((2, g, 8, LANES)), res((2, g, 8, LANES))]
    args = [u4, wcat, mtot, v, a1, a2]
    y_shape = jax.ShapeDtypeStruct((nb, k4, n, LANES), F32)
    if with_h0:
        in_specs.append(res((2, g, h0.shape[2], LANES)))
        args.append(h0)
        out_shape, out_specs, scratch = y_shape, tile, []
    else:
        out_shape = (y_shape, jax.ShapeDtypeStruct((2, g, c // seg, LANES), F32))
        out_specs = (tile, pl.BlockSpec((2, g, c // seg, LANES), lambda b: (0, 0, 0, 0)))
        scratch = [pltpu.VMEM((c, LANES), F32)]
    return pl.pallas_call(
        body,
        out_shape=out_shape,
        grid=(nb,),
        in_specs=in_specs,
        out_specs=out_specs,
        scratch_shapes=scratch,
        compiler_params=_params(1, 48),
        name="s5mix",
    )(*args)


def _postmix_body(*refs, d_conv, aliased):
    if aliased:
        refs = refs[1:]
    (cn_ref, y_ref, x_ref, mod_ref, wglu_ref, gs_ref, wout_ref, g2_ref, r2_ref, r1_ref,
     xo_ref, h2_ref, lg_ref) = refs
    yg = jax.nn.gelu(jnp.concatenate([y_ref[k] for k in range(y_ref.shape[0])], axis=1))
    z = _dot(yg.astype(BF16), wglu_ref[...])
    s = yg * jax.nn.sigmoid(z)
    d_ssm = s.shape[-1]
    sn = [(_rms(s[:, lo:lo + HEAD_DIM]) * gs_ref[:, lo:lo + HEAD_DIM]).astype(BF16)
          for lo in range(0, d_ssm, HEAD_DIM)]
    mix = _dot(cn_ref[...], wout_ref[0:d_conv, :]) + _dot(jnp.concatenate(sn, axis=1), wout_ref[d_conv:, :])
    xn = x_ref[...] + mod_ref[2:3, :] * mix
    xo_ref[...] = xn
    h2 = _rms(xn) * g2_ref[...]
    h2 = h2 * (1.0 + mod_ref[4:5, :]) + mod_ref[3:4, :]
    hi, lo_ = _split_bf16(h2)
    tm, d = h2.shape
    for k in range(d // LANES):
        h2_ref[pl.ds(k, tm, stride=d // LANES), :] = h2[:, k * LANES:(k + 1) * LANES]
    d1 = _dot(hi, r2_ref[...])
    d2 = _dot(lo_, r1_ref[...])
    lg_ref[...] = d1[:, 0:LANES] + d1[:, LANES:2 * LANES] + d2


def _postmix(conv_n, y4, x, mod4, l, mod_row, wglu_bf, gs, wout_bf, norm2_g, r2, r1, *, tm, pseudo,
             table_tokens, table=None, tok0=0):
    nb, n, d = x.shape
    d_conv = conv_n.shape[-1]
    d_ssm = y4.shape[1] * LANES
    tok_rows = d // LANES
    aliased = table is not None
    body = functools.partial(_postmix_body, d_conv=d_conv, aliased=aliased)
    _, y_spec = _u_tiles(nb, n, d_ssm, tm, pseudo)
    tok = lambda w: pl.BlockSpec((None, tm, w), lambda b, i: (b, i, 0))
    lay = lambda r, c, **kw: pl.BlockSpec((None, r, c), lambda b, i: (l, 0, 0), **kw)
    in_specs = [tok(d_conv), y_spec, tok(d),
                pl.BlockSpec((None, None, N_MOD, d), lambda b, i: (l, mod_row(b), 0, 0)),
                lay(d_ssm, d_ssm, pipeline_mode=pl.Buffered(1)),
                lay(1, d_ssm),
                lay(d_conv + d_ssm, d, pipeline_mode=pl.Buffered(1)),
                lay(1, d),
                lay(d, 2 * LANES, pipeline_mode=pl.Buffered(1)),
                lay(d, LANES, pipeline_mode=pl.Buffered(1))]
    args = [conv_n, y4, x, mod4, wglu_bf, gs, wout_bf, norm2_g, r2, r1]
    if aliased:
        in_specs.insert(0, pl.BlockSpec(memory_space=pl.ANY))
        args.insert(0, table)
    blk0 = tok0 // tm
    return pl.pallas_call(
        body,
        out_shape=(jax.ShapeDtypeStruct((nb, n, d), F32),
                   jax.ShapeDtypeStruct((table_tokens * tok_rows, LANES), F32),
                   jax.ShapeDtypeStruct((nb, n, LANES), F32)),
        grid=(nb, n // tm),
        in_specs=in_specs,
        out_specs=(tok(d),
                   pl.BlockSpec((tm * tok_rows, LANES), lambda b, i: (blk0 + b * (n // tm) + i, 0)),
                   tok(LANES)),
        input_output_aliases={0: 1} if aliased else {},
        compiler_params=_params(2, 56),
        name="postmix",
    )(*args)


BISECT_STEPS = 32

def _route_body(lg_ref, pos_ref, wv_ref, cum_ref, tri_ref, *, cap):
    nb, ne, n = lg_ref.shape
    rc = min(256, n)
    for r0 in range(0, n, rc):
        ri = lax.broadcasted_iota(I32, (rc, n), 0) + r0
        ci = lax.broadcasted_iota(I32, (rc, n), 1)
        tri_ref[r0:r0 + rc, :] = jnp.where(ri < ci, 1.0, 0.0).astype(BF16)
    lg = lg_ref[...]
    e = jnp.exp(lg - jnp.max(lg, axis=1, keepdims=True))
    aff = e / jnp.sum(e, axis=1, keepdims=True)
    capf = float(cap)

    def enough(t):
        return jnp.sum(jnp.where(aff >= t, 1.0, 0.0), axis=2, keepdims=True) >= capf

    hi = jnp.full((nb, ne, 1), 2.0, F32)
    for j in range(6, -1, -1):
        cand = hi * (2.0 ** -(1 << j))
        hi = jnp.where(enough(cand), hi, cand)
    half = hi * 0.5
    lo = jnp.where(enough(half), half, 0.0)
    for _ in range(BISECT_STEPS):
        mid = (lo + hi) * 0.5
        ok = enough(mid)
        lo = jnp.where(ok, mid, lo)
        hi = jnp.where(ok, hi, mid)
    gt = aff >= hi
    eq = jnp.logical_and(aff >= lo, aff < hi)
    need = capf - jnp.sum(jnp.where(gt, 1.0, 0.0), axis=2, keepdims=True)
    tri = tri_ref[...]
    eq_rank = _dot(jnp.where(eq, 1.0, 0.0).astype(BF16).reshape(nb * ne, n), tri).reshape(nb, ne, n)
    sel = jnp.logical_or(gt, jnp.logical_and(eq, eq_rank < need))
    pos = _dot(jnp.where(sel, 1.0, 0.0).astype(BF16).reshape(nb * ne, n), tri).reshape(nb, ne, n)
    pos_ref[...] = jnp.where(sel, pos.astype(I32), -1)
    wv_ref[...] = jnp.where(sel, aff, 0.0)
    cum_ref[...] = pos.astype(I32)


def _route(logits_t, cap):
    nb, ne, n = logits_t.shape
    body = functools.partial(_route_body, cap=cap)
    full = pl.BlockSpec((nb, ne, n), lambda i: (0, 0, 0))
    return pl.pallas_call(
        body,
        out_shape=(jax.ShapeDtypeStruct((nb, ne, n), I32),
                   jax.ShapeDtypeStruct((nb, ne, n), F32),
                   jax.ShapeDtypeStruct((nb, ne, n), I32)),
        grid=(1,),
        in_specs=[full],
        out_specs=(full, full, full),
        scratch_shapes=[pltpu.VMEM((n, n), BF16)],
        compiler_params=_params(1, 48),
        name="route",
    )(logits_t)


def _slotidx_body(pos_ref, o_ref, *, cap):
    n, ne = pos_ref.shape
    pos = pos_ref[...]
    tok = lax.broadcasted_iota(I32, (n, 1), 0).astype(F32)
    slot = lax.broadcasted_iota(I32, (n, cap), 1)
    for e in range(ne):
        hit = jnp.where(slot == pos[:, e:e + 1], tok, 0.0)
        o_ref[e:e + 1, :] = jnp.sum(hit, axis=0, keepdims=True).astype(I32)


def _slotidx(pos_cols, cap):
    nb, n, ne = pos_cols.shape
    return pl.pallas_call(
        functools.partial(_slotidx_body, cap=cap),
        out_shape=jax.ShapeDtypeStruct((nb, ne, cap), I32),
        grid=(nb,),
        in_specs=[pl.BlockSpec((None, n, ne), lambda b: (b, 0, 0))],
        out_specs=pl.BlockSpec((None, ne, cap), lambda b: (b, 0, 0)),
        compiler_params=_params(1, 32),
        name="slotidx",
    )(pos_cols)


def _ffn_body(idx_ref, tab_ref, wg_ref, wu_ref, wd_ref, y_ref, xbuf_ref, xs_ref, acc_ref, sem, *, tok_rows, nf):
    f = pl.program_id(2)
    blk = pl.program_id(0) * pl.num_programs(1) + pl.program_id(1)
    nblk = pl.num_programs(0) * pl.num_programs(1)
    step = blk * nf + f
    mh = xs_ref.shape[1]
    ch = mh // nf
    cur = jnp.bitwise_and(blk, 1)
    ring = jnp.bitwise_and(step, 1)

    def token_copy(blk_i, c, j, ring_i):
        tok = idx_ref[blk_i * mh + c * ch + j]
        src = tab_ref.at[pl.ds(pl.multiple_of(tok * tok_rows, tok_rows), tok_rows), :]
        dst = xbuf_ref.at[ring_i, pl.ds(pl.multiple_of(j * tok_rows, tok_rows), tok_rows), :]
        return pltpu.make_async_copy(src, dst, sem.at[ring_i])

    def wait_chunk(ring_i):
        pltpu.make_async_copy(tab_ref.at[pl.ds(0, ch * tok_rows), :], xbuf_ref.at[ring_i], sem.at[ring_i]).wait()

    def convert(ring_i, xs_slot, c):
        rows = pl.ds(pl.multiple_of(c * ch, ch), ch)
        for k in range(tok_rows):
            w = xbuf_ref[ring_i, pl.ds(k, ch, stride=tok_rows), :]
            xs_ref[xs_slot, rows, k * LANES:(k + 1) * LANES] = w.astype(BF16)

    def issue_loop(blk_i, c, ring_i):
        def issue(j, carry):
            token_copy(blk_i, c, j, ring_i).start()
            return carry
        lax.fori_loop(0, ch, issue, 0)

    @pl.when(step == 0)
    def _():
        for c in range(nf):
            issue_loop(0, c, 0)
            wait_chunk(0)
            convert(0, 0, c)
        issue_loop(1 % nblk, 0, 1)

    @pl.when(f == 0)
    def _():
        acc_ref[...] = jnp.zeros_like(acc_ref)

    q = step + nf + 1
    qb = q // nf
    qb = jnp.where(qb < nblk, qb, 0)
    for j in range(ch):
        token_copy(qb, q % nf, j, ring).start(priority=j % 2)

    xs = xs_ref[cur]
    g = _dot(xs, wg_ref[...].astype(BF16))
    u = _dot(xs, wu_ref[...].astype(BF16))
    act = (jax.nn.silu(g) * u).astype(BF16)
    acc_ref[...] += _dot(act, wd_ref[...].astype(BF16))

    wait_chunk(1 - ring)
    convert(1 - ring, 1 - cur, f)

    @pl.when(f == nf - 1)
    def _():
        y_ref[...] = acc_ref[...].astype(y_ref.dtype)

    @pl.when(step == nblk * nf - 1)
    def _():
        wait_chunk(ring)


def _ffn(slot_tok, table, l, w_gate, w_up, w_down, *, tf, m_split):
    _, ne, d, dff = w_gate.shape
    tok_rows = d // LANES
    m = slot_tok.shape[0] // ne
    mh = m // m_split
    nf = dff // tf
    assert mh % nf == 0 and table.shape[1] == LANES
    body = functools.partial(_ffn_body, tok_rows=tok_rows, nf=nf)
    grid_spec = pltpu.PrefetchScalarGridSpec(
        num_scalar_prefetch=1,
        grid=(m_split, ne, nf),
        in_specs=[pl.BlockSpec(memory_space=pl.ANY),
                  pl.BlockSpec((None, None, d, tf), lambda h, e, f, idx: (l, e, 0, f)),
                  pl.BlockSpec((None, None, d, tf), lambda h, e, f, idx: (l, e, 0, f)),
                  pl.BlockSpec((None, None, tf, d), lambda h, e, f, idx: (l, e, f, 0))],
        out_specs=pl.BlockSpec((None, mh, d), lambda h, e, f, idx: (e, h, 0)),
        scratch_shapes=[pltpu.VMEM((2, mh // nf * tok_rows, LANES), F32),
                        pltpu.VMEM((2, mh, d), BF16),
                        pltpu.VMEM((mh, d), F32),
                        pltpu.SemaphoreType.DMA((2,))])
    return pl.pallas_call(
        body,
        out_shape=jax.ShapeDtypeStruct((ne, m, d), BF16),
        grid_spec=grid_spec,
        compiler_params=_params(3, 56),
        name="ffn",
    )(slot_tok, table, w_gate, w_up, w_down)


COMBINE_WIN = 64


def _combine_body(*refs, cap, final):
    if final:
        st_ref, y_ref, pos_ref, wv_ref, x_ref, mod_ref, gf_ref, o_ref = refs
    else:
        st_ref, y_ref, pos_ref, wv_ref, x_ref, mod_ref, o_ref = refs
    ne = y_ref.shape[0]
    tt = x_ref.shape[0]
    win = COMBINE_WIN
    nt1 = pl.num_programs(1) + 1
    base = pl.program_id(0) * ne * nt1 + pl.program_id(1)
    pos = pos_ref[...]
    wv = wv_ref[...]

    def finish(moe):
        xn = x_ref[...] + mod_ref[5:6, :] * moe
        if final:
            xn = _rms(xn) * gf_ref[...]
        o_ref[...] = xn

    w0s = []
    fits = None
    for e in range(ne):
        s0 = st_ref[base + e * nt1]
        s1 = st_ref[base + e * nt1 + 1]
        w0 = jnp.minimum(jnp.left_shift(jnp.right_shift(s0, 4), 4), cap - win)
        ok = s1 - w0 <= win
        fits = ok if fits is None else jnp.logical_and(fits, ok)
        w0s.append(pl.multiple_of(w0, 16))

    @pl.when(fits)
    def _():
        lane = lax.broadcasted_iota(I32, (tt, LANES), 1)
        low = lane < win
        pieces, rows = [], []
        for e in range(0, ne, 2):
            tgt = jnp.where(low, lane + w0s[e], lane + (w0s[e + 1] - win))
            p = jnp.where(low, pos[:, e:e + 1], pos[:, e + 1:e + 2])
            w = jnp.where(low, wv[:, e:e + 1], wv[:, e + 1:e + 2])
            pieces.append(jnp.where(tgt == p, w, 0.0).astype(BF16))
            rows.append(y_ref[e, pl.ds(w0s[e], win), :])
            rows.append(y_ref[e + 1, pl.ds(w0s[e + 1], win), :])
        finish(_dot(jnp.concatenate(pieces, axis=1), jnp.concatenate(rows, axis=0)))

    @pl.when(jnp.logical_not(fits))
    def _():
        slot = lax.broadcasted_iota(I32, (tt, cap), 1)
        pieces = [jnp.where(slot == pos[:, e:e + 1], wv[:, e:e + 1], 0.0).astype(BF16) for e in range(ne)]
        finish(_dot(jnp.concatenate(pieces, axis=1), y_ref[...].reshape(ne * cap, y_ref.shape[-1])))


def _combine(starts, y, blk0, pos_cols, wv_cols, x, mod4, l, mod_row, cap, final_g=None, *, tt):
    nb, n, d = x.shape
    ne = y.shape[0]
    final = final_g is not None
    assert 2 * COMBINE_WIN == LANES and ne % 2 == 0 and cap % 16 == 0 and cap >= COMBINE_WIN
    body = functools.partial(_combine_body, cap=cap, final=final)
    in_specs = [pl.BlockSpec((ne, cap, d), lambda b, i, st: (0, blk0 + b, 0)),
                pl.BlockSpec((None, tt, ne), lambda b, i, st: (b, i, 0)),
                pl.BlockSpec((None, tt, ne), lambda b, i, st: (b, i, 0)),
                pl.BlockSpec((None, tt, d), lambda b, i, st: (b, i, 0)),
                pl.BlockSpec((None, None, N_MOD, d), lambda b, i, st: (l, mod_row(b), 0, 0))]
    args = [starts, y, pos_cols, wv_cols, x, mod4]
    if final:
        in_specs.append(pl.BlockSpec((1, d), lambda b, i, st: (0, 0)))
        args.append(final_g)
    grid_spec = pltpu.PrefetchScalarGridSpec(
        num_scalar_prefetch=1,
        grid=(nb, n // tt),
        in_specs=in_specs,
        out_specs=pl.BlockSpec((None, tt, d), lambda b, i, st: (b, i, 0)))
    return pl.pallas_call(
        body,
        out_shape=jax.ShapeDtypeStruct((nb, n, d), F32),
        grid_spec=grid_spec,
        compiler_params=_params(2, 56),
        name="combine",
    )(*args)


def kernel(x, c, ctx, c_ctx, ada_w, ada_b, norm1_g, w_in, conv_w, ssm_lam_re, ssm_lam_im, ssm_log_dt,
           ssm_b_re, ssm_b_im, ssm_c_re, ssm_c_im, ssm_d, ssm_w_glu, out_norm_conv_g, out_norm_ssm_g,
           w_out, norm2_g, router_w, exp_w_gate, exp_w_up, exp_w_down, final_norm_g):
    bsz, n, d = x.shape
    nc = ctx.shape[1]
    depth = ada_w.shape[0]
    d_conv = conv_w.shape[-1]
    d_ssm = ssm_d.shape[-1]
    g = d_ssm // SSM_GROUP
    p = ssm_lam_re.shape[-1]
    ne = router_w.shape[-1]
    dff = exp_w_gate.shape[-1]
    cap = EC_FACTOR * n // ne
    cap_c = EC_FACTOR * nc // ne
    nbc = bsz * nc
    assert bsz == 8 and 2 * p == LANES and ssm_b_re.shape[-1] == SSM_GROUP
    assert nbc == n and bsz * cap_c == cap
    assert n // CHUNK == 1 << N_LEVELS and nc % CHUNK == 0 and (nc // CHUNK) & (nc // CHUNK - 1) == 0
    assert d_conv % HEAD_DIM == 0 and d_ssm % HEAD_DIM == 0 and (3 * d_conv) % d_ssm == 0
    assert N_MOD * d == ada_w.shape[-1] and ne <= LANES

    tm = min(512, n)
    tm_in = min(1024, n)
    tt = min(256, n)
    tf = min(256, dff)

    rows = 16
    cvec = jnp.zeros((rows, d), F32).at[:bsz].set(c).at[bsz].set(c_ctx)
    mod4 = _ada(cvec, ada_w, ada_b).reshape(depth, rows, N_MOD, d)
    lat_row = lambda b: b
    ctx_row = lambda b: bsz

    w_in_bf = w_in.astype(BF16)
    w_out_bf = w_out.astype(BF16)
    w_glu_bf = ssm_w_glu.astype(BF16)
    rw_hi, rw_lo = _split_bf16(router_w)
    zpad = jnp.zeros((depth, d, LANES - ne), BF16)
    r1 = jnp.concatenate([rw_hi, zpad], axis=-1)
    r2 = jnp.concatenate([rw_hi, zpad, rw_lo, zpad], axis=-1)

    tile2 = lambda a: jnp.concatenate([a, a], axis=-1)
    lr_t = tile2(ssm_lam_re)[:, :, :, None, :]
    li_t = tile2(ssm_lam_im)[:, :, :, None, :]
    ldt = ssm_log_dt[:, :, :, None, None]
    brt = jnp.swapaxes(ssm_b_re, -1, -2)
    bit = jnp.swapaxes(ssm_b_im, -1, -2)
    bc1 = jnp.concatenate([brt, bit], axis=-1)
    bc2 = jnp.concatenate([-bit, brt], axis=-1)
    ca = jnp.concatenate([ssm_c_re, -ssm_c_im], axis=-1)
    cb = jnp.concatenate([-ssm_c_im, -ssm_c_re], axis=-1)
    dv = jnp.tile(ssm_d.reshape(depth, g, 1, SSM_GROUP), (1, 1, 1, CHUNK))

    g1n = norm1_g.reshape(depth, 1, d)
    g2n = norm2_g.reshape(depth, 1, d)
    gcn = out_norm_conv_g.reshape(depth, 1, d_conv)
    gsn = out_norm_ssm_g.reshape(depth, 1, d_ssm)
    gfin = final_norm_g.reshape(1, d)

    kk = jnp.asarray([float(k) for k in range(CHUNK + 1)]
                     + [float(CHUNK << j) for j in range(1, N_LEVELS)] + [0.0], F32).reshape(POW_ROWS, 1)

    xl = x
    xc = ctx
    for l in range(depth):
        last = l == depth - 1
        conv_l, u_lat = _inproj(xl, mod4, l, lat_row, g1n, w_in_bf, conv_w, gcn, rowlen=GRID_W, tm=tm_in,
                                pseudo=False)
        if last:
            u_ctx = _uproj(xc, mod4, l, ctx_row, g1n, w_in_bf, d_ssm, tm=nc)
        else:
            conv_c, u_ctx = _inproj(xc, mod4, l, ctx_row, g1n, w_in_bf, conv_w, gcn, rowlen=nc, tm=nc,
                                    pseudo=True)

        wcat, mtot, vmat, a1, a2 = _s5prep(l, kk, lr_t, li_t, ldt, bc1, bc2, ca, cb, dv)
        y_ctx, h0 = _s5mix(u_ctx, wcat, mtot, vmat, a1, a2, seg=nc // CHUNK)
        y_lat = _s5mix(u_lat, wcat, mtot, vmat, a1, a2, seg=n // CHUNK, h0=h0)

        n_tab = bsz * n + (0 if last else nbc)
        table = None if last else jnp.zeros((n_tab * (d // LANES), LANES), F32)
        xl, table, lg = _postmix(conv_l, y_lat, xl, mod4, l, lat_row, w_glu_bf, gsn, w_out_bf, g2n, r2, r1,
                                 tm=tm, pseudo=False, table_tokens=n_tab, table=table)
        lg_t = jnp.swapaxes(lg[:, :, :ne], 1, 2)
        pos, wv, cum = _route(lg_t, cap)
        capcol = jnp.full((bsz, ne, 1), cap, I32)
        starts = jnp.concatenate([cum[:, :, ::tt], capcol], axis=2).reshape(-1)
        pos_cols = jnp.swapaxes(pos, 1, 2)
        wv_cols = jnp.swapaxes(wv, 1, 2)
        offs_n = (jnp.arange(bsz, dtype=I32) * n)[:, None, None]
        slot_tok = jnp.swapaxes(_slotidx(pos_cols, cap) + offs_n, 0, 1).reshape(ne, bsz * cap)

        if not last:
            xc, table, lgc = _postmix(conv_c, y_ctx, xc, mod4, l, ctx_row, w_glu_bf, gsn, w_out_bf, g2n,
                                      r2, r1, tm=nc, pseudo=True, table_tokens=n_tab, table=table, tok0=bsz * n)
            lgc_t = jnp.swapaxes(lgc[:, :, :ne], 1, 2)
            posc, wvc, cumc = _route(lgc_t, cap_c)
            offs = (jnp.arange(bsz, dtype=I32) * cap_c)[:, None, None]
            posc = jnp.where(posc >= 0, posc + offs, -1)
            posc_cols = jnp.swapaxes(posc, 1, 2).reshape(1, nbc, ne)
            wvc_cols = jnp.swapaxes(wvc, 1, 2).reshape(1, nbc, ne)
            startsc = jnp.swapaxes(cumc[:, :, ::tt] + offs, 0, 1).reshape(ne, -1)
            startsc = jnp.concatenate([startsc, jnp.full((ne, 1), cap, I32)], axis=1).reshape(-1)
            slot_tok = jnp.concatenate([slot_tok, _slotidx(posc_cols, cap)[0] + bsz * n], axis=1)

        m_split = 2
        mh = slot_tok.shape[1] // m_split
        slot_tok = jnp.swapaxes(slot_tok.reshape(ne, m_split, mh), 0, 1).reshape(-1)
        y_exp = _ffn(slot_tok, table, l, exp_w_gate, exp_w_up, exp_w_down, tf=tf, m_split=m_split)
        xl = _combine(starts, y_exp, 0, pos_cols, wv_cols, xl, mod4, l, lat_row, cap,
                      gfin if last else None, tt=tt)
        if not last:
            xc = _combine(startsc, y_exp, bsz, posc_cols, wvc_cols, xc.reshape(1, nbc, d), mod4, l, ctx_row, cap,
                          tt=tt).reshape(bsz, nc, d)
    return xl
```

```python
import functools

import jax
import jax.numpy as jnp
from jax import lax
from jax.experimental import pallas as pl
from jax.experimental.pallas import tpu as pltpu

F32 = jnp.float32
BF16 = jnp.bfloat16
I32 = jnp.int32

EPS = 1e-6
GRID_W = 64
HEAD_DIM = 128
SSM_GROUP = 16
EC_FACTOR = 2
N_MOD = 6
CHUNK = 16
LANES = 128
CHUNK_W = CHUNK * SSM_GROUP

_MIB = 1 << 20


def _params(n_axes, vmem_mib):
    return pltpu.CompilerParams(dimension_semantics=("arbitrary",) * n_axes,
                                vmem_limit_bytes=vmem_mib * _MIB)


def _split_bf16(a):
    hi = a.astype(BF16)
    lo = (a - hi.astype(F32)).astype(BF16)
    return hi, lo


_NN = (((1,), (0,)), ((), ()))
_NT = (((1,), (1,)), ((), ()))


def _dot(a, b, dims=_NN):
    return lax.dot_general(a, b, dims, preferred_element_type=F32)


def _dot3(a, b, dims=_NN):
    ah, al = _split_bf16(a)
    bh, bl = _split_bf16(b)
    return _dot(ah, bh, dims) + _dot(ah, bl, dims) + _dot(al, bh, dims)


def _rms(x):
    return x * lax.rsqrt(jnp.mean(x * x, axis=-1, keepdims=True) + EPS)


def _ada_body(c_ref, w_ref, b_ref, o_ref):
    s = jax.nn.silu(c_ref[...])
    o_ref[...] = _dot3(s, w_ref[...]) + b_ref[...]


def _ada(cvec, ada_w, ada_b):
    depth, d, n6 = ada_w.shape
    tn = next(t for t in (1024, 512, 256, 128) if n6 % t == 0)
    rows = cvec.shape[0]
    return pl.pallas_call(
        _ada_body,
        out_shape=jax.ShapeDtypeStruct((depth, rows, n6), F32),
        grid=(depth, n6 // tn),
        in_specs=[
            pl.BlockSpec((rows, d), lambda l, j: (0, 0)),
            pl.BlockSpec((None, d, tn), lambda l, j: (l, 0, j)),
            pl.BlockSpec((None, 1, tn), lambda l, j: (l, 0, j)),
        ],
        out_specs=pl.BlockSpec((None, rows, tn), lambda l, j: (l, 0, j)),
        compiler_params=_params(2, 40),
        name="ada",
    )(cvec, ada_w, ada_b.reshape(depth, 1, n6))


def _store_lane_tiles(u_ref, u):
    for k in range(u_ref.shape[0]):
        u_ref[k] = u[:, k * LANES:(k + 1) * LANES]


def _u_tiles(nb, n, d_ssm, tm, pseudo, clamp=lambda b, i: (b, i)):
    k4 = d_ssm // LANES

    def pseudo_map(b, i):
        b, i = clamp(b, i)
        return (0, 0, b * (n // tm) + i, 0)

    def tile_map(b, i):
        b, i = clamp(b, i)
        return (b, 0, i, 0)

    if pseudo:
        return (jax.ShapeDtypeStruct((1, k4, nb * n, LANES), F32), pl.BlockSpec((None, k4, tm, LANES), pseudo_map))
    return (jax.ShapeDtypeStruct((nb, k4, n, LANES), F32), pl.BlockSpec((None, k4, tm, LANES), tile_map))


def _inproj_body(x_ref, mod_ref, g_ref, w_ref, cw_ref, gc_ref, conv_ref, u_ref, *, rowlen, d_conv, cn):
    x = x_ref[...]
    tm = x.shape[0]
    h = _rms(x) * g_ref[...]
    h = h * (1.0 + mod_ref[1:2, :]) + mod_ref[0:1, :]
    hb = h.astype(BF16)
    t = jnp.bitwise_and(lax.broadcasted_iota(I32, (tm, 1), 0), rowlen - 1)
    first = t == 0
    last = t == rowlen - 1
    for j in range(d_conv // cn):
        c0 = j * cn
        bg = _dot(hb, w_ref[:, c0:c0 + cn])
        cg = _dot(hb, w_ref[:, d_conv + c0:d_conv + c0 + cn])
        v = _dot(hb, w_ref[:, 2 * d_conv + c0:2 * d_conv + c0 + cn])
        z = cg * v
        zp = jnp.where(first, 0.0, pltpu.roll(z, 1, 0))
        zn = jnp.where(last, 0.0, pltpu.roll(z, tm - 1, 0))
        cw = cw_ref[:, c0:c0 + cn]
        y = bg * (cw[0:1, :] * zp + cw[1:2, :] * z + cw[2:3, :] * zn)
        for hd in range(cn // HEAD_DIM):
            lo = hd * HEAD_DIM
            yh = _rms(y[:, lo:lo + HEAD_DIM]) * gc_ref[:, c0 + lo:c0 + lo + HEAD_DIM]
            conv_ref[:, c0 + lo:c0 + lo + HEAD_DIM] = yh.astype(conv_ref.dtype)
    _store_lane_tiles(u_ref, _dot(hb, w_ref[:, 3 * d_conv:]))


def _inproj(x, mod4, l, mod_row, norm_g, w_bf, conv_w, gc, *, rowlen, tm, pseudo):
    nb, n, d = x.shape
    depth, _, d_in = w_bf.shape
    d_conv = conv_w.shape[-1]
    d_ssm = d_in - 3 * d_conv
    cn = min(512, d_conv)
    body = functools.partial(_inproj_body, rowlen=rowlen, d_conv=d_conv, cn=cn)
    u_shape, u_spec = _u_tiles(nb, n, d_ssm, tm, pseudo)
    return pl.pallas_call(
        body,
        out_shape=(jax.ShapeDtypeStruct((nb, n, d_conv), BF16), u_shape),
        grid=(nb, n // tm),
        in_specs=[
            pl.BlockSpec((None, tm, d), lambda b, i: (b, i, 0)),
            pl.BlockSpec((None, None, N_MOD, d), lambda b, i: (l, mod_row(b), 0, 0)),
            pl.BlockSpec((None, 1, d), lambda b, i: (l, 0, 0)),
            pl.BlockSpec((None, d, d_in), lambda b, i: (l, 0, 0), pipeline_mode=pl.Buffered(1)),
            pl.BlockSpec((None, 3, d_conv), lambda b, i: (l, 0, 0)),
            pl.BlockSpec((None, 1, d_conv), lambda b, i: (l, 0, 0)),
        ],
        out_specs=(pl.BlockSpec((None, tm, d_conv), lambda b, i: (b, i, 0)), u_spec),
        compiler_params=_params(2, 56),
        name="inproj",
    )(x, mod4, norm_g, w_bf, conv_w, gc)


def _uproj_body(x_ref, mod_ref, g_ref, w_ref, u_ref):
    h = _rms(x_ref[...]) * g_ref[...]
    h = h * (1.0 + mod_ref[1:2, :]) + mod_ref[0:1, :]
    _store_lane_tiles(u_ref, _dot(h.astype(BF16), w_ref[...]))


def _uproj(x, mod4, l, mod_row, norm_g, w_bf, d_ssm, *, tm):
    nb, n, d = x.shape
    d_in = w_bf.shape[-1]
    col_blk = (d_in - d_ssm) // d_ssm
    u_shape, u_spec = _u_tiles(nb, n, d_ssm, tm, True)
    return pl.pallas_call(
        _uproj_body,
        out_shape=u_shape,
        grid=(nb, n // tm),
        in_specs=[
            pl.BlockSpec((None, tm, d), lambda b, i: (b, i, 0)),
            pl.BlockSpec((None, None, N_MOD, d), lambda b, i: (l, mod_row(b), 0, 0)),
            pl.BlockSpec((None, 1, d), lambda b, i: (l, 0, 0)),
            pl.BlockSpec((None, d, d_ssm), lambda b, i: (l, 0, col_blk)),
        ],
        out_specs=u_spec,
        compiler_params=_params(2, 32),
        name="uproj",
    )(x, mod4, norm_g, w_bf)


N_LEVELS = 7
POW_ROWS = 24


def _s5prep_body(kk_ref, lr_ref, li_ref, ldt_ref, bc1_ref, bc2_ref, ca_ref, cb_ref, dv_ref,
                 wcat_ref, mtot_ref, v_ref, a1_ref, a2_ref):
    t_ = CHUNK
    kk = kk_ref[...]
    lane_blk = jnp.right_shift(lax.broadcasted_iota(I32, (SSM_GROUP, CHUNK_W), 1), 4)
    lane = lax.broadcasted_iota(I32, (1, LANES), 1)
    sgn = jnp.where(lane < LANES // 2, -1.0, 1.0)
    eye = jnp.where(lax.broadcasted_iota(I32, (LANES, LANES), 0)
                    == lax.broadcasted_iota(I32, (LANES, LANES), 1), 1.0, 0.0)
    mtot = jnp.zeros((CHUNK_W, CHUNK_W), F32)
    for d in range(2):
        lr = lr_ref[d]
        li = li_ref[d]
        dt = jnp.exp(ldt_ref[d])
        pm = jnp.exp(kk * (dt * lr))
        ang = kk * (dt * li)
        pr = pm * jnp.cos(ang)
        pi = pm * jnp.sin(ang)
        ar = pr[1:2, :]
        ai = pi[1:2, :]
        den = lr * lr + li * li
        nr = ar - 1.0
        kr = (nr * lr + ai * li) / den
        ki = (ai * lr - nr * li) / den
        bc1 = bc1_ref[d]
        bc2 = bc2_ref[d]
        bb1 = kr * bc1 + ki * bc2
        bb2 = kr * bc2 - ki * bc1
        ca = ca_ref[d]
        cb = cb_ref[d]
        cak = [ca * pr[k:k + 1, :] + cb * pi[k:k + 1, :] for k in range(t_ + 1)]
        lag_order = range(t_) if d == 0 else range(t_ - 1, -1, -1)
        cak_all = jnp.concatenate([cak[k] for k in lag_order], axis=0)
        kall_t = _dot3(bb1, cak_all, _NT)
        rows = []
        for j in range(t_):
            if d == 0:
                shift, keep = (SSM_GROUP * j) % CHUNK_W, lane_blk >= j
            else:
                shift, keep = (SSM_GROUP * (j + 1)) % CHUNK_W, lane_blk <= j
            r = pltpu.roll(kall_t, shift, 1) if shift else kall_t
            rows.append(jnp.where(keep, r, 0.0))
        mtot = mtot + jnp.concatenate(rows, axis=0)
        e_v = [i + 1 for i in range(t_)] if d == 0 else [t_ - i for i in range(t_)]
        v_t = jnp.concatenate([cak[e] for e in e_v], axis=0)
        v_ref[d] = _dot3(eye, v_t, _NT).astype(v_ref.dtype)
        e_w = [t_ - 1 - j for j in range(t_)] if d == 0 else list(range(t_))
        w = jnp.concatenate([bb1 * pr[e:e + 1, :] + bb2 * pi[e:e + 1, :] for e in e_w], axis=0)
        wcat_ref[:, d * LANES:(d + 1) * LANES] = w.astype(wcat_ref.dtype)
        a1_ref[d] = pr[t_:t_ + 8, :]
        a2_ref[d] = sgn * pi[t_:t_ + 8, :]
    diag = (lax.broadcasted_iota(I32, (CHUNK_W, CHUNK_W), 0)
            == lax.broadcasted_iota(I32, (CHUNK_W, CHUNK_W), 1))
    mtot = mtot + jnp.where(diag, dv_ref[...], 0.0)
    mtot_ref[...] = mtot.astype(mtot_ref.dtype)


def _s5prep(l, kk, lr_t, li_t, ldt, bc1, bc2, ca, cb, dv):
    g = lr_t.shape[2]
    spec5 = lambda r, c: pl.BlockSpec((None, 2, None, r, c), lambda i: (l, 0, i, 0, 0))
    return pl.pallas_call(
        _s5prep_body,
        out_shape=(jax.ShapeDtypeStruct((g, CHUNK_W, 2 * LANES), BF16),
                   jax.ShapeDtypeStruct((g, CHUNK_W, CHUNK_W), BF16),
                   jax.ShapeDtypeStruct((2, g, LANES, CHUNK_W), BF16),
                   jax.ShapeDtypeStruct((2, g, 8, LANES), F32),
                   jax.ShapeDtypeStruct((2, g, 8, LANES), F32)),
        grid=(g,),
        in_specs=[pl.BlockSpec((POW_ROWS, 1), lambda i: (0, 0)),
                  spec5(1, LANES), spec5(1, LANES), spec5(1, 1),
                  spec5(SSM_GROUP, LANES), spec5(SSM_GROUP, LANES),
                  spec5(SSM_GROUP, LANES), spec5(SSM_GROUP, LANES),
                  pl.BlockSpec((None, None, 1, CHUNK_W), lambda i: (l, i, 0, 0))],
        out_specs=(pl.BlockSpec((None, CHUNK_W, 2 * LANES), lambda i: (i, 0, 0)),
                   pl.BlockSpec((None, CHUNK_W, CHUNK_W), lambda i: (i, 0, 0)),
                   pl.BlockSpec((2, None, LANES, CHUNK_W), lambda i: (0, i, 0, 0)),
                   pl.BlockSpec((2, None, 8, LANES), lambda i: (0, i, 0, 0)),
                   pl.BlockSpec((2, None, 8, LANES), lambda i: (0, i, 0, 0))),
        compiler_params=_params(1, 32),
        name="s5prep",
    )(kk, lr_t, li_t, ldt, bc1, bc2, ca, cb, dv)


def _s5mix_body(*refs, seg, with_h0):
    if with_h0:
        u_ref, wcat_ref, m_ref, v_ref, a1_ref, a2_ref, h0_ref, y_ref = refs
    else:
        u_ref, wcat_ref, m_ref, v_ref, a1_ref, a2_ref, y_ref, fin_ref, fs_ref = refs
    k4, n, _ = u_ref.shape
    c = n // CHUNK
    gpt = LANES // SSM_GROUP
    b = pl.program_id(0)
    rowm = jnp.bitwise_and(lax.broadcasted_iota(I32, (1, c, 1), 1), seg - 1)
    shifts = [1 << j for j in range(seg.bit_length() - 1)]

    def swap(x):
        return pltpu.roll(x, LANES // 2, 2)

    def scan(s, d, g0):
        a1 = a1_ref[d, g0:g0 + gpt]
        a2 = a2_ref[d, g0:g0 + gpt]
        if with_h0:
            h0 = h0_ref[d, g0:g0 + gpt, pl.ds(b, 1), :]
        else:
            h0 = 0.0
        if d == 0:
            x = jnp.where(rowm == 0, h0, pltpu.roll(s, 1, 1))
        else:
            x = jnp.where(rowm == seg - 1, h0, pltpu.roll(s, c - 1, 1))
        for j, sh in enumerate(shifts):
            if d == 0:
                xs = jnp.where(rowm >= sh, pltpu.roll(x, sh, 1), 0.0)
            else:
                xs = jnp.where(rowm < seg - sh, pltpu.roll(x, c - sh, 1), 0.0)
            x = x + a1[:, j:j + 1, :] * xs + a2[:, j:j + 1, :] * swap(xs)
        return x

    for k in range(k4):
        g0 = k * gpt
        xk = jnp.concatenate([u_ref[k, pl.ds(t, c, stride=CHUNK), :] for t in range(CHUNK)], axis=0)
        xt = xk.T
        zs, ss = [], []
        for gg in range(gpt):
            rg = jnp.concatenate([xt[gg * SSM_GROUP:(gg + 1) * SSM_GROUP, t * c:(t + 1) * c]
                                  for t in range(CHUNK)], axis=0)
            z = rg.T.astype(BF16)
            zs.append(z)
            ss.append(_dot(z, wcat_ref[g0 + gg]))
        hins = []
        for d in range(2):
            sd = jnp.stack([s[:, d * LANES:(d + 1) * LANES] for s in ss], axis=0)
            hin = scan(sd, d, g0)
            hins.append(hin)
            if not with_h0:
                a1 = a1_ref[d, g0:g0 + gpt]
                a2 = a2_ref[d, g0:g0 + gpt]
                fin = a1[:, 0:1, :] * hin + a2[:, 0:1, :] * swap(hin) + sd
                first = seg - 1 if d == 0 else 0
                for gg in range(gpt):
                    fs_ref[...] = fin[gg]
                    fin_ref[d, g0 + gg] = fs_ref[pl.ds(first, c // seg, stride=seg), :]
        yts = []
        for gg in range(gpt):
            lhs = jnp.concatenate([zs[gg], hins[0][gg].astype(BF16), hins[1][gg].astype(BF16)], axis=1)
            rhs = jnp.concatenate([m_ref[g0 + gg], v_ref[0, g0 + gg], v_ref[1, g0 + gg]], axis=0)
            yts.append(_dot(lhs, rhs).T)
        xto = jnp.concatenate(
            [jnp.concatenate([yt[t * SSM_GROUP:(t + 1) * SSM_GROUP, :] for yt in yts], axis=0)
             for t in range(CHUNK)], axis=1)
        xo = xto.T
        for t in range(CHUNK):
            y_ref[k, pl.ds(t, c, stride=CHUNK), :] = xo[t * c:(t + 1) * c, :]


def _s5mix(u4, wcat, mtot, v, a1, a2, *, seg, h0=None):
    nb, k4, n, _ = u4.shape
    g = wcat.shape[0]
    c = n // CHUNK
    with_h0 = h0 is not None
    body = functools.partial(_s5mix_body, seg=seg, with_h0=with_h0)
    tile = pl.BlockSpec((None, k4, n, LANES), lambda b: (b, 0, 0, 0))
    res = lambda shape: pl.BlockSpec(shape, lambda b: (0,) * len(shape), pipeline_mode=pl.Buffered(1))
    in_specs = [tile, res((g, CHUNK_W, 2 * LANES)), res((g, CHUNK_W, CHUNK_W)), res((2, g, LANES, CHUNK_W)),
                res((2, g, 8, LANES)), res((2, g, 8, LANES))]
    args = [u4, wcat, mtot, v, a1, a2]
    y_shape = jax.ShapeDtypeStruct((nb, k4, n, LANES), F32)
    if with_h0:
        in_specs.append(res((2, g, h0.shape[2], LANES)))
        args.append(h0)
        out_shape, out_specs, scratch = y_shape, tile, []
    else:
        out_shape = (y_shape, jax.ShapeDtypeStruct((2, g, c // seg, LANES), F32))
        out_specs = (tile, pl.BlockSpec((2, g, c // seg, LANES), lambda b: (0, 0, 0, 0)))
        scratch = [pltpu.VMEM((c, LANES), F32)]
    return pl.pallas_call(
        body,
        out_shape=out_shape,
        grid=(nb,),
        in_specs=in_specs,
        out_specs=out_specs,
        scratch_shapes=scratch,
        compiler_params=_params(1, 48),
        name="s5mix",
    )(*args)


def _postmix_body(*refs, d_conv, aliased, nb_real, padded):
    if aliased:
        refs = refs[1:]
    if padded:
        @pl.when(pl.program_id(0) < nb_real)
        def _():
            _postmix_tile(*refs, d_conv=d_conv)

        @pl.when(pl.program_id(0) >= nb_real)
        def _():
            refs[-2][...] = jnp.zeros_like(refs[-2])
    else:
        _postmix_tile(*refs, d_conv=d_conv)


def _postmix_tile(cn_ref, y_ref, x_ref, mod_ref, wglu_ref, gs_ref, wout_ref, g2_ref, r2_ref, r1_ref,
                  xo_ref, h2_ref, lg_ref, *, d_conv):
    yg = jax.nn.gelu(jnp.concatenate([y_ref[k] for k in range(y_ref.shape[0])], axis=1))
    z = _dot(yg.astype(BF16), wglu_ref[...])
    s = yg * jax.nn.sigmoid(z)
    d_ssm = s.shape[-1]
    sn = [(_rms(s[:, lo:lo + HEAD_DIM]) * gs_ref[:, lo:lo + HEAD_DIM]).astype(BF16)
          for lo in range(0, d_ssm, HEAD_DIM)]
    mix = _dot(cn_ref[...], wout_ref[0:d_conv, :]) + _dot(jnp.concatenate(sn, axis=1), wout_ref[d_conv:, :])
    xn = x_ref[...] + mod_ref[2:3, :] * mix
    xo_ref[...] = xn
    h2 = _rms(xn) * g2_ref[...]
    h2 = h2 * (1.0 + mod_ref[4:5, :]) + mod_ref[3:4, :]
    hi, lo_ = _split_bf16(h2)
    tm, d = h2.shape
    for k in range(d // LANES):
        h2_ref[pl.ds(k, tm, stride=d // LANES), :] = h2[:, k * LANES:(k + 1) * LANES]
    d1 = _dot(hi, r2_ref[...])
    d2 = _dot(lo_, r1_ref[...])
    lg_ref[...] = d1[:, 0:LANES] + d1[:, LANES:2 * LANES] + d2


def _postmix(conv_n, y4, x, mod4, l, mod_row, wglu_bf, gs, wout_bf, norm2_g, r2, r1, *, tm, pseudo,
             table_tokens, table=None, tok0=0, pad_samples=0):
    nb, n, d = x.shape
    d_conv = conv_n.shape[-1]
    d_ssm = y4.shape[1] * LANES
    tok_rows = d // LANES
    aliased = table is not None
    body = functools.partial(_postmix_body, d_conv=d_conv, aliased=aliased, nb_real=nb, padded=pad_samples > 0)

    def clamp(b, i):
        return jnp.minimum(b, nb - 1), jnp.where(b < nb, i, n // tm - 1)

    def tok_map(b, i):
        b, i = clamp(b, i)
        return (b, i, 0)

    _, y_spec = _u_tiles(nb, n, d_ssm, tm, pseudo, clamp)
    tok = lambda w: pl.BlockSpec((None, tm, w), tok_map)
    lay = lambda r, c, **kw: pl.BlockSpec((None, r, c), lambda b, i: (l, 0, 0), **kw)
    in_specs = [tok(d_conv), y_spec, tok(d),
                pl.BlockSpec((None, None, N_MOD, d), lambda b, i: (l, mod_row(jnp.minimum(b, nb - 1)), 0, 0)),
                lay(d_ssm, d_ssm, pipeline_mode=pl.Buffered(1)),
                lay(1, d_ssm),
                lay(d_conv + d_ssm, d, pipeline_mode=pl.Buffered(1)),
                lay(1, d),
                lay(d, 2 * LANES, pipeline_mode=pl.Buffered(1)),
                lay(d, LANES, pipeline_mode=pl.Buffered(1))]
    args = [conv_n, y4, x, mod4, wglu_bf, gs, wout_bf, norm2_g, r2, r1]
    if aliased:
        in_specs.insert(0, pl.BlockSpec(memory_space=pl.ANY))
        args.insert(0, table)
    blk0 = tok0 // tm
    return pl.pallas_call(
        body,
        out_shape=(jax.ShapeDtypeStruct((nb, n, d), F32),
                   jax.ShapeDtypeStruct((table_tokens * tok_rows, LANES), F32),
                   jax.ShapeDtypeStruct((nb, n, LANES), F32)),
        grid=(nb + pad_samples, n // tm),
        in_specs=in_specs,
        out_specs=(tok(d),
                   pl.BlockSpec((tm * tok_rows, LANES), lambda b, i: (blk0 + b * (n // tm) + i, 0)),
                   tok(LANES)),
        input_output_aliases={0: 1} if aliased else {},
        compiler_params=_params(2, 56),
        name="postmix",
    )(*args)


BISECT_STEPS = 32

def _route_body(lg_ref, pos_ref, wv_ref, cum_ref, tri_ref, *, cap):
    nb, ne, n = lg_ref.shape
    rc = min(256, n)
    for r0 in range(0, n, rc):
        ri = lax.broadcasted_iota(I32, (rc, n), 0) + r0
        ci = lax.broadcasted_iota(I32, (rc, n), 1)
        tri_ref[r0:r0 + rc, :] = jnp.where(ri < ci, 1.0, 0.0).astype(BF16)
    lg = lg_ref[...]
    e = jnp.exp(lg - jnp.max(lg, axis=1, keepdims=True))
    aff = e / jnp.sum(e, axis=1, keepdims=True)
    capf = float(cap)

    def enough(t):
        return jnp.sum(jnp.where(aff >= t, 1.0, 0.0), axis=2, keepdims=True) >= capf

    hi = jnp.full((nb, ne, 1), 2.0, F32)
    for j in range(6, -1, -1):
        cand = hi * (2.0 ** -(1 << j))
        hi = jnp.where(enough(cand), hi, cand)
    half = hi * 0.5
    lo = jnp.where(enough(half), half, 0.0)
    for _ in range(BISECT_STEPS):
        mid = (lo + hi) * 0.5
        ok = enough(mid)
        lo = jnp.where(ok, mid, lo)
        hi = jnp.where(ok, hi, mid)
    gt = aff >= hi
    eq = jnp.logical_and(aff >= lo, aff < hi)
    need = capf - jnp.sum(jnp.where(gt, 1.0, 0.0), axis=2, keepdims=True)
    tri = tri_ref[...]
    eq_rank = _dot(jnp.where(eq, 1.0, 0.0).astype(BF16).reshape(nb * ne, n), tri).reshape(nb, ne, n)
    sel = jnp.logical_or(gt, jnp.logical_and(eq, eq_rank < need))
    pos = _dot(jnp.where(sel, 1.0, 0.0).astype(BF16).reshape(nb * ne, n), tri).reshape(nb, ne, n)
    pos_ref[...] = jnp.where(sel, pos.astype(I32), -1)
    wv_ref[...] = jnp.where(sel, aff, 0.0)
    cum_ref[...] = pos.astype(I32)


def _route(logits_t, cap):
    nb, ne, n = logits_t.shape
    body = functools.partial(_route_body, cap=cap)
    full = pl.BlockSpec((nb, ne, n), lambda i: (0, 0, 0))
    return pl.pallas_call(
        body,
        out_shape=(jax.ShapeDtypeStruct((nb, ne, n), I32),
                   jax.ShapeDtypeStruct((nb, ne, n), F32),
                   jax.ShapeDtypeStruct((nb, ne, n), I32)),
        grid=(1,),
        in_specs=[full],
        out_specs=(full, full, full),
        scratch_shapes=[pltpu.VMEM((n, n), BF16)],
        compiler_params=_params(1, 48),
        name="route",
    )(logits_t)


def _slotidx_body(pos_ref, o_ref, *, cap):
    n, ne = pos_ref.shape
    pos = pos_ref[...]
    tok = lax.broadcasted_iota(I32, (n, 1), 0).astype(F32)
    slot = lax.broadcasted_iota(I32, (n, cap), 1)
    for e in range(ne):
        hit = jnp.where(slot == pos[:, e:e + 1], tok, 0.0)
        o_ref[e:e + 1, :] = jnp.sum(hit, axis=0, keepdims=True).astype(I32)


def _slotidx(pos_cols, cap):
    nb, n, ne = pos_cols.shape
    return pl.pallas_call(
        functools.partial(_slotidx_body, cap=cap),
        out_shape=jax.ShapeDtypeStruct((nb, ne, cap), I32),
        grid=(nb,),
        in_specs=[pl.BlockSpec((None, n, ne), lambda b: (b, 0, 0))],
        out_specs=pl.BlockSpec((None, ne, cap), lambda b: (b, 0, 0)),
        compiler_params=_params(1, 32),
        name="slotidx",
    )(pos_cols)


def _ffn_body(idx_ref, tab_ref, wg_ref, wu_ref, wd_ref, y_ref, xbuf_ref, xs_ref, acc_ref, sem, *, tok_rows, nf):
    f = pl.program_id(2)
    blk = pl.program_id(0) * pl.num_programs(1) + pl.program_id(1)
    nblk = pl.num_programs(0) * pl.num_programs(1)
    step = blk * nf + f
    mh = xs_ref.shape[1]
    ch = mh // nf
    cur = jnp.bitwise_and(blk, 1)
    ring = jnp.bitwise_and(step, 1)

    def token_copy(blk_i, c, j, ring_i):
        tok = idx_ref[blk_i * mh + c * ch + j]
        src = tab_ref.at[pl.ds(pl.multiple_of(tok * tok_rows, tok_rows), tok_rows), :]
        dst = xbuf_ref.at[ring_i, pl.ds(pl.multiple_of(j * tok_rows, tok_rows), tok_rows), :]
        return pltpu.make_async_copy(src, dst, sem.at[ring_i])

    def wait_chunk(ring_i):
        pltpu.make_async_copy(tab_ref.at[pl.ds(0, ch * tok_rows), :], xbuf_ref.at[ring_i], sem.at[ring_i]).wait()

    def convert(ring_i, xs_slot, c):
        rows = pl.ds(pl.multiple_of(c * ch, ch), ch)
        for k in range(tok_rows):
            w = xbuf_ref[ring_i, pl.ds(k, ch, stride=tok_rows), :]
            xs_ref[xs_slot, rows, k * LANES:(k + 1) * LANES] = w.astype(BF16)

    def issue_loop(blk_i, c, ring_i):
        def issue(j, carry):
            token_copy(blk_i, c, j, ring_i).start()
            return carry
        lax.fori_loop(0, ch, issue, 0)

    @pl.when(step == 0)
    def _():
        for c in range(nf):
            issue_loop(0, c, 0)
            wait_chunk(0)
            convert(0, 0, c)
        issue_loop(1 % nblk, 0, 1)

    @pl.when(f == 0)
    def _():
        acc_ref[...] = jnp.zeros_like(acc_ref)

    q = step + nf + 1
    qb = q // nf
    qb = jnp.where(qb < nblk, qb, 0)
    for j in range(ch):
        token_copy(qb, q % nf, j, ring).start(priority=j % 2)

    xs = xs_ref[cur]
    g = _dot(xs, wg_ref[...].astype(BF16))
    u = _dot(xs, wu_ref[...].astype(BF16))
    act = (jax.nn.silu(g) * u).astype(BF16)
    acc_ref[...] += _dot(act, wd_ref[...].astype(BF16))

    wait_chunk(1 - ring)
    convert(1 - ring, 1 - cur, f)

    @pl.when(f == nf - 1)
    def _():
        y_ref[...] = acc_ref[...].astype(y_ref.dtype)

    @pl.when(step == nblk * nf - 1)
    def _():
        wait_chunk(ring)


def _ffn(slot_tok, table, l, w_gate, w_up, w_down, *, tf, m_split):
    _, ne, d, dff = w_gate.shape
    tok_rows = d // LANES
    m = slot_tok.shape[0] // ne
    mh = m // m_split
    nf = dff // tf
    assert mh % nf == 0 and table.shape[1] == LANES
    body = functools.partial(_ffn_body, tok_rows=tok_rows, nf=nf)
    grid_spec = pltpu.PrefetchScalarGridSpec(
        num_scalar_prefetch=1,
        grid=(m_split, ne, nf),
        in_specs=[pl.BlockSpec(memory_space=pl.ANY),
                  pl.BlockSpec((None, None, d, tf), lambda h, e, f, idx: (l, e, 0, f)),
                  pl.BlockSpec((None, None, d, tf), lambda h, e, f, idx: (l, e, 0, f)),
                  pl.BlockSpec((None, None, tf, d), lambda h, e, f, idx: (l, e, f, 0))],
        out_specs=pl.BlockSpec((None, mh, d), lambda h, e, f, idx: (e, h, 0)),
        scratch_shapes=[pltpu.VMEM((2, mh // nf * tok_rows, LANES), F32),
                        pltpu.VMEM((2, mh, d), BF16),
                        pltpu.VMEM((mh, d), F32),
                        pltpu.SemaphoreType.DMA((2,))])
    return pl.pallas_call(
        body,
        out_shape=jax.ShapeDtypeStruct((ne, m, d), BF16),
        grid_spec=grid_spec,
        compiler_params=_params(3, 56),
        name="ffn",
    )(slot_tok, table, w_gate, w_up, w_down)


COMBINE_WIN = 64


def _combine_body(*refs, cap, final):
    if final:
        st_ref, y_ref, pos_ref, wv_ref, x_ref, mod_ref, gf_ref, o_ref = refs
    else:
        st_ref, y_ref, pos_ref, wv_ref, x_ref, mod_ref, o_ref = refs
    ne = y_ref.shape[0]
    tt = x_ref.shape[0]
    win = COMBINE_WIN
    nt1 = pl.num_programs(1) + 1
    base = pl.program_id(0) * ne * nt1 + pl.program_id(1)
    pos = pos_ref[...]
    wv = wv_ref[...]

    def finish(moe):
        xn = x_ref[...] + mod_ref[5:6, :] * moe
        if final:
            xn = _rms(xn) * gf_ref[...]
        o_ref[...] = xn

    w0s = []
    fits = None
    for e in range(ne):
        s0 = st_ref[base + e * nt1]
        s1 = st_ref[base + e * nt1 + 1]
        w0 = jnp.minimum(jnp.left_shift(jnp.right_shift(s0, 4), 4), cap - win)
        ok = s1 - w0 <= win
        fits = ok if fits is None else jnp.logical_and(fits, ok)
        w0s.append(pl.multiple_of(w0, 16))

    @pl.when(fits)
    def _():
        lane = lax.broadcasted_iota(I32, (tt, LANES), 1)
        low = lane < win
        pieces, rows = [], []
        for e in range(0, ne, 2):
            tgt = jnp.where(low, lane + w0s[e], lane + (w0s[e + 1] - win))
            p = jnp.where(low, pos[:, e:e + 1], pos[:, e + 1:e + 2])
            w = jnp.where(low, wv[:, e:e + 1], wv[:, e + 1:e + 2])
            pieces.append(jnp.where(tgt == p, w, 0.0).astype(BF16))
            rows.append(y_ref[e, pl.ds(w0s[e], win), :])
            rows.append(y_ref[e + 1, pl.ds(w0s[e + 1], win), :])
        finish(_dot(jnp.concatenate(pieces, axis=1), jnp.concatenate(rows, axis=0)))

    @pl.when(jnp.logical_not(fits))
    def _():
        slot = lax.broadcasted_iota(I32, (tt, cap), 1)
        pieces = [jnp.where(slot == pos[:, e:e + 1], wv[:, e:e + 1], 0.0).astype(BF16) for e in range(ne)]
        finish(_dot(jnp.concatenate(pieces, axis=1), y_ref[...].reshape(ne * cap, y_ref.shape[-1])))


def _combine(starts, y, blk0, pos_cols, wv_cols, x, mod4, l, mod_row, cap, final_g=None, *, tt):
    nb, n, d = x.shape
    ne = y.shape[0]
    final = final_g is not None
    assert 2 * COMBINE_WIN == LANES and ne % 2 == 0 and cap % 16 == 0 and cap >= COMBINE_WIN
    body = functools.partial(_combine_body, cap=cap, final=final)
    in_specs = [pl.BlockSpec((ne, cap, d), lambda b, i, st: (0, blk0 + b, 0)),
                pl.BlockSpec((None, tt, ne), lambda b, i, st: (b, i, 0)),
                pl.BlockSpec((None, tt, ne), lambda b, i, st: (b, i, 0)),
                pl.BlockSpec((None, tt, d), lambda b, i, st: (b, i, 0)),
                pl.BlockSpec((None, None, N_MOD, d), lambda b, i, st: (l, mod_row(b), 0, 0))]
    args = [starts, y, pos_cols, wv_cols, x, mod4]
    if final:
        in_specs.append(pl.BlockSpec((1, d), lambda b, i, st: (0, 0)))
        args.append(final_g)
    grid_spec = pltpu.PrefetchScalarGridSpec(
        num_scalar_prefetch=1,
        grid=(nb, n // tt),
        in_specs=in_specs,
        out_specs=pl.BlockSpec((None, tt, d), lambda b, i, st: (b, i, 0)))
    return pl.pallas_call(
        body,
        out_shape=jax.ShapeDtypeStruct((nb, n, d), F32),
        grid_spec=grid_spec,
        compiler_params=_params(2, 56),
        name="combine",
    )(*args)


def kernel(x, c, ctx, c_ctx, ada_w, ada_b, norm1_g, w_in, conv_w, ssm_lam_re, ssm_lam_im, ssm_log_dt,
           ssm_b_re, ssm_b_im, ssm_c_re, ssm_c_im, ssm_d, ssm_w_glu, out_norm_conv_g, out_norm_ssm_g,
           w_out, norm2_g, router_w, exp_w_gate, exp_w_up, exp_w_down, final_norm_g):
    bsz, n, d = x.shape
    nc = ctx.shape[1]
    depth = ada_w.shape[0]
    d_conv = conv_w.shape[-1]
    d_ssm = ssm_d.shape[-1]
    g = d_ssm // SSM_GROUP
    p = ssm_lam_re.shape[-1]
    ne = router_w.shape[-1]
    dff = exp_w_gate.shape[-1]
    cap = EC_FACTOR * n // ne
    cap_c = EC_FACTOR * nc // ne
    nbc = bsz * nc
    assert bsz == 8 and 2 * p == LANES and ssm_b_re.shape[-1] == SSM_GROUP
    assert nbc == n and bsz * cap_c == cap
    assert n // CHUNK == 1 << N_LEVELS and nc % CHUNK == 0 and (nc // CHUNK) & (nc // CHUNK - 1) == 0
    assert d_conv % HEAD_DIM == 0 and d_ssm % HEAD_DIM == 0 and (3 * d_conv) % d_ssm == 0
    assert N_MOD * d == ada_w.shape[-1] and ne <= LANES

    tm = min(512, n)
    tt = min(256, n)
    tf = min(256, dff)

    rows = 16
    cvec = jnp.zeros((rows, d), F32).at[:bsz].set(c).at[bsz].set(c_ctx)
    mod4 = _ada(cvec, ada_w, ada_b).reshape(depth, rows, N_MOD, d)
    lat_row = lambda b: b
    ctx_row = lambda b: bsz

    w_in_bf = w_in.astype(BF16)
    w_out_bf = w_out.astype(BF16)
    w_glu_bf = ssm_w_glu.astype(BF16)
    rw_hi, rw_lo = _split_bf16(router_w)
    zpad = jnp.zeros((depth, d, LANES - ne), BF16)
    r1 = jnp.concatenate([rw_hi, zpad], axis=-1)
    r2 = jnp.concatenate([rw_hi, zpad, rw_lo, zpad], axis=-1)

    tile2 = lambda a: jnp.concatenate([a, a], axis=-1)
    lr_t = tile2(ssm_lam_re)[:, :, :, None, :]
    li_t = tile2(ssm_lam_im)[:, :, :, None, :]
    ldt = ssm_log_dt[:, :, :, None, None]
    brt = jnp.swapaxes(ssm_b_re, -1, -2)
    bit = jnp.swapaxes(ssm_b_im, -1, -2)
    bc1 = jnp.concatenate([brt, bit], axis=-1)
    bc2 = jnp.concatenate([-bit, brt], axis=-1)
    ca = jnp.concatenate([ssm_c_re, -ssm_c_im], axis=-1)
    cb = jnp.concatenate([-ssm_c_im, -ssm_c_re], axis=-1)
    dv = jnp.tile(ssm_d.reshape(depth, g, 1, SSM_GROUP), (1, 1, 1, CHUNK))

    g1n = norm1_g.reshape(depth, 1, d)
    g2n = norm2_g.reshape(depth, 1, d)
    gcn = out_norm_conv_g.reshape(depth, 1, d_conv)
    gsn = out_norm_ssm_g.reshape(depth, 1, d_ssm)
    gfin = final_norm_g.reshape(1, d)

    kk = jnp.asarray([float(k) for k in range(CHUNK + 1)]
                     + [float(CHUNK << j) for j in range(1, N_LEVELS)] + [0.0], F32).reshape(POW_ROWS, 1)

    xl = x
    xc = ctx
    for l in range(depth):
        last = l == depth - 1
        conv_l, u_lat = _inproj(xl, mod4, l, lat_row, g1n, w_in_bf, conv_w, gcn, rowlen=GRID_W, tm=tm,
                                pseudo=False)
        if last:
            u_ctx = _uproj(xc, mod4, l, ctx_row, g1n, w_in_bf, d_ssm, tm=nc)
        else:
            conv_c, u_ctx = _inproj(xc, mod4, l, ctx_row, g1n, w_in_bf, conv_w, gcn, rowlen=nc, tm=nc,
                                    pseudo=True)

        wcat, mtot, vmat, a1, a2 = _s5prep(l, kk, lr_t, li_t, ldt, bc1, bc2, ca, cb, dv)
        y_ctx, h0 = _s5mix(u_ctx, wcat, mtot, vmat, a1, a2, seg=nc // CHUNK)
        y_lat = _s5mix(u_lat, wcat, mtot, vmat, a1, a2, seg=n // CHUNK, h0=h0)

        n_tab = bsz * n + (0 if last else nbc)
        xl, table, lg = _postmix(conv_l, y_lat, xl, mod4, l, lat_row, w_glu_bf, gsn, w_out_bf, g2n, r2, r1,
                                 tm=tm, pseudo=False, table_tokens=n_tab, pad_samples=0 if last else nbc // n)
        lg_t = jnp.swapaxes(lg[:, :, :ne], 1, 2)
        pos, wv, cum = _route(lg_t, cap)
        capcol = jnp.full((bsz, ne, 1), cap, I32)
        starts = jnp.concatenate([cum[:, :, ::tt], capcol], axis=2).reshape(-1)
        pos_cols = jnp.swapaxes(pos, 1, 2)
        wv_cols = jnp.swapaxes(wv, 1, 2)
        offs_n = (jnp.arange(bsz, dtype=I32) * n)[:, None, None]
        slot_tok = jnp.swapaxes(_slotidx(pos_cols, cap) + offs_n, 0, 1).reshape(ne, bsz * cap)

        if not last:
            xc, table, lgc = _postmix(conv_c, y_ctx, xc, mod4, l, ctx_row, w_glu_bf, gsn, w_out_bf, g2n,
                                      r2, r1, tm=nc, pseudo=True, table_tokens=n_tab, table=table, tok0=bsz * n)
            lgc_t = jnp.swapaxes(lgc[:, :, :ne], 1, 2)
            posc, wvc, cumc = _route(lgc_t, cap_c)
            offs = (jnp.arange(bsz, dtype=I32) * cap_c)[:, None, None]
            posc = jnp.where(posc >= 0, posc + offs, -1)
            posc_cols = jnp.swapaxes(posc, 1, 2).reshape(1, nbc, ne)
            wvc_cols = jnp.swapaxes(wvc, 1, 2).reshape(1, nbc, ne)
            startsc = jnp.swapaxes(cumc[:, :, ::tt] + offs, 0, 1).reshape(ne, -1)
            startsc = jnp.concatenate([startsc, jnp.full((ne, 1), cap, I32)], axis=1).reshape(-1)
            slot_tok = jnp.concatenate([slot_tok, _slotidx(posc_cols, cap)[0] + bsz * n], axis=1)

        m_split = 2
        mh = slot_tok.shape[1] // m_split
        slot_tok = jnp.swapaxes(slot_tok.reshape(ne, m_split, mh), 0, 1).reshape(-1)
        y_exp = _ffn(slot_tok, table, l, exp_w_gate, exp_w_up, exp_w_down, tf=tf, m_split=m_split)
        xl = _combine(starts, y_exp, 0, pos_cols, wv_cols, xl, mod4, l, lat_row, cap,
                      gfin if last else None, tt=tt)
        if not last:
            xc = _combine(startsc, y_exp, bsz, posc_cols, wvc_cols, xc.reshape(1, nbc, d), mod4, l, ctx_row, cap,
                          tt=tt).reshape(bsz, nc, d)
    return xl
```

```python
import functools

import jax
import jax.numpy as jnp
from jax import lax
from jax.experimental import pallas as pl
from jax.experimental.pallas import tpu as pltpu

F32 = jnp.float32
BF16 = jnp.bfloat16
I32 = jnp.int32

EPS = 1e-6
GRID_W = 64
HEAD_DIM = 128
SSM_GROUP = 16
EC_FACTOR = 2
N_MOD = 6
CHUNK = 16
LANES = 128
CHUNK_W = CHUNK * SSM_GROUP

_MIB = 1 << 20


def _params(n_axes, vmem_mib):
    return pltpu.CompilerParams(dimension_semantics=("arbitrary",) * n_axes,
                                vmem_limit_bytes=vmem_mib * _MIB)


def _split_bf16(a):
    hi = a.astype(BF16)
    lo = (a - hi.astype(F32)).astype(BF16)
    return hi, lo


_NN = (((1,), (0,)), ((), ()))
_NT = (((1,), (1,)), ((), ()))


def _dot(a, b, dims=_NN):
    return lax.dot_general(a, b, dims, preferred_element_type=F32)


def _dot3(a, b, dims=_NN):
    ah, al = _split_bf16(a)
    bh, bl = _split_bf16(b)
    return _dot(ah, bh, dims) + _dot(ah, bl, dims) + _dot(al, bh, dims)


def _rms(x):
    return x * lax.rsqrt(jnp.mean(x * x, axis=-1, keepdims=True) + EPS)


def _ada_body(c_ref, w_ref, b_ref, o_ref):
    s = jax.nn.silu(c_ref[...])
    o_ref[...] = _dot3(s, w_ref[...]) + b_ref[...]


def _ada(cvec, ada_w, ada_b):
    depth, d, n6 = ada_w.shape
    tn = next(t for t in (1024, 512, 256, 128) if n6 % t == 0)
    rows = cvec.shape[0]
    return pl.pallas_call(
        _ada_body,
        out_shape=jax.ShapeDtypeStruct((depth, rows, n6), F32),
        grid=(depth, n6 // tn),
        in_specs=[
            pl.BlockSpec((rows, d), lambda l, j: (0, 0)),
            pl.BlockSpec((None, d, tn), lambda l, j: (l, 0, j)),
            pl.BlockSpec((None, 1, tn), lambda l, j: (l, 0, j)),
        ],
        out_specs=pl.BlockSpec((None, rows, tn), lambda l, j: (l, 0, j)),
        compiler_params=_params(2, 40),
        name="ada",
    )(cvec, ada_w, ada_b.reshape(depth, 1, n6))


def _store_lane_tiles(u_ref, u):
    for k in range(u_ref.shape[0]):
        u_ref[k] = u[:, k * LANES:(k + 1) * LANES]


def _u_tiles(nb, n, d_ssm, tm, pseudo, clamp=lambda b, i: (b, i)):
    k4 = d_ssm // LANES

    def pseudo_map(b, i):
        b, i = clamp(b, i)
        return (0, 0, b * (n // tm) + i, 0)

    def tile_map(b, i):
        b, i = clamp(b, i)
        return (b, 0, i, 0)

    if pseudo:
        return (jax.ShapeDtypeStruct((1, k4, nb * n, LANES), F32), pl.BlockSpec((None, k4, tm, LANES), pseudo_map))
    return (jax.ShapeDtypeStruct((nb, k4, n, LANES), F32), pl.BlockSpec((None, k4, tm, LANES), tile_map))


def _inproj_body(x_ref, mod_ref, g_ref, w_ref, cw_ref, gc_ref, conv_ref, u_ref, *, rowlen, d_conv, cn):
    x = x_ref[...]
    tm = x.shape[0]
    h = _rms(x) * g_ref[...]
    h = h * (1.0 + mod_ref[1:2, :]) + mod_ref[0:1, :]
    hb = h.astype(BF16)
    t = jnp.bitwise_and(lax.broadcasted_iota(I32, (tm, 1), 0), rowlen - 1)
    first = t == 0
    last = t == rowlen - 1
    for j in range(d_conv // cn):
        c0 = j * cn
        bg = _dot(hb, w_ref[:, c0:c0 + cn])
        cg = _dot(hb, w_ref[:, d_conv + c0:d_conv + c0 + cn])
        v = _dot(hb, w_ref[:, 2 * d_conv + c0:2 * d_conv + c0 + cn])
        z = cg * v
        zp = jnp.where(first, 0.0, pltpu.roll(z, 1, 0))
        zn = jnp.where(last, 0.0, pltpu.roll(z, tm - 1, 0))
        cw = cw_ref[:, c0:c0 + cn]
        y = bg * (cw[0:1, :] * zp + cw[1:2, :] * z + cw[2:3, :] * zn)
        for hd in range(cn // HEAD_DIM):
            lo = hd * HEAD_DIM
            yh = _rms(y[:, lo:lo + HEAD_DIM]) * gc_ref[:, c0 + lo:c0 + lo + HEAD_DIM]
            conv_ref[:, c0 + lo:c0 + lo + HEAD_DIM] = yh.astype(conv_ref.dtype)
    _store_lane_tiles(u_ref, _dot(hb, w_ref[:, 3 * d_conv:]))


def _inproj(x, mod4, l, mod_row, norm_g, w_bf, conv_w, gc, *, rowlen, tm, pseudo):
    nb, n, d = x.shape
    depth, _, d_in = w_bf.shape
    d_conv = conv_w.shape[-1]
    d_ssm = d_in - 3 * d_conv
    cn = min(512, d_conv)
    body = functools.partial(_inproj_body, rowlen=rowlen, d_conv=d_conv, cn=cn)
    u_shape, u_spec = _u_tiles(nb, n, d_ssm, tm, pseudo)
    return pl.pallas_call(
        body,
        out_shape=(jax.ShapeDtypeStruct((nb, n, d_conv), BF16), u_shape),
        grid=(nb, n // tm),
        in_specs=[
            pl.BlockSpec((None, tm, d), lambda b, i: (b, i, 0)),
            pl.BlockSpec((None, None, N_MOD, d), lambda b, i: (l, mod_row(b), 0, 0)),
            pl.BlockSpec((None, 1, d), lambda b, i: (l, 0, 0)),
            pl.BlockSpec((None, d, d_in), lambda b, i: (l, 0, 0), pipeline_mode=pl.Buffered(1)),
            pl.BlockSpec((None, 3, d_conv), lambda b, i: (l, 0, 0)),
            pl.BlockSpec((None, 1, d_conv), lambda b, i: (l, 0, 0)),
        ],
        out_specs=(pl.BlockSpec((None, tm, d_conv), lambda b, i: (b, i, 0)), u_spec),
        compiler_params=_params(2, 56),
        name="inproj",
    )(x, mod4, norm_g, w_bf, conv_w, gc)


def _uproj_body(x_ref, mod_ref, g_ref, w_ref, u_ref):
    h = _rms(x_ref[...]) * g_ref[...]
    h = h * (1.0 + mod_ref[1:2, :]) + mod_ref[0:1, :]
    _store_lane_tiles(u_ref, _dot(h.astype(BF16), w_ref[...]))


def _uproj(x, mod4, l, mod_row, norm_g, w_bf, d_ssm, *, tm):
    nb, n, d = x.shape
    d_in = w_bf.shape[-1]
    col_blk = (d_in - d_ssm) // d_ssm
    u_shape, u_spec = _u_tiles(nb, n, d_ssm, tm, True)
    return pl.pallas_call(
        _uproj_body,
        out_shape=u_shape,
        grid=(nb, n // tm),
        in_specs=[
            pl.BlockSpec((None, tm, d), lambda b, i: (b, i, 0)),
            pl.BlockSpec((None, None, N_MOD, d), lambda b, i: (l, mod_row(b), 0, 0)),
            pl.BlockSpec((None, 1, d), lambda b, i: (l, 0, 0)),
            pl.BlockSpec((None, d, d_ssm), lambda b, i: (l, 0, col_blk)),
        ],
        out_specs=u_spec,
        compiler_params=_params(2, 32),
        name="uproj",
    )(x, mod4, norm_g, w_bf)


N_LEVELS = 7
POW_ROWS = 24


def _s5prep_body(kk_ref, lr_ref, li_ref, ldt_ref, bc1_ref, bc2_ref, ca_ref, cb_ref, dv_ref,
                 wcat_ref, mtot_ref, v_ref, a1_ref, a2_ref):
    t_ = CHUNK
    kk = kk_ref[...]
    lane_blk = jnp.right_shift(lax.broadcasted_iota(I32, (SSM_GROUP, CHUNK_W), 1), 4)
    lane = lax.broadcasted_iota(I32, (1, LANES), 1)
    sgn = jnp.where(lane < LANES // 2, -1.0, 1.0)
    eye = jnp.where(lax.broadcasted_iota(I32, (LANES, LANES), 0)
                    == lax.broadcasted_iota(I32, (LANES, LANES), 1), 1.0, 0.0)
    mtot = jnp.zeros((CHUNK_W, CHUNK_W), F32)
    for d in range(2):
        lr = lr_ref[d]
        li = li_ref[d]
        dt = jnp.exp(ldt_ref[d])
        pm = jnp.exp(kk * (dt * lr))
        ang = kk * (dt * li)
        pr = pm * jnp.cos(ang)
        pi = pm * jnp.sin(ang)
        ar = pr[1:2, :]
        ai = pi[1:2, :]
        den = lr * lr + li * li
        nr = ar - 1.0
        kr = (nr * lr + ai * li) / den
        ki = (ai * lr - nr * li) / den
        bc1 = bc1_ref[d]
        bc2 = bc2_ref[d]
        bb1 = kr * bc1 + ki * bc2
        bb2 = kr * bc2 - ki * bc1
        ca = ca_ref[d]
        cb = cb_ref[d]
        cak = [ca * pr[k:k + 1, :] + cb * pi[k:k + 1, :] for k in range(t_ + 1)]
        lag_order = range(t_) if d == 0 else range(t_ - 1, -1, -1)
        cak_all = jnp.concatenate([cak[k] for k in lag_order], axis=0)
        kall_t = _dot3(bb1, cak_all, _NT)
        rows = []
        for j in range(t_):
            if d == 0:
                shift, keep = (SSM_GROUP * j) % CHUNK_W, lane_blk >= j
            else:
                shift, keep = (SSM_GROUP * (j + 1)) % CHUNK_W, lane_blk <= j
            r = pltpu.roll(kall_t, shift, 1) if shift else kall_t
            rows.append(jnp.where(keep, r, 0.0))
        mtot = mtot + jnp.concatenate(rows, axis=0)
        e_v = [i + 1 for i in range(t_)] if d == 0 else [t_ - i for i in range(t_)]
        v_t = jnp.concatenate([cak[e] for e in e_v], axis=0)
        v_ref[d] = _dot3(eye, v_t, _NT).astype(v_ref.dtype)
        e_w = [t_ - 1 - j for j in range(t_)] if d == 0 else list(range(t_))
        w = jnp.concatenate([bb1 * pr[e:e + 1, :] + bb2 * pi[e:e + 1, :] for e in e_w], axis=0)
        wcat_ref[:, d * LANES:(d + 1) * LANES] = w.astype(wcat_ref.dtype)
        a1_ref[d] = pr[t_:t_ + 8, :]
        a2_ref[d] = sgn * pi[t_:t_ + 8, :]
    diag = (lax.broadcasted_iota(I32, (CHUNK_W, CHUNK_W), 0)
            == lax.broadcasted_iota(I32, (CHUNK_W, CHUNK_W), 1))
    mtot = mtot + jnp.where(diag, dv_ref[...], 0.0)
    mtot_ref[...] = mtot.astype(mtot_ref.dtype)


def _s5prep(l, kk, lr_t, li_t, ldt, bc1, bc2, ca, cb, dv):
    g = lr_t.shape[2]
    spec5 = lambda r, c: pl.BlockSpec((None, 2, None, r, c), lambda i: (l, 0, i, 0, 0))
    return pl.pallas_call(
        _s5prep_body,
        out_shape=(jax.ShapeDtypeStruct((g, CHUNK_W, 2 * LANES), BF16),
                   jax.ShapeDtypeStruct((g, CHUNK_W, CHUNK_W), BF16),
                   jax.ShapeDtypeStruct((2, g, LANES, CHUNK_W), BF16),
                   jax.ShapeDtypeStruct((2, g, 8, LANES), F32),
                   jax.ShapeDtypeStruct((2, g, 8, LANES), F32)),
        grid=(g,),
        in_specs=[pl.BlockSpec((POW_ROWS, 1), lambda i: (0, 0)),
                  spec5(1, LANES), spec5(1, LANES), spec5(1, 1),
                  spec5(SSM_GROUP, LANES), spec5(SSM_GROUP, LANES),
                  spec5(SSM_GROUP, LANES), spec5(SSM_GROUP, LANES),
                  pl.BlockSpec((None, None, 1, CHUNK_W), lambda i: (l, i, 0, 0))],
        out_specs=(pl.BlockSpec((None, CHUNK_W, 2 * LANES), lambda i: (i, 0, 0)),
                   pl.BlockSpec((None, CHUNK_W, CHUNK_W), lambda i: (i, 0, 0)),
                   pl.BlockSpec((2, None, LANES, CHUNK_W), lambda i: (0, i, 0, 0)),
                   pl.BlockSpec((2, None, 8, LANES), lambda i: (0, i, 0, 0)),
                   pl.BlockSpec((2, None, 8, LANES), lambda i: (0, i, 0, 0))),
        compiler_params=_params(1, 32),
        name="s5prep",
    )(kk, lr_t, li_t, ldt, bc1, bc2, ca, cb, dv)


def _s5mix_body(*refs, seg, with_h0):
    if with_h0:
        u_ref, wcat_ref, m_ref, v_ref, a1_ref, a2_ref, h0_ref, y_ref = refs
    else:
        u_ref, wcat_ref, m_ref, v_ref, a1_ref, a2_ref, y_ref, fin_ref, fs_ref = refs
    k4, n, _ = u_ref.shape
    c = n // CHUNK
    gpt = LANES // SSM_GROUP
    b = pl.program_id(0)
    rowm = jnp.bitwise_and(lax.broadcasted_iota(I32, (1, c, 1), 1), seg - 1)
    shifts = [1 << j for j in range(seg.bit_length() - 1)]

    def swap(x):
        return pltpu.roll(x, LANES // 2, 2)

    def scan(s, d, g0):
        a1 = a1_ref[d, g0:g0 + gpt]
        a2 = a2_ref[d, g0:g0 + gpt]
        if with_h0:
            h0 = h0_ref[d, g0:g0 + gpt, pl.ds(b, 1), :]
        else:
            h0 = 0.0
        if d == 0:
            x = jnp.where(rowm == 0, h0, pltpu.roll(s, 1, 1))
        else:
            x = jnp.where(rowm == seg - 1, h0, pltpu.roll(s, c - 1, 1))
        for j, sh in enumerate(shifts):
            if d == 0:
                xs = jnp.where(rowm >= sh, pltpu.roll(x, sh, 1), 0.0)
            else:
                xs = jnp.where(rowm < seg - sh, pltpu.roll(x, c - sh, 1), 0.0)
            x = x + a1[:, j:j + 1, :] * xs + a2[:, j:j + 1, :] * swap(xs)
        return x

    for k in range(k4):
        g0 = k * gpt
        xk = jnp.concatenate([u_ref[k, pl.ds(t, c, stride=CHUNK), :] for t in range(CHUNK)], axis=0)
        xt = xk.T
        zs, ss = [], []
        for gg in range(gpt):
            rg = jnp.concatenate([xt[gg * SSM_GROUP:(gg + 1) * SSM_GROUP, t * c:(t + 1) * c]
                                  for t in range(CHUNK)], axis=0)
            z = rg.T.astype(BF16)
            zs.append(z)
            ss.append(_dot(z, wcat_ref[g0 + gg]))
        hins = []
        for d in range(2):
            sd = jnp.stack([s[:, d * LANES:(d + 1) * LANES] for s in ss], axis=0)
            hin = scan(sd, d, g0)
            hins.append(hin)
            if not with_h0:
                a1 = a1_ref[d, g0:g0 + gpt]
                a2 = a2_ref[d, g0:g0 + gpt]
                fin = a1[:, 0:1, :] * hin + a2[:, 0:1, :] * swap(hin) + sd
                first = seg - 1 if d == 0 else 0
                for gg in range(gpt):
                    fs_ref[...] = fin[gg]
                    fin_ref[d, g0 + gg] = fs_ref[pl.ds(first, c // seg, stride=seg), :]
        yts = []
        for gg in range(gpt):
            lhs = jnp.concatenate([zs[gg], hins[0][gg].astype(BF16), hins[1][gg].astype(BF16)], axis=1)
            rhs = jnp.concatenate([m_ref[g0 + gg], v_ref[0, g0 + gg], v_ref[1, g0 + gg]], axis=0)
            yts.append(_dot(lhs, rhs).T)
        xto = jnp.concatenate(
            [jnp.concatenate([yt[t * SSM_GROUP:(t + 1) * SSM_GROUP, :] for yt in yts], axis=0)
             for t in range(CHUNK)], axis=1)
        xo = xto.T
        for t in range(CHUNK):
            y_ref[k, pl.ds(t, c, stride=CHUNK), :] = xo[t * c:(t + 1) * c, :]


def _s5mix(u4, wcat, mtot, v, a1, a2, *, seg, h0=None):
    nb, k4, n, _ = u4.shape
    g = wcat.shape[0]
    c = n // CHUNK
    with_h0 = h0 is not None
    body = functools.partial(_s5mix_body, seg=seg, with_h0=with_h0)
    tile = pl.BlockSpec((None, k4, n, LANES), lambda b: (b, 0, 0, 0))
    res = lambda shape: pl.BlockSpec(shape, lambda b: (0,) * len(shape), pipeline_mode=pl.Buffered(1))
    in_specs = [tile, res((g, CHUNK_W, 2 * LANES)), res((g, CHUNK_W, CHUNK_W)), res((2, g, LANES, CHUNK_W)),
                res((2, g, 8, LANES)), res((2, g, 8, LANES))]
    args = [u4, wcat, mtot, v, a1, a2]
    y_shape = jax.ShapeDtypeStruct((nb, k4, n, LANES), F32)
    if with_h0:
        in_specs.append(res((2, g, h0.shape[2], LANES)))
        args.append(h0)
        out_shape, out_specs, scratch = y_shape, tile, []
    else:
        out_shape = (y_shape, jax.ShapeDtypeStruct((2, g, c // seg, LANES), F32))
        out_specs = (tile, pl.BlockSpec((2, g, c // seg, LANES), lambda b: (0, 0, 0, 0)))
        scratch = [pltpu.VMEM((c, LANES), F32)]
    return pl.pallas_call(
        body,
        out_shape=out_shape,
        grid=(nb,),
        in_specs=in_specs,
        out_specs=out_specs,
        scratch_shapes=scratch,
        compiler_params=_params(1, 48),
        name="s5mix",
    )(*args)


def _postmix_body(*refs, d_conv, aliased, nb_real, padded):
    if aliased:
        refs = refs[1:]
    if padded:
        @pl.when(pl.program_id(0) < nb_real)
        def _():
            _postmix_tile(*refs, d_conv=d_conv)

        @pl.when(pl.program_id(0) >= nb_real)
        def _():
            refs[-2][...] = jnp.zeros_like(refs[-2])
    else:
        _postmix_tile(*refs, d_conv=d_conv)


def _postmix_tile(cn_ref, y_ref, x_ref, mod_ref, wglu_ref, gs_ref, wout_ref, g2_ref, r2_ref, r1_ref,
                  xo_ref, h2_ref, lg_ref, *, d_conv):
    yg = jax.nn.gelu(jnp.concatenate([y_ref[k] for k in range(y_ref.shape[0])], axis=1))
    z = _dot(yg.astype(BF16), wglu_ref[...])
    s = yg * jax.nn.sigmoid(z)
    d_ssm = s.shape[-1]
    sn = [(_rms(s[:, lo:lo + HEAD_DIM]) * gs_ref[:, lo:lo + HEAD_DIM]).astype(BF16)
          for lo in range(0, d_ssm, HEAD_DIM)]
    mix = _dot(cn_ref[...], wout_ref[0:d_conv, :]) + _dot(jnp.concatenate(sn, axis=1), wout_ref[d_conv:, :])
    xn = x_ref[...] + mod_ref[2:3, :] * mix
    xo_ref[...] = xn
    h2 = _rms(xn) * g2_ref[...]
    h2 = h2 * (1.0 + mod_ref[4:5, :]) + mod_ref[3:4, :]
    hi, lo_ = _split_bf16(h2)
    tm, d = h2.shape
    for k in range(d // LANES):
        h2_ref[pl.ds(k, tm, stride=d // LANES), :] = h2[:, k * LANES:(k + 1) * LANES]
    d1 = _dot(hi, r2_ref[...])
    d2 = _dot(lo_, r1_ref[...])
    lg_ref[...] = d1[:, 0:LANES] + d1[:, LANES:2 * LANES] + d2


def _postmix(conv_n, y4, x, mod4, l, mod_row, wglu_bf, gs, wout_bf, norm2_g, r2, r1, *, tm, pseudo,
             table_tokens, table=None, tok0=0, pad_samples=0):
    nb, n, d = x.shape
    d_conv = conv_n.shape[-1]
    d_ssm = y4.shape[1] * LANES
    tok_rows = d // LANES
    aliased = table is not None
    body = functools.partial(_postmix_body, d_conv=d_conv, aliased=aliased, nb_real=nb, padded=pad_samples > 0)

    def clamp(b, i):
        return jnp.minimum(b, nb - 1), jnp.where(b < nb, i, n // tm - 1)

    def tok_map(b, i):
        b, i = clamp(b, i)
        return (b, i, 0)

    _, y_spec = _u_tiles(nb, n, d_ssm, tm, pseudo, clamp)
    tok = lambda w: pl.BlockSpec((None, tm, w), tok_map)
    lay = lambda r, c, **kw: pl.BlockSpec((None, r, c), lambda b, i: (l, 0, 0), **kw)
    in_specs = [tok(d_conv), y_spec, tok(d),
                pl.BlockSpec((None, None, N_MOD, d), lambda b, i: (l, mod_row(jnp.minimum(b, nb - 1)), 0, 0)),
                lay(d_ssm, d_ssm, pipeline_mode=pl.Buffered(1)),
                lay(1, d_ssm),
                lay(d_conv + d_ssm, d, pipeline_mode=pl.Buffered(1)),
                lay(1, d),
                lay(d, 2 * LANES, pipeline_mode=pl.Buffered(1)),
                lay(d, LANES, pipeline_mode=pl.Buffered(1))]
    args = [conv_n, y4, x, mod4, wglu_bf, gs, wout_bf, norm2_g, r2, r1]
    if aliased:
        in_specs.insert(0, pl.BlockSpec(memory_space=pl.ANY))
        args.insert(0, table)
    blk0 = tok0 // tm
    return pl.pallas_call(
        body,
        out_shape=(jax.ShapeDtypeStruct((nb, n, d), F32),
                   jax.ShapeDtypeStruct((table_tokens * tok_rows, LANES), F32),
                   jax.ShapeDtypeStruct((nb, n, LANES), F32)),
        grid=(nb + pad_samples, n // tm),
        in_specs=in_specs,
        out_specs=(tok(d),
                   pl.BlockSpec((tm * tok_rows, LANES), lambda b, i: (blk0 + b * (n // tm) + i, 0)),
                   tok(LANES)),
        input_output_aliases={0: 1} if aliased else {},
        compiler_params=_params(2, 56),
        name="postmix",
    )(*args)


BISECT_STEPS = 32

def _route_body(lg_ref, pos_ref, wv_ref, cum_ref, tri_ref, *, cap):
    nb, ne, n = lg_ref.shape
    rc = min(256, n)
    for r0 in range(0, n, rc):
        ri = lax.broadcasted_iota(I32, (rc, n), 0) + r0
        ci = lax.broadcasted_iota(I32, (rc, n), 1)
        tri_ref[r0:r0 + rc, :] = jnp.where(ri < ci, 1.0, 0.0).astype(BF16)
    lg = lg_ref[...]
    e = jnp.exp(lg - jnp.max(lg, axis=1, keepdims=True))
    aff = e / jnp.sum(e, axis=1, keepdims=True)
    capf = float(cap)

    def enough(t):
        return jnp.sum(jnp.where(aff >= t, 1.0, 0.0), axis=2, keepdims=True) >= capf

    hi = jnp.full((nb, ne, 1), 2.0, F32)
    for j in range(6, -1, -1):
        cand = hi * (2.0 ** -(1 << j))
        hi = jnp.where(enough(cand), hi, cand)
    half = hi * 0.5
    lo = jnp.where(enough(half), half, 0.0)
    for _ in range(BISECT_STEPS):
        mid = (lo + hi) * 0.5
        ok = enough(mid)
        lo = jnp.where(ok, mid, lo)
        hi = jnp.where(ok, hi, mid)
    gt = aff >= hi
    eq = jnp.logical_and(aff >= lo, aff < hi)
    need = capf - jnp.sum(jnp.where(gt, 1.0, 0.0), axis=2, keepdims=True)
    tri = tri_ref[...]
    eq_rank = _dot(jnp.where(eq, 1.0, 0.0).astype(BF16).reshape(nb * ne, n), tri).reshape(nb, ne, n)
    sel = jnp.logical_or(gt, jnp.logical_and(eq, eq_rank < need))
    pos = _dot(jnp.where(sel, 1.0, 0.0).astype(BF16).reshape(nb * ne, n), tri).reshape(nb, ne, n)
    pos_ref[...] = jnp.where(sel, pos.astype(I32), -1)
    wv_ref[...] = jnp.where(sel, aff, 0.0)
    cum_ref[...] = pos.astype(I32)


def _route(logits_t, cap):
    nb, ne, n = logits_t.shape
    body = functools.partial(_route_body, cap=cap)
    full = pl.BlockSpec((nb, ne, n), lambda i: (0, 0, 0))
    return pl.pallas_call(
        body,
        out_shape=(jax.ShapeDtypeStruct((nb, ne, n), I32),
                   jax.ShapeDtypeStruct((nb, ne, n), F32),
                   jax.ShapeDtypeStruct((nb, ne, n), I32)),
        grid=(1,),
        in_specs=[full],
        out_specs=(full, full, full),
        scratch_shapes=[pltpu.VMEM((n, n), BF16)],
        compiler_params=_params(1, 48),
        name="route",
    )(logits_t)


def _slotidx_body(st_ref, pos_ref, o_ref, *, cap, tt):
    n, ne = pos_ref.shape
    nt = n // tt
    win = COMBINE_WIN
    base = pl.program_id(0) * ne * (nt + 1)
    pos = pos_ref[...]

    w0s, fits = {}, None
    for e in range(ne):
        for i in range(nt):
            s0 = st_ref[base + e * (nt + 1) + i]
            s1 = st_ref[base + e * (nt + 1) + i + 1]
            w0 = jnp.minimum(s0, cap - win)
            ok = s1 - w0 <= win
            fits = ok if fits is None else jnp.logical_and(fits, ok)
            w0s[e, i] = w0

    @pl.when(fits)
    def _():
        lane = lax.broadcasted_iota(I32, (tt, win), 1)
        tok0 = lax.broadcasted_iota(I32, (tt, 1), 0).astype(F32)
        pad = jnp.zeros((1, cap - win), F32)
        for e in range(ne):
            acc = jnp.zeros((1, cap), F32)
            for i in range(nt):
                hit = (lane + w0s[e, i]) == pos[i * tt:(i + 1) * tt, e:e + 1]
                part = jnp.sum(jnp.where(hit, tok0 + float(i * tt), 0.0), axis=0, keepdims=True)
                acc = acc + pltpu.roll(jnp.concatenate([part, pad], axis=1), w0s[e, i], 1)
            o_ref[e:e + 1, :] = acc.astype(I32)

    @pl.when(jnp.logical_not(fits))
    def _():
        tok = lax.broadcasted_iota(I32, (n, 1), 0).astype(F32)
        slot = lax.broadcasted_iota(I32, (n, cap), 1)
        for e in range(ne):
            hit = jnp.where(slot == pos[:, e:e + 1], tok, 0.0)
            o_ref[e:e + 1, :] = jnp.sum(hit, axis=0, keepdims=True).astype(I32)


def _slotidx(starts, pos_cols, cap, tt):
    nb, n, ne = pos_cols.shape
    grid_spec = pltpu.PrefetchScalarGridSpec(
        num_scalar_prefetch=1,
        grid=(nb,),
        in_specs=[pl.BlockSpec((None, n, ne), lambda b, st: (b, 0, 0))],
        out_specs=pl.BlockSpec((None, ne, cap), lambda b, st: (b, 0, 0)))
    return pl.pallas_call(
        functools.partial(_slotidx_body, cap=cap, tt=tt),
        out_shape=jax.ShapeDtypeStruct((nb, ne, cap), I32),
        grid_spec=grid_spec,
        compiler_params=_params(1, 32),
        name="slotidx",
    )(starts, pos_cols)


def _ffn_body(idx_ref, tab_ref, wg_ref, wu_ref, wd_ref, y_ref, xbuf_ref, xs_ref, acc_ref, sem, *, tok_rows, nf):
    f = pl.program_id(2)
    blk = pl.program_id(0) * pl.num_programs(1) + pl.program_id(1)
    nblk = pl.num_programs(0) * pl.num_programs(1)
    step = blk * nf + f
    mh = xs_ref.shape[1]
    ch = mh // nf
    cur = jnp.bitwise_and(blk, 1)
    ring = jnp.bitwise_and(step, 1)

    def token_copy(blk_i, c, j, ring_i):
        tok = idx_ref[blk_i * mh + c * ch + j]
        src = tab_ref.at[pl.ds(pl.multiple_of(tok * tok_rows, tok_rows), tok_rows), :]
        dst = xbuf_ref.at[ring_i, pl.ds(pl.multiple_of(j * tok_rows, tok_rows), tok_rows), :]
        return pltpu.make_async_copy(src, dst, sem.at[ring_i])

    def wait_chunk(ring_i):
        pltpu.make_async_copy(tab_ref.at[pl.ds(0, ch * tok_rows), :], xbuf_ref.at[ring_i], sem.at[ring_i]).wait()

    def convert(ring_i, xs_slot, c):
        rows = pl.ds(pl.multiple_of(c * ch, ch), ch)
        for k in range(tok_rows):
            w = xbuf_ref[ring_i, pl.ds(k, ch, stride=tok_rows), :]
            xs_ref[xs_slot, rows, k * LANES:(k + 1) * LANES] = w.astype(BF16)

    def issue_loop(blk_i, c, ring_i):
        def issue(j, carry):
            token_copy(blk_i, c, j, ring_i).start()
            return carry
        lax.fori_loop(0, ch, issue, 0)

    @pl.when(step == 0)
    def _():
        for c in range(nf):
            issue_loop(0, c, 0)
            wait_chunk(0)
            convert(0, 0, c)
        issue_loop(1 % nblk, 0, 1)

    @pl.when(f == 0)
    def _():
        acc_ref[...] = jnp.zeros_like(acc_ref)

    q = step + nf + 1
    qb = q // nf
    qb = jnp.where(qb < nblk, qb, 0)
    for j in range(ch):
        token_copy(qb, q % nf, j, ring).start(priority=j % 2)

    xs = xs_ref[cur]
    g = _dot(xs, wg_ref[...].astype(BF16))
    u = _dot(xs, wu_ref[...].astype(BF16))
    act = (jax.nn.silu(g) * u).astype(BF16)
    acc_ref[...] += _dot(act, wd_ref[...].astype(BF16))

    wait_chunk(1 - ring)
    convert(1 - ring, 1 - cur, f)

    @pl.when(f == nf - 1)
    def _():
        y_ref[...] = acc_ref[...].astype(y_ref.dtype)

    @pl.when(step == nblk * nf - 1)
    def _():
        wait_chunk(ring)


def _ffn(slot_tok, table, l, w_gate, w_up, w_down, *, tf, m_split):
    _, ne, d, dff = w_gate.shape
    tok_rows = d // LANES
    m = slot_tok.shape[0] // ne
    mh = m // m_split
    nf = dff // tf
    assert mh % nf == 0 and table.shape[1] == LANES
    body = functools.partial(_ffn_body, tok_rows=tok_rows, nf=nf)
    grid_spec = pltpu.PrefetchScalarGridSpec(
        num_scalar_prefetch=1,
        grid=(m_split, ne, nf),
        in_specs=[pl.BlockSpec(memory_space=pl.ANY),
                  pl.BlockSpec((None, None, d, tf), lambda h, e, f, idx: (l, e, 0, f)),
                  pl.BlockSpec((None, None, d, tf), lambda h, e, f, idx: (l, e, 0, f)),
                  pl.BlockSpec((None, None, tf, d), lambda h, e, f, idx: (l, e, f, 0))],
        out_specs=pl.BlockSpec((None, mh, d), lambda h, e, f, idx: (e, h, 0)),
        scratch_shapes=[pltpu.VMEM((2, mh // nf * tok_rows, LANES), F32),
                        pltpu.VMEM((2, mh, d), BF16),
                        pltpu.VMEM((mh, d), F32),
                        pltpu.SemaphoreType.DMA((2,))])
    return pl.pallas_call(
        body,
        out_shape=jax.ShapeDtypeStruct((ne, m, d), BF16),
        grid_spec=grid_spec,
        compiler_params=_params(3, 56),
        name="ffn",
    )(slot_tok, table, w_gate, w_up, w_down)


COMBINE_WIN = 64


def _combine_body(*refs, cap, final):
    if final:
        st_ref, y_ref, pos_ref, wv_ref, x_ref, mod_ref, gf_ref, o_ref = refs
    else:
        st_ref, y_ref, pos_ref, wv_ref, x_ref, mod_ref, o_ref = refs
    ne = y_ref.shape[0]
    tt = x_ref.shape[0]
    win = COMBINE_WIN
    nt1 = pl.num_programs(1) + 1
    base = pl.program_id(0) * ne * nt1 + pl.program_id(1)
    pos = pos_ref[...]
    wv = wv_ref[...]

    def finish(moe):
        xn = x_ref[...] + mod_ref[5:6, :] * moe
        if final:
            xn = _rms(xn) * gf_ref[...]
        o_ref[...] = xn

    w0s = []
    fits = None
    for e in range(ne):
        s0 = st_ref[base + e * nt1]
        s1 = st_ref[base + e * nt1 + 1]
        w0 = jnp.minimum(jnp.left_shift(jnp.right_shift(s0, 4), 4), cap - win)
        ok = s1 - w0 <= win
        fits = ok if fits is None else jnp.logical_and(fits, ok)
        w0s.append(pl.multiple_of(w0, 16))

    @pl.when(fits)
    def _():
        lane = lax.broadcasted_iota(I32, (tt, LANES), 1)
        low = lane < win
        pieces, rows = [], []
        for e in range(0, ne, 2):
            tgt = jnp.where(low, lane + w0s[e], lane + (w0s[e + 1] - win))
            p = jnp.where(low, pos[:, e:e + 1], pos[:, e + 1:e + 2])
            w = jnp.where(low, wv[:, e:e + 1], wv[:, e + 1:e + 2])
            pieces.append(jnp.where(tgt == p, w, 0.0).astype(BF16))
            rows.append(y_ref[e, pl.ds(w0s[e], win), :])
            rows.append(y_ref[e + 1, pl.ds(w0s[e + 1], win), :])
        finish(_dot(jnp.concatenate(pieces, axis=1), jnp.concatenate(rows, axis=0)))

    @pl.when(jnp.logical_not(fits))
    def _():
        slot = lax.broadcasted_iota(I32, (tt, cap), 1)
        pieces = [jnp.where(slot == pos[:, e:e + 1], wv[:, e:e + 1], 0.0).astype(BF16) for e in range(ne)]
        finish(_dot(jnp.concatenate(pieces, axis=1), y_ref[...].reshape(ne * cap, y_ref.shape[-1])))


def _combine(starts, y, blk0, pos_cols, wv_cols, x, mod4, l, mod_row, cap, final_g=None, *, tt):
    nb, n, d = x.shape
    ne = y.shape[0]
    final = final_g is not None
    assert 2 * COMBINE_WIN == LANES and ne % 2 == 0 and cap % 16 == 0 and cap >= COMBINE_WIN
    body = functools.partial(_combine_body, cap=cap, final=final)
    in_specs = [pl.BlockSpec((ne, cap, d), lambda b, i, st: (0, blk0 + b, 0)),
                pl.BlockSpec((None, tt, ne), lambda b, i, st: (b, i, 0)),
                pl.BlockSpec((None, tt, ne), lambda b, i, st: (b, i, 0)),
                pl.BlockSpec((None, tt, d), lambda b, i, st: (b, i, 0)),
                pl.BlockSpec((None, None, N_MOD, d), lambda b, i, st: (l, mod_row(b), 0, 0))]
    args = [starts, y, pos_cols, wv_cols, x, mod4]
    if final:
        in_specs.append(pl.BlockSpec((1, d), lambda b, i, st: (0, 0)))
        args.append(final_g)
    grid_spec = pltpu.PrefetchScalarGridSpec(
        num_scalar_prefetch=1,
        grid=(nb, n // tt),
        in_specs=in_specs,
        out_specs=pl.BlockSpec((None, tt, d), lambda b, i, st: (b, i, 0)))
    return pl.pallas_call(
        body,
        out_shape=jax.ShapeDtypeStruct((nb, n, d), F32),
        grid_spec=grid_spec,
        compiler_params=_params(2, 56),
        name="combine",
    )(*args)


def kernel(x, c, ctx, c_ctx, ada_w, ada_b, norm1_g, w_in, conv_w, ssm_lam_re, ssm_lam_im, ssm_log_dt,
           ssm_b_re, ssm_b_im, ssm_c_re, ssm_c_im, ssm_d, ssm_w_glu, out_norm_conv_g, out_norm_ssm_g,
           w_out, norm2_g, router_w, exp_w_gate, exp_w_up, exp_w_down, final_norm_g):
    bsz, n, d = x.shape
    nc = ctx.shape[1]
    depth = ada_w.shape[0]
    d_conv = conv_w.shape[-1]
    d_ssm = ssm_d.shape[-1]
    g = d_ssm // SSM_GROUP
    p = ssm_lam_re.shape[-1]
    ne = router_w.shape[-1]
    dff = exp_w_gate.shape[-1]
    cap = EC_FACTOR * n // ne
    cap_c = EC_FACTOR * nc // ne
    nbc = bsz * nc
    assert bsz == 8 and 2 * p == LANES and ssm_b_re.shape[-1] == SSM_GROUP
    assert nbc == n and bsz * cap_c == cap
    assert n // CHUNK == 1 << N_LEVELS and nc % CHUNK == 0 and (nc // CHUNK) & (nc // CHUNK - 1) == 0
    assert d_conv % HEAD_DIM == 0 and d_ssm % HEAD_DIM == 0 and (3 * d_conv) % d_ssm == 0
    assert N_MOD * d == ada_w.shape[-1] and ne <= LANES

    tm = min(512, n)
    tt = min(256, n)
    tf = min(256, dff)

    rows = 16
    cvec = jnp.zeros((rows, d), F32).at[:bsz].set(c).at[bsz].set(c_ctx)
    mod4 = _ada(cvec, ada_w, ada_b).reshape(depth, rows, N_MOD, d)
    lat_row = lambda b: b
    ctx_row = lambda b: bsz

    w_in_bf = w_in.astype(BF16)
    w_out_bf = w_out.astype(BF16)
    w_glu_bf = ssm_w_glu.astype(BF16)
    rw_hi, rw_lo = _split_bf16(router_w)
    zpad = jnp.zeros((depth, d, LANES - ne), BF16)
    r1 = jnp.concatenate([rw_hi, zpad], axis=-1)
    r2 = jnp.concatenate([rw_hi, zpad, rw_lo, zpad], axis=-1)

    tile2 = lambda a: jnp.concatenate([a, a], axis=-1)
    lr_t = tile2(ssm_lam_re)[:, :, :, None, :]
    li_t = tile2(ssm_lam_im)[:, :, :, None, :]
    ldt = ssm_log_dt[:, :, :, None, None]
    brt = jnp.swapaxes(ssm_b_re, -1, -2)
    bit = jnp.swapaxes(ssm_b_im, -1, -2)
    bc1 = jnp.concatenate([brt, bit], axis=-1)
    bc2 = jnp.concatenate([-bit, brt], axis=-1)
    ca = jnp.concatenate([ssm_c_re, -ssm_c_im], axis=-1)
    cb = jnp.concatenate([-ssm_c_im, -ssm_c_re], axis=-1)
    dv = jnp.tile(ssm_d.reshape(depth, g, 1, SSM_GROUP), (1, 1, 1, CHUNK))

    g1n = norm1_g.reshape(depth, 1, d)
    g2n = norm2_g.reshape(depth, 1, d)
    gcn = out_norm_conv_g.reshape(depth, 1, d_conv)
    gsn = out_norm_ssm_g.reshape(depth, 1, d_ssm)
    gfin = final_norm_g.reshape(1, d)

    kk = jnp.asarray([float(k) for k in range(CHUNK + 1)]
                     + [float(CHUNK << j) for j in range(1, N_LEVELS)] + [0.0], F32).reshape(POW_ROWS, 1)

    xl = x
    xc = ctx
    for l in range(depth):
        last = l == depth - 1
        conv_l, u_lat = _inproj(xl, mod4, l, lat_row, g1n, w_in_bf, conv_w, gcn, rowlen=GRID_W, tm=tm,
                                pseudo=False)
        if last:
            u_ctx = _uproj(xc, mod4, l, ctx_row, g1n, w_in_bf, d_ssm, tm=nc)
        else:
            conv_c, u_ctx = _inproj(xc, mod4, l, ctx_row, g1n, w_in_bf, conv_w, gcn, rowlen=nc, tm=nc,
                                    pseudo=True)

        wcat, mtot, vmat, a1, a2 = _s5prep(l, kk, lr_t, li_t, ldt, bc1, bc2, ca, cb, dv)
        y_ctx, h0 = _s5mix(u_ctx, wcat, mtot, vmat, a1, a2, seg=nc // CHUNK)
        y_lat = _s5mix(u_lat, wcat, mtot, vmat, a1, a2, seg=n // CHUNK, h0=h0)

        n_tab = bsz * n + (0 if last else nbc)
        xl, table, lg = _postmix(conv_l, y_lat, xl, mod4, l, lat_row, w_glu_bf, gsn, w_out_bf, g2n, r2, r1,
                                 tm=tm, pseudo=False, table_tokens=n_tab, pad_samples=0 if last else nbc // n)
        lg_t = jnp.swapaxes(lg[:, :, :ne], 1, 2)
        pos, wv, cum = _route(lg_t, cap)
        capcol = jnp.full((bsz, ne, 1), cap, I32)
        starts = jnp.concatenate([cum[:, :, ::tt], capcol], axis=2).reshape(-1)
        pos_cols = jnp.swapaxes(pos, 1, 2)
        wv_cols = jnp.swapaxes(wv, 1, 2)
        offs_n = (jnp.arange(bsz, dtype=I32) * n)[:, None, None]
        slot_tok = jnp.swapaxes(_slotidx(starts, pos_cols, cap, tt) + offs_n, 0, 1).reshape(ne, bsz * cap)

        if not last:
            xc, table, lgc = _postmix(conv_c, y_ctx, xc, mod4, l, ctx_row, w_glu_bf, gsn, w_out_bf, g2n,
                                      r2, r1, tm=nc, pseudo=True, table_tokens=n_tab, table=table, tok0=bsz * n)
            lgc_t = jnp.swapaxes(lgc[:, :, :ne], 1, 2)
            posc, wvc, cumc = _route(lgc_t, cap_c)
            offs = (jnp.arange(bsz, dtype=I32) * cap_c)[:, None, None]
            posc = jnp.where(posc >= 0, posc + offs, -1)
            posc_cols = jnp.swapaxes(posc, 1, 2).reshape(1, nbc, ne)
            wvc_cols = jnp.swapaxes(wvc, 1, 2).reshape(1, nbc, ne)
            startsc = jnp.swapaxes(cumc[:, :, ::tt] + offs, 0, 1).reshape(ne, -1)
            startsc = jnp.concatenate([startsc, jnp.full((ne, 1), cap, I32)], axis=1).reshape(-1)
            slot_tok = jnp.concatenate([slot_tok, _slotidx(startsc, posc_cols, cap, tt)[0] + bsz * n], axis=1)

        m_split = 2
        mh = slot_tok.shape[1] // m_split
        slot_tok = jnp.swapaxes(slot_tok.reshape(ne, m_split, mh), 0, 1).reshape(-1)
        y_exp = _ffn(slot_tok, table, l, exp_w_gate, exp_w_up, exp_w_down, tf=tf, m_split=m_split)
        xl = _combine(starts, y_exp, 0, pos_cols, wv_cols, xl, mod4, l, lat_row, cap,
                      gfin if last else None, tt=tt)
        if not last:
            xc = _combine(startsc, y_exp, bsz, posc_cols, wvc_cols, xc.reshape(1, nbc, d), mod4, l, ctx_row, cap,
                          tt=tt).reshape(bsz, nc, d)
    return xl
```

```python
import functools

import jax
import jax.numpy as jnp
from jax import lax
from jax.experimental import pallas as pl
from jax.experimental.pallas import tpu as pltpu

F32 = jnp.float32
BF16 = jnp.bfloat16
I32 = jnp.int32

EPS = 1e-6
GRID_W = 64
HEAD_DIM = 128
SSM_GROUP = 16
EC_FACTOR = 2
N_MOD = 6
CHUNK = 16
LANES = 128
CHUNK_W = CHUNK * SSM_GROUP

_MIB = 1 << 20


def _params(n_axes, vmem_mib):
    return pltpu.CompilerParams(dimension_semantics=("arbitrary",) * n_axes,
                                vmem_limit_bytes=vmem_mib * _MIB)


def _split_bf16(a):
    hi = a.astype(BF16)
    lo = (a - hi.astype(F32)).astype(BF16)
    return hi, lo


_NN = (((1,), (0,)), ((), ()))
_NT = (((1,), (1,)), ((), ()))


def _dot(a, b, dims=_NN):
    return lax.dot_general(a, b, dims, preferred_element_type=F32)


def _dot3(a, b, dims=_NN):
    ah, al = _split_bf16(a)
    bh, bl = _split_bf16(b)
    return _dot(ah, bh, dims) + _dot(ah, bl, dims) + _dot(al, bh, dims)


def _rms(x):
    return x * lax.rsqrt(jnp.mean(x * x, axis=-1, keepdims=True) + EPS)


def _ada_body(c_ref, w_ref, b_ref, o_ref):
    s = jax.nn.silu(c_ref[...])
    o_ref[...] = _dot3(s, w_ref[...]) + b_ref[...]


def _ada(cvec, ada_w, ada_b):
    depth, d, n6 = ada_w.shape
    tn = next(t for t in (1024, 512, 256, 128) if n6 % t == 0)
    rows = cvec.shape[0]
    return pl.pallas_call(
        _ada_body,
        out_shape=jax.ShapeDtypeStruct((depth, rows, n6), F32),
        grid=(depth, n6 // tn),
        in_specs=[
            pl.BlockSpec((rows, d), lambda l, j: (0, 0)),
            pl.BlockSpec((None, d, tn), lambda l, j: (l, 0, j)),
            pl.BlockSpec((None, 1, tn), lambda l, j: (l, 0, j)),
        ],
        out_specs=pl.BlockSpec((None, rows, tn), lambda l, j: (l, 0, j)),
        compiler_params=_params(2, 40),
        name="ada",
    )(cvec, ada_w, ada_b.reshape(depth, 1, n6))


def _store_lane_tiles(u_ref, u):
    for k in range(u_ref.shape[0]):
        u_ref[k] = u[:, k * LANES:(k + 1) * LANES]


def _u_tiles(nb, n, d_ssm, tm, pseudo, clamp=lambda b, i: (b, i)):
    k4 = d_ssm // LANES

    def pseudo_map(b, i):
        b, i = clamp(b, i)
        return (0, 0, b * (n // tm) + i, 0)

    def tile_map(b, i):
        b, i = clamp(b, i)
        return (b, 0, i, 0)

    if pseudo:
        return (jax.ShapeDtypeStruct((1, k4, nb * n, LANES), F32), pl.BlockSpec((None, k4, tm, LANES), pseudo_map))
    return (jax.ShapeDtypeStruct((nb, k4, n, LANES), F32), pl.BlockSpec((None, k4, tm, LANES), tile_map))


def _inproj_body(x_ref, mod_ref, g_ref, w_ref, cw_ref, gc_ref, conv_ref, u_ref, *, rowlen, d_conv, cn):
    x = x_ref[...]
    tm = x.shape[0]
    h = _rms(x) * g_ref[...]
    h = h * (1.0 + mod_ref[1:2, :]) + mod_ref[0:1, :]
    hb = h.astype(BF16)
    t = jnp.bitwise_and(lax.broadcasted_iota(I32, (tm, 1), 0), rowlen - 1)
    first = t == 0
    last = t == rowlen - 1
    for j in range(d_conv // cn):
        c0 = j * cn
        bg = _dot(hb, w_ref[:, c0:c0 + cn])
        cg = _dot(hb, w_ref[:, d_conv + c0:d_conv + c0 + cn])
        v = _dot(hb, w_ref[:, 2 * d_conv + c0:2 * d_conv + c0 + cn])
        z = cg * v
        zp = jnp.where(first, 0.0, pltpu.roll(z, 1, 0))
        zn = jnp.where(last, 0.0, pltpu.roll(z, tm - 1, 0))
        cw = cw_ref[:, c0:c0 + cn]
        y = bg * (cw[0:1, :] * zp + cw[1:2, :] * z + cw[2:3, :] * zn)
        for hd in range(cn // HEAD_DIM):
            lo = hd * HEAD_DIM
            yh = _rms(y[:, lo:lo + HEAD_DIM]) * gc_ref[:, c0 + lo:c0 + lo + HEAD_DIM]
            conv_ref[:, c0 + lo:c0 + lo + HEAD_DIM] = yh.astype(conv_ref.dtype)
    _store_lane_tiles(u_ref, _dot(hb, w_ref[:, 3 * d_conv:]))


def _inproj(x, mod4, l, mod_row, norm_g, w_bf, conv_w, gc, *, rowlen, tm, pseudo):
    nb, n, d = x.shape
    depth, _, d_in = w_bf.shape
    d_conv = conv_w.shape[-1]
    d_ssm = d_in - 3 * d_conv
    cn = min(256, d_conv)
    body = functools.partial(_inproj_body, rowlen=rowlen, d_conv=d_conv, cn=cn)
    u_shape, u_spec = _u_tiles(nb, n, d_ssm, tm, pseudo)
    return pl.pallas_call(
        body,
        out_shape=(jax.ShapeDtypeStruct((nb, n, d_conv), BF16), u_shape),
        grid=(nb, n // tm),
        in_specs=[
            pl.BlockSpec((None, tm, d), lambda b, i: (b, i, 0)),
            pl.BlockSpec((None, None, N_MOD, d), lambda b, i: (l, mod_row(b), 0, 0)),
            pl.BlockSpec((None, 1, d), lambda b, i: (l, 0, 0)),
            pl.BlockSpec((None, d, d_in), lambda b, i: (l, 0, 0), pipeline_mode=pl.Buffered(1)),
            pl.BlockSpec((None, 3, d_conv), lambda b, i: (l, 0, 0)),
            pl.BlockSpec((None, 1, d_conv), lambda b, i: (l, 0, 0)),
        ],
        out_specs=(pl.BlockSpec((None, tm, d_conv), lambda b, i: (b, i, 0)), u_spec),
        compiler_params=_params(2, 56),
        name="inproj",
    )(x, mod4, norm_g, w_bf, conv_w, gc)


def _uproj_body(x_ref, mod_ref, g_ref, w_ref, u_ref):
    h = _rms(x_ref[...]) * g_ref[...]
    h = h * (1.0 + mod_ref[1:2, :]) + mod_ref[0:1, :]
    _store_lane_tiles(u_ref, _dot(h.astype(BF16), w_ref[...]))


def _uproj(x, mod4, l, mod_row, norm_g, w_bf, d_ssm, *, tm):
    nb, n, d = x.shape
    d_in = w_bf.shape[-1]
    col_blk = (d_in - d_ssm) // d_ssm
    u_shape, u_spec = _u_tiles(nb, n, d_ssm, tm, True)
    return pl.pallas_call(
        _uproj_body,
        out_shape=u_shape,
        grid=(nb, n // tm),
        in_specs=[
            pl.BlockSpec((None, tm, d), lambda b, i: (b, i, 0)),
            pl.BlockSpec((None, None, N_MOD, d), lambda b, i: (l, mod_row(b), 0, 0)),
            pl.BlockSpec((None, 1, d), lambda b, i: (l, 0, 0)),
            pl.BlockSpec((None, d, d_ssm), lambda b, i: (l, 0, col_blk)),
        ],
        out_specs=u_spec,
        compiler_params=_params(2, 32),
        name="uproj",
    )(x, mod4, norm_g, w_bf)


N_LEVELS = 7
POW_ROWS = 24


def _s5prep_body(kk_ref, lr_ref, li_ref, ldt_ref, bc1_ref, bc2_ref, ca_ref, cb_ref, dv_ref,
                 wcat_ref, mtot_ref, v_ref, a1_ref, a2_ref):
    t_ = CHUNK
    kk = kk_ref[...]
    lane_blk = jnp.right_shift(lax.broadcasted_iota(I32, (SSM_GROUP, CHUNK_W), 1), 4)
    lane = lax.broadcasted_iota(I32, (1, LANES), 1)
    sgn = jnp.where(lane < LANES // 2, -1.0, 1.0)
    eye = jnp.where(lax.broadcasted_iota(I32, (LANES, LANES), 0)
                    == lax.broadcasted_iota(I32, (LANES, LANES), 1), 1.0, 0.0)
    mtot = jnp.zeros((CHUNK_W, CHUNK_W), F32)
    for d in range(2):
        lr = lr_ref[d]
        li = li_ref[d]
        dt = jnp.exp(ldt_ref[d])
        pm = jnp.exp(kk * (dt * lr))
        ang = kk * (dt * li)
        pr = pm * jnp.cos(ang)
        pi = pm * jnp.sin(ang)
        ar = pr[1:2, :]
        ai = pi[1:2, :]
        den = lr * lr + li * li
        nr = ar - 1.0
        kr = (nr * lr + ai * li) / den
        ki = (ai * lr - nr * li) / den
        bc1 = bc1_ref[d]
        bc2 = bc2_ref[d]
        bb1 = kr * bc1 + ki * bc2
        bb2 = kr * bc2 - ki * bc1
        ca = ca_ref[d]
        cb = cb_ref[d]
        cak = [ca * pr[k:k + 1, :] + cb * pi[k:k + 1, :] for k in range(t_ + 1)]
        lag_order = range(t_) if d == 0 else range(t_ - 1, -1, -1)
        cak_all = jnp.concatenate([cak[k] for k in lag_order], axis=0)
        kall_t = _dot3(bb1, cak_all, _NT)
        rows = []
        for j in range(t_):
            if d == 0:
                shift, keep = (SSM_GROUP * j) % CHUNK_W, lane_blk >= j
            else:
                shift, keep = (SSM_GROUP * (j + 1)) % CHUNK_W, lane_blk <= j
            r = pltpu.roll(kall_t, shift, 1) if shift else kall_t
            rows.append(jnp.where(keep, r, 0.0))
        mtot = mtot + jnp.concatenate(rows, axis=0)
        e_v = [i + 1 for i in range(t_)] if d == 0 else [t_ - i for i in range(t_)]
        v_t = jnp.concatenate([cak[e] for e in e_v], axis=0)
        v_ref[d] = _dot3(eye, v_t, _NT).astype(v_ref.dtype)
        e_w = [t_ - 1 - j for j in range(t_)] if d == 0 else list(range(t_))
        w = jnp.concatenate([bb1 * pr[e:e + 1, :] + bb2 * pi[e:e + 1, :] for e in e_w], axis=0)
        wcat_ref[:, d * LANES:(d + 1) * LANES] = w.astype(wcat_ref.dtype)
        a1_ref[d] = pr[t_:t_ + 8, :]
        a2_ref[d] = sgn * pi[t_:t_ + 8, :]
    diag = (lax.broadcasted_iota(I32, (CHUNK_W, CHUNK_W), 0)
            == lax.broadcasted_iota(I32, (CHUNK_W, CHUNK_W), 1))
    mtot = mtot + jnp.where(diag, dv_ref[...], 0.0)
    mtot_ref[...] = mtot.astype(mtot_ref.dtype)


def _s5prep(l, kk, lr_t, li_t, ldt, bc1, bc2, ca, cb, dv):
    g = lr_t.shape[2]
    spec5 = lambda r, c: pl.BlockSpec((None, 2, None, r, c), lambda i: (l, 0, i, 0, 0))
    return pl.pallas_call(
        _s5prep_body,
        out_shape=(jax.ShapeDtypeStruct((g, CHUNK_W, 2 * LANES), BF16),
                   jax.ShapeDtypeStruct((g, CHUNK_W, CHUNK_W), BF16),
                   jax.ShapeDtypeStruct((2, g, LANES, CHUNK_W), BF16),
                   jax.ShapeDtypeStruct((2, g, 8, LANES), F32),
                   jax.ShapeDtypeStruct((2, g, 8, LANES), F32)),
        grid=(g,),
        in_specs=[pl.BlockSpec((POW_ROWS, 1), lambda i: (0, 0)),
                  spec5(1, LANES), spec5(1, LANES), spec5(1, 1),
                  spec5(SSM_GROUP, LANES), spec5(SSM_GROUP, LANES),
                  spec5(SSM_GROUP, LANES), spec5(SSM_GROUP, LANES),
                  pl.BlockSpec((None, None, 1, CHUNK_W), lambda i: (l, i, 0, 0))],
        out_specs=(pl.BlockSpec((None, CHUNK_W, 2 * LANES), lambda i: (i, 0, 0)),
                   pl.BlockSpec((None, CHUNK_W, CHUNK_W), lambda i: (i, 0, 0)),
                   pl.BlockSpec((2, None, LANES, CHUNK_W), lambda i: (0, i, 0, 0)),
                   pl.BlockSpec((2, None, 8, LANES), lambda i: (0, i, 0, 0)),
                   pl.BlockSpec((2, None, 8, LANES), lambda i: (0, i, 0, 0))),
        compiler_params=_params(1, 32),
        name="s5prep",
    )(kk, lr_t, li_t, ldt, bc1, bc2, ca, cb, dv)


def _s5mix_body(*refs, seg, with_h0):
    if with_h0:
        u_ref, wcat_ref, m_ref, v_ref, a1_ref, a2_ref, h0_ref, y_ref = refs
    else:
        u_ref, wcat_ref, m_ref, v_ref, a1_ref, a2_ref, y_ref, fin_ref, fs_ref = refs
    k4, n, _ = u_ref.shape
    c = n // CHUNK
    gpt = LANES // SSM_GROUP
    b = pl.program_id(0)
    rowm = jnp.bitwise_and(lax.broadcasted_iota(I32, (1, c, 1), 1), seg - 1)
    shifts = [1 << j for j in range(seg.bit_length() - 1)]

    def swap(x):
        return pltpu.roll(x, LANES // 2, 2)

    def scan(s, d, g0):
        a1 = a1_ref[d, g0:g0 + gpt]
        a2 = a2_ref[d, g0:g0 + gpt]
        if with_h0:
            h0 = h0_ref[d, g0:g0 + gpt, pl.ds(b, 1), :]
        else:
            h0 = 0.0
        if d == 0:
            x = jnp.where(rowm == 0, h0, pltpu.roll(s, 1, 1))
        else:
            x = jnp.where(rowm == seg - 1, h0, pltpu.roll(s, c - 1, 1))
        for j, sh in enumerate(shifts):
            if d == 0:
                xs = jnp.where(rowm >= sh, pltpu.roll(x, sh, 1), 0.0)
            else:
                xs = jnp.where(rowm < seg - sh, pltpu.roll(x, c - sh, 1), 0.0)
            x = x + a1[:, j:j + 1, :] * xs + a2[:, j:j + 1, :] * swap(xs)
        return x

    for k in range(k4):
        g0 = k * gpt
        xk = jnp.concatenate([u_ref[k, pl.ds(t, c, stride=CHUNK), :] for t in range(CHUNK)], axis=0)
        xt = xk.T
        zs, ss = [], []
        for gg in range(gpt):
            rg = jnp.concatenate([xt[gg * SSM_GROUP:(gg + 1) * SSM_GROUP, t * c:(t + 1) * c]
                                  for t in range(CHUNK)], axis=0)
            z = rg.T.astype(BF16)
            zs.append(z)
            ss.append(_dot(z, wcat_ref[g0 + gg]))
        hins = []
        for d in range(2):
            sd = jnp.stack([s[:, d * LANES:(d + 1) * LANES] for s in ss], axis=0)
            hin = scan(sd, d, g0)
            hins.append(hin)
            if not with_h0:
                a1 = a1_ref[d, g0:g0 + gpt]
                a2 = a2_ref[d, g0:g0 + gpt]
                fin = a1[:, 0:1, :] * hin + a2[:, 0:1, :] * swap(hin) + sd
                first = seg - 1 if d == 0 else 0
                for gg in range(gpt):
                    fs_ref[...] = fin[gg]
                    fin_ref[d, g0 + gg] = fs_ref[pl.ds(first, c // seg, stride=seg), :]
        yts = []
        for gg in range(gpt):
            lhs = jnp.concatenate([zs[gg], hins[0][gg].astype(BF16), hins[1][gg].astype(BF16)], axis=1)
            rhs = jnp.concatenate([m_ref[g0 + gg], v_ref[0, g0 + gg], v_ref[1, g0 + gg]], axis=0)
            yts.append(_dot(lhs, rhs).T)
        xto = jnp.concatenate(
            [jnp.concatenate([yt[t * SSM_GROUP:(t + 1) * SSM_GROUP, :] for yt in yts], axis=0)
             for t in range(CHUNK)], axis=1)
        xo = xto.T
        for t in range(CHUNK):
            y_ref[k, pl.ds(t, c, stride=CHUNK), :] = xo[t * c:(t + 1) * c, :]


def _s5mix(u4, wcat, mtot, v, a1, a2, *, seg, h0=None):
    nb, k4, n, _ = u4.shape
    g = wcat.shape[0]
    c = n // CHUNK
    with_h0 = h0 is not None
    body = functools.partial(_s5mix_body, seg=seg, with_h0=with_h0)
    tile = pl.BlockSpec((None, k4, n, LANES), lambda b: (b, 0, 0, 0))
    res = lambda shape: pl.BlockSpec(shape, lambda b: (0,) * len(shape), pipeline_mode=pl.Buffered(1))
    in_specs = [tile, res((g, CHUNK_W, 2 * LANES)), res((g, CHUNK_W, CHUNK_W)), res((2, g, LANES, CHUNK_W)),
                res((2, g, 8, LANES)), res((2, g, 8, LANES))]
    args = [u4, wcat, mtot, v, a1, a2]
    y_shape = jax.ShapeDtypeStruct((nb, k4, n, LANES), F32)
    if with_h0:
        in_specs.append(res((2, g, h0.shape[2], LANES)))
        args.append(h0)
        out_shape, out_specs, scratch = y_shape, tile, []
    else:
        out_shape = (y_shape, jax.ShapeDtypeStruct((2, g, c // seg, LANES), F32))
        out_specs = (tile, pl.BlockSpec((2, g, c // seg, LANES), lambda b: (0, 0, 0, 0)))
        scratch = [pltpu.VMEM((c, LANES), F32)]
    return pl.pallas_call(
        body,
        out_shape=out_shape,
        grid=(nb,),
        in_specs=in_specs,
        out_specs=out_specs,
        scratch_shapes=scratch,
        compiler_params=_params(1, 48),
        name="s5mix",
    )(*args)


def _postmix_body(*refs, d_conv, aliased, nb_real, padded):
    if aliased:
        refs = refs[1:]
    if padded:
        @pl.when(pl.program_id(0) < nb_real)
        def _():
            _postmix_tile(*refs, d_conv=d_conv)

        @pl.when(pl.program_id(0) >= nb_real)
        def _():
            refs[-2][...] = jnp.zeros_like(refs[-2])
    else:
        _postmix_tile(*refs, d_conv=d_conv)


def _postmix_tile(cn_ref, y_ref, x_ref, mod_ref, wglu_ref, gs_ref, wout_ref, g2_ref, r2_ref, r1_ref,
                  xo_ref, h2_ref, lg_ref, *, d_conv):
    yg = jax.nn.gelu(jnp.concatenate([y_ref[k] for k in range(y_ref.shape[0])], axis=1))
    z = _dot(yg.astype(BF16), wglu_ref[...])
    s = yg * jax.nn.sigmoid(z)
    d_ssm = s.shape[-1]
    sn = [(_rms(s[:, lo:lo + HEAD_DIM]) * gs_ref[:, lo:lo + HEAD_DIM]).astype(BF16)
          for lo in range(0, d_ssm, HEAD_DIM)]
    mix = _dot(cn_ref[...], wout_ref[0:d_conv, :]) + _dot(jnp.concatenate(sn, axis=1), wout_ref[d_conv:, :])
    xn = x_ref[...] + mod_ref[2:3, :] * mix
    xo_ref[...] = xn
    h2 = _rms(xn) * g2_ref[...]
    h2 = h2 * (1.0 + mod_ref[4:5, :]) + mod_ref[3:4, :]
    hi, lo_ = _split_bf16(h2)
    tm, d = h2.shape
    for k in range(d // LANES):
        h2_ref[pl.ds(k, tm, stride=d // LANES), :] = h2[:, k * LANES:(k + 1) * LANES]
    d1 = _dot(hi, r2_ref[...])
    d2 = _dot(lo_, r1_ref[...])
    lg_ref[...] = d1[:, 0:LANES] + d1[:, LANES:2 * LANES] + d2


def _postmix(conv_n, y4, x, mod4, l, mod_row, wglu_bf, gs, wout_bf, norm2_g, r2, r1, *, tm, pseudo,
             table_tokens, table=None, tok0=0, pad_samples=0):
    nb, n, d = x.shape
    d_conv = conv_n.shape[-1]
    d_ssm = y4.shape[1] * LANES
    tok_rows = d // LANES
    aliased = table is not None
    body = functools.partial(_postmix_body, d_conv=d_conv, aliased=aliased, nb_real=nb, padded=pad_samples > 0)

    def clamp(b, i):
        return jnp.minimum(b, nb - 1), jnp.where(b < nb, i, n // tm - 1)

    def tok_map(b, i):
        b, i = clamp(b, i)
        return (b, i, 0)

    _, y_spec = _u_tiles(nb, n, d_ssm, tm, pseudo, clamp)
    tok = lambda w: pl.BlockSpec((None, tm, w), tok_map)
    lay = lambda r, c, **kw: pl.BlockSpec((None, r, c), lambda b, i: (l, 0, 0), **kw)
    in_specs = [tok(d_conv), y_spec, tok(d),
                pl.BlockSpec((None, None, N_MOD, d), lambda b, i: (l, mod_row(jnp.minimum(b, nb - 1)), 0, 0)),
                lay(d_ssm, d_ssm, pipeline_mode=pl.Buffered(1)),
                lay(1, d_ssm),
                lay(d_conv + d_ssm, d, pipeline_mode=pl.Buffered(1)),
                lay(1, d),
                lay(d, 2 * LANES, pipeline_mode=pl.Buffered(1)),
                lay(d, LANES, pipeline_mode=pl.Buffered(1))]
    args = [conv_n, y4, x, mod4, wglu_bf, gs, wout_bf, norm2_g, r2, r1]
    if aliased:
        in_specs.insert(0, pl.BlockSpec(memory_space=pl.ANY))
        args.insert(0, table)
    blk0 = tok0 // tm
    return pl.pallas_call(
        body,
        out_shape=(jax.ShapeDtypeStruct((nb, n, d), F32),
                   jax.ShapeDtypeStruct((table_tokens * tok_rows, LANES), F32),
                   jax.ShapeDtypeStruct((nb, n, LANES), F32)),
        grid=(nb + pad_samples, n // tm),
        in_specs=in_specs,
        out_specs=(tok(d),
                   pl.BlockSpec((tm * tok_rows, LANES), lambda b, i: (blk0 + b * (n // tm) + i, 0)),
                   tok(LANES)),
        input_output_aliases={0: 1} if aliased else {},
        compiler_params=_params(2, 56),
        name="postmix",
    )(*args)


BISECT_STEPS = 32

def _route_body(lg_ref, pos_ref, wv_ref, cum_ref, tri_ref, *, cap):
    nb, ne, n = lg_ref.shape
    rc = min(256, n)
    for r0 in range(0, n, rc):
        ri = lax.broadcasted_iota(I32, (rc, n), 0) + r0
        ci = lax.broadcasted_iota(I32, (rc, n), 1)
        tri_ref[r0:r0 + rc, :] = jnp.where(ri < ci, 1.0, 0.0).astype(BF16)
    lg = lg_ref[...]
    e = jnp.exp(lg - jnp.max(lg, axis=1, keepdims=True))
    aff = e / jnp.sum(e, axis=1, keepdims=True)
    capf = float(cap)

    def enough(t):
        return jnp.sum(jnp.where(aff >= t, 1.0, 0.0), axis=2, keepdims=True) >= capf

    hi = jnp.full((nb, ne, 1), 2.0, F32)
    for j in range(6, -1, -1):
        cand = hi * (2.0 ** -(1 << j))
        hi = jnp.where(enough(cand), hi, cand)
    half = hi * 0.5
    lo = jnp.where(enough(half), half, 0.0)
    for _ in range(BISECT_STEPS):
        mid = (lo + hi) * 0.5
        ok = enough(mid)
        lo = jnp.where(ok, mid, lo)
        hi = jnp.where(ok, hi, mid)
    gt = aff >= hi
    eq = jnp.logical_and(aff >= lo, aff < hi)
    need = capf - jnp.sum(jnp.where(gt, 1.0, 0.0), axis=2, keepdims=True)
    tri = tri_ref[...]
    eq_rank = _dot(jnp.where(eq, 1.0, 0.0).astype(BF16).reshape(nb * ne, n), tri).reshape(nb, ne, n)
    sel = jnp.logical_or(gt, jnp.logical_and(eq, eq_rank < need))
    pos = _dot(jnp.where(sel, 1.0, 0.0).astype(BF16).reshape(nb * ne, n), tri).reshape(nb, ne, n)
    pos_ref[...] = jnp.where(sel, pos.astype(I32), -1)
    wv_ref[...] = jnp.where(sel, aff, 0.0)
    cum_ref[...] = pos.astype(I32)


def _route(logits_t, cap):
    nb, ne, n = logits_t.shape
    body = functools.partial(_route_body, cap=cap)
    full = pl.BlockSpec((nb, ne, n), lambda i: (0, 0, 0))
    return pl.pallas_call(
        body,
        out_shape=(jax.ShapeDtypeStruct((nb, ne, n), I32),
                   jax.ShapeDtypeStruct((nb, ne, n), F32),
                   jax.ShapeDtypeStruct((nb, ne, n), I32)),
        grid=(1,),
        in_specs=[full],
        out_specs=(full, full, full),
        scratch_shapes=[pltpu.VMEM((n, n), BF16)],
        compiler_params=_params(1, 48),
        name="route",
    )(logits_t)


def _slotidx_body(pos_ref, o_ref, *, cap):
    n, ne = pos_ref.shape
    pos = pos_ref[...]
    tok = lax.broadcasted_iota(I32, (n, 1), 0).astype(F32)
    slot = lax.broadcasted_iota(I32, (n, cap), 1)
    for e in range(ne):
        hit = jnp.where(slot == pos[:, e:e + 1], tok, 0.0)
        o_ref[e:e + 1, :] = jnp.sum(hit, axis=0, keepdims=True).astype(I32)


def _slotidx(pos_cols, cap):
    nb, n, ne = pos_cols.shape
    return pl.pallas_call(
        functools.partial(_slotidx_body, cap=cap),
        out_shape=jax.ShapeDtypeStruct((nb, ne, cap), I32),
        grid=(nb,),
        in_specs=[pl.BlockSpec((None, n, ne), lambda b: (b, 0, 0))],
        out_specs=pl.BlockSpec((None, ne, cap), lambda b: (b, 0, 0)),
        compiler_params=_params(1, 32),
        name="slotidx",
    )(pos_cols)


def _ffn_body(idx_ref, tab_ref, wg_ref, wu_ref, wd_ref, y_ref, xbuf_ref, xs_ref, acc_ref, sem, *, tok_rows, nf):
    f = pl.program_id(2)
    blk = pl.program_id(0) * pl.num_programs(1) + pl.program_id(1)
    nblk = pl.num_programs(0) * pl.num_programs(1)
    step = blk * nf + f
    mh = xs_ref.shape[1]
    ch = mh // nf
    cur = jnp.bitwise_and(blk, 1)
    ring = jnp.bitwise_and(step, 1)

    def token_copy(blk_i, c, j, ring_i):
        tok = idx_ref[blk_i * mh + c * ch + j]
        src = tab_ref.at[pl.ds(pl.multiple_of(tok * tok_rows, tok_rows), tok_rows), :]
        dst = xbuf_ref.at[ring_i, pl.ds(pl.multiple_of(j * tok_rows, tok_rows), tok_rows), :]
        return pltpu.make_async_copy(src, dst, sem.at[ring_i])

    def wait_chunk(ring_i):
        pltpu.make_async_copy(tab_ref.at[pl.ds(0, ch * tok_rows), :], xbuf_ref.at[ring_i], sem.at[ring_i]).wait()

    def convert(ring_i, xs_slot, c):
        rows = pl.ds(pl.multiple_of(c * ch, ch), ch)
        for k in range(tok_rows):
            w = xbuf_ref[ring_i, pl.ds(k, ch, stride=tok_rows), :]
            xs_ref[xs_slot, rows, k * LANES:(k + 1) * LANES] = w.astype(BF16)

    def issue_loop(blk_i, c, ring_i):
        def issue(j, carry):
            token_copy(blk_i, c, j, ring_i).start()
            return carry
        lax.fori_loop(0, ch, issue, 0)

    @pl.when(step == 0)
    def _():
        for c in range(nf):
            issue_loop(0, c, 0)
            wait_chunk(0)
            convert(0, 0, c)
        issue_loop(1 % nblk, 0, 1)

    @pl.when(f == 0)
    def _():
        acc_ref[...] = jnp.zeros_like(acc_ref)

    q = step + nf + 1
    qb = q // nf
    qb = jnp.where(qb < nblk, qb, 0)
    for j in range(ch):
        token_copy(qb, q % nf, j, ring).start(priority=j % 2)

    xs = xs_ref[cur]
    g = _dot(xs, wg_ref[...].astype(BF16))
    u = _dot(xs, wu_ref[...].astype(BF16))
    act = (jax.nn.silu(g) * u).astype(BF16)
    acc_ref[...] += _dot(act, wd_ref[...].astype(BF16))

    wait_chunk(1 - ring)
    convert(1 - ring, 1 - cur, f)

    @pl.when(f == nf - 1)
    def _():
        y_ref[...] = acc_ref[...].astype(y_ref.dtype)

    @pl.when(step == nblk * nf - 1)
    def _():
        wait_chunk(ring)


def _ffn(slot_tok, table, l, w_gate, w_up, w_down, *, tf, m_split):
    _, ne, d, dff = w_gate.shape
    tok_rows = d // LANES
    m = slot_tok.shape[0] // ne
    mh = m // m_split
    nf = dff // tf
    assert mh % nf == 0 and table.shape[1] == LANES
    body = functools.partial(_ffn_body, tok_rows=tok_rows, nf=nf)
    grid_spec = pltpu.PrefetchScalarGridSpec(
        num_scalar_prefetch=1,
        grid=(m_split, ne, nf),
        in_specs=[pl.BlockSpec(memory_space=pl.ANY),
                  pl.BlockSpec((None, None, d, tf), lambda h, e, f, idx: (l, e, 0, f)),
                  pl.BlockSpec((None, None, d, tf), lambda h, e, f, idx: (l, e, 0, f)),
                  pl.BlockSpec((None, None, tf, d), lambda h, e, f, idx: (l, e, f, 0))],
        out_specs=pl.BlockSpec((None, mh, d), lambda h, e, f, idx: (e, h, 0)),
        scratch_shapes=[pltpu.VMEM((2, mh // nf * tok_rows, LANES), F32),
                        pltpu.VMEM((2, mh, d), BF16),
                        pltpu.VMEM((mh, d), F32),
                        pltpu.SemaphoreType.DMA((2,))])
    return pl.pallas_call(
        body,
        out_shape=jax.ShapeDtypeStruct((ne, m, d), BF16),
        grid_spec=grid_spec,
        compiler_params=_params(3, 56),
        name="ffn",
    )(slot_tok, table, w_gate, w_up, w_down)


COMBINE_WIN = 64


def _combine_body(*refs, cap, final):
    if final:
        st_ref, y_ref, pos_ref, wv_ref, x_ref, mod_ref, gf_ref, o_ref = refs
    else:
        st_ref, y_ref, pos_ref, wv_ref, x_ref, mod_ref, o_ref = refs
    ne = y_ref.shape[0]
    tt = x_ref.shape[0]
    win = COMBINE_WIN
    nt1 = pl.num_programs(1) + 1
    base = pl.program_id(0) * ne * nt1 + pl.program_id(1)
    pos = pos_ref[...]
    wv = wv_ref[...]

    def finish(moe):
        xn = x_ref[...] + mod_ref[5:6, :] * moe
        if final:
            xn = _rms(xn) * gf_ref[...]
        o_ref[...] = xn

    w0s = []
    fits = None
    for e in range(ne):
        s0 = st_ref[base + e * nt1]
        s1 = st_ref[base + e * nt1 + 1]
        w0 = jnp.minimum(jnp.left_shift(jnp.right_shift(s0, 4), 4), cap - win)
        ok = s1 - w0 <= win
        fits = ok if fits is None else jnp.logical_and(fits, ok)
        w0s.append(pl.multiple_of(w0, 16))

    @pl.when(fits)
    def _():
        lane = lax.broadcasted_iota(I32, (tt, LANES), 1)
        low = lane < win
        pieces, rows = [], []
        for e in range(0, ne, 2):
            tgt = jnp.where(low, lane + w0s[e], lane + (w0s[e + 1] - win))
            p = jnp.where(low, pos[:, e:e + 1], pos[:, e + 1:e + 2])
            w = jnp.where(low, wv[:, e:e + 1], wv[:, e + 1:e + 2])
            pieces.append(jnp.where(tgt == p, w, 0.0).astype(BF16))
            rows.append(y_ref[e, pl.ds(w0s[e], win), :])
            rows.append(y_ref[e + 1, pl.ds(w0s[e + 1], win), :])
        finish(_dot(jnp.concatenate(pieces, axis=1), jnp.concatenate(rows, axis=0)))

    @pl.when(jnp.logical_not(fits))
    def _():
        slot = lax.broadcasted_iota(I32, (tt, cap), 1)
        pieces = [jnp.where(slot == pos[:, e:e + 1], wv[:, e:e + 1], 0.0).astype(BF16) for e in range(ne)]
        finish(_dot(jnp.concatenate(pieces, axis=1), y_ref[...].reshape(ne * cap, y_ref.shape[-1])))


def _combine(starts, y, blk0, pos_cols, wv_cols, x, mod4, l, mod_row, cap, final_g=None, *, tt):
    nb, n, d = x.shape
    ne = y.shape[0]
    final = final_g is not None
    assert 2 * COMBINE_WIN == LANES and ne % 2 == 0 and cap % 16 == 0 and cap >= COMBINE_WIN
    body = functools.partial(_combine_body, cap=cap, final=final)
    in_specs = [pl.BlockSpec((ne, cap, d), lambda b, i, st: (0, blk0 + b, 0)),
                pl.BlockSpec((None, tt, ne), lambda b, i, st: (b, i, 0)),
                pl.BlockSpec((None, tt, ne), lambda b, i, st: (b, i, 0)),
                pl.BlockSpec((None, tt, d), lambda b, i, st: (b, i, 0)),
                pl.BlockSpec((None, None, N_MOD, d), lambda b, i, st: (l, mod_row(b), 0, 0))]
    args = [starts, y, pos_cols, wv_cols, x, mod4]
    if final:
        in_specs.append(pl.BlockSpec((1, d), lambda b, i, st: (0, 0)))
        args.append(final_g)
    grid_spec = pltpu.PrefetchScalarGridSpec(
        num_scalar_prefetch=1,
        grid=(nb, n // tt),
        in_specs=in_specs,
        out_specs=pl.BlockSpec((None, tt, d), lambda b, i, st: (b, i, 0)))
    return pl.pallas_call(
        body,
        out_shape=jax.ShapeDtypeStruct((nb, n, d), F32),
        grid_spec=grid_spec,
        compiler_params=_params(2, 56),
        name="combine",
    )(*args)


def kernel(x, c, ctx, c_ctx, ada_w, ada_b, norm1_g, w_in, conv_w, ssm_lam_re, ssm_lam_im, ssm_log_dt,
           ssm_b_re, ssm_b_im, ssm_c_re, ssm_c_im, ssm_d, ssm_w_glu, out_norm_conv_g, out_norm_ssm_g,
           w_out, norm2_g, router_w, exp_w_gate, exp_w_up, exp_w_down, final_norm_g):
    bsz, n, d = x.shape
    nc = ctx.shape[1]
    depth = ada_w.shape[0]
    d_conv = conv_w.shape[-1]
    d_ssm = ssm_d.shape[-1]
    g = d_ssm // SSM_GROUP
    p = ssm_lam_re.shape[-1]
    ne = router_w.shape[-1]
    dff = exp_w_gate.shape[-1]
    cap = EC_FACTOR * n // ne
    cap_c = EC_FACTOR * nc // ne
    nbc = bsz * nc
    assert bsz == 8 and 2 * p == LANES and ssm_b_re.shape[-1] == SSM_GROUP
    assert nbc == n and bsz * cap_c == cap
    assert n // CHUNK == 1 << N_LEVELS and nc % CHUNK == 0 and (nc // CHUNK) & (nc // CHUNK - 1) == 0
    assert d_conv % HEAD_DIM == 0 and d_ssm % HEAD_DIM == 0 and (3 * d_conv) % d_ssm == 0
    assert N_MOD * d == ada_w.shape[-1] and ne <= LANES

    tm = min(512, n)
    tt = min(256, n)
    tf = min(256, dff)

    rows = 16
    cvec = jnp.zeros((rows, d), F32).at[:bsz].set(c).at[bsz].set(c_ctx)
    mod4 = _ada(cvec, ada_w, ada_b).reshape(depth, rows, N_MOD, d)
    lat_row = lambda b: b
    ctx_row = lambda b: bsz

    w_in_bf = w_in.astype(BF16)
    w_out_bf = w_out.astype(BF16)
    w_glu_bf = ssm_w_glu.astype(BF16)
    rw_hi, rw_lo = _split_bf16(router_w)
    zpad = jnp.zeros((depth, d, LANES - ne), BF16)
    r1 = jnp.concatenate([rw_hi, zpad], axis=-1)
    r2 = jnp.concatenate([rw_hi, zpad, rw_lo, zpad], axis=-1)

    tile2 = lambda a: jnp.concatenate([a, a], axis=-1)
    lr_t = tile2(ssm_lam_re)[:, :, :, None, :]
    li_t = tile2(ssm_lam_im)[:, :, :, None, :]
    ldt = ssm_log_dt[:, :, :, None, None]
    brt = jnp.swapaxes(ssm_b_re, -1, -2)
    bit = jnp.swapaxes(ssm_b_im, -1, -2)
    bc1 = jnp.concatenate([brt, bit], axis=-1)
    bc2 = jnp.concatenate([-bit, brt], axis=-1)
    ca = jnp.concatenate([ssm_c_re, -ssm_c_im], axis=-1)
    cb = jnp.concatenate([-ssm_c_im, -ssm_c_re], axis=-1)
    dv = jnp.tile(ssm_d.reshape(depth, g, 1, SSM_GROUP), (1, 1, 1, CHUNK))

    g1n = norm1_g.reshape(depth, 1, d)
    g2n = norm2_g.reshape(depth, 1, d)
    gcn = out_norm_conv_g.reshape(depth, 1, d_conv)
    gsn = out_norm_ssm_g.reshape(depth, 1, d_ssm)
    gfin = final_norm_g.reshape(1, d)

    kk = jnp.asarray([float(k) for k in range(CHUNK + 1)]
                     + [float(CHUNK << j) for j in range(1, N_LEVELS)] + [0.0], F32).reshape(POW_ROWS, 1)

    xl = x
    xc = ctx
    for l in range(depth):
        last = l == depth - 1
        conv_l, u_lat = _inproj(xl, mod4, l, lat_row, g1n, w_in_bf, conv_w, gcn, rowlen=GRID_W, tm=tm,
                                pseudo=False)
        if last:
            u_ctx = _uproj(xc, mod4, l, ctx_row, g1n, w_in_bf, d_ssm, tm=nc)
        else:
            conv_c, u_ctx = _inproj(xc, mod4, l, ctx_row, g1n, w_in_bf, conv_w, gcn, rowlen=nc, tm=nc,
                                    pseudo=True)

        wcat, mtot, vmat, a1, a2 = _s5prep(l, kk, lr_t, li_t, ldt, bc1, bc2, ca, cb, dv)
        y_ctx, h0 = _s5mix(u_ctx, wcat, mtot, vmat, a1, a2, seg=nc // CHUNK)
        y_lat = _s5mix(u_lat, wcat, mtot, vmat, a1, a2, seg=n // CHUNK, h0=h0)

        n_tab = bsz * n + (0 if last else nbc)
        xl, table, lg = _postmix(conv_l, y_lat, xl, mod4, l, lat_row, w_glu_bf, gsn, w_out_bf, g2n, r2, r1,
                                 tm=tm, pseudo=False, table_tokens=n_tab, pad_samples=0 if last else nbc // n)
        lg_t = jnp.swapaxes(lg[:, :, :ne], 1, 2)
        pos, wv, cum = _route(lg_t, cap)
        capcol = jnp.full((bsz, ne, 1), cap, I32)
        starts = jnp.concatenate([cum[:, :, ::tt], capcol], axis=2).reshape(-1)
        pos_cols = jnp.swapaxes(pos, 1, 2)
        wv_cols = jnp.swapaxes(wv, 1, 2)
        offs_n = (jnp.arange(bsz, dtype=I32) * n)[:, None, None]
        slot_tok = jnp.swapaxes(_slotidx(pos_cols, cap) + offs_n, 0, 1).reshape(ne, bsz * cap)

        if not last:
            xc, table, lgc = _postmix(conv_c, y_ctx, xc, mod4, l, ctx_row, w_glu_bf, gsn, w_out_bf, g2n,
                                      r2, r1, tm=nc, pseudo=True, table_tokens=n_tab, table=table, tok0=bsz * n)
            lgc_t = jnp.swapaxes(lgc[:, :, :ne], 1, 2)
            posc, wvc, cumc = _route(lgc_t, cap_c)
            offs = (jnp.arange(bsz, dtype=I32) * cap_c)[:, None, None]
            posc = jnp.where(posc >= 0, posc + offs, -1)
            posc_cols = jnp.swapaxes(posc, 1, 2).reshape(1, nbc, ne)
            wvc_cols = jnp.swapaxes(wvc, 1, 2).reshape(1, nbc, ne)
            startsc = jnp.swapaxes(cumc[:, :, ::tt] + offs, 0, 1).reshape(ne, -1)
            startsc = jnp.concatenate([startsc, jnp.full((ne, 1), cap, I32)], axis=1).reshape(-1)
            slot_tok = jnp.concatenate([slot_tok, _slotidx(posc_cols, cap)[0] + bsz * n], axis=1)

        m_split = 2
        mh = slot_tok.shape[1] // m_split
        slot_tok = jnp.swapaxes(slot_tok.reshape(ne, m_split, mh), 0, 1).reshape(-1)
        y_exp = _ffn(slot_tok, table, l, exp_w_gate, exp_w_up, exp_w_down, tf=tf, m_split=m_split)
        xl = _combine(starts, y_exp, 0, pos_cols, wv_cols, xl, mod4, l, lat_row, cap,
                      gfin if last else None, tt=tt)
        if not last:
            xc = _combine(startsc, y_exp, bsz, posc_cols, wvc_cols, xc.reshape(1, nbc, d), mod4, l, ctx_row, cap,
                          tt=tt).reshape(bsz, nc, d)
    return xl
```
